```python
import math
import jax, jax.numpy as jnp
from jax import lax
import numpy as np

D_MODEL = 1024
BATCH = 4
SEQ = 4096
DEPTH = 2

HEAD_DIM = 64
HEADS_PER_GROUP = D_MODEL // (2 * HEAD_DIM)
ATTN_GROUPS = ((128, 1), (512, 4), (2048, 16))
N_GROUPS = len(ATTN_GROUPS)
N_ATTN_HEADS = N_GROUPS * HEADS_PER_GROUP
GROUP_WIDTH = HEADS_PER_GROUP * HEAD_DIM
ATTN_OUT = GROUP_WIDTH
BLOCK = 128
REL_BUCKETS = 32
REL_MAX_DIST = 2048
NEG = -1e30
SSM_WIDTH = D_MODEL // 2
SSM_GROUP = 16
SSM_GROUPS = SSM_WIDTH // SSM_GROUP
SSM_STATE = 64
DT_MIN = 1e-3
DT_MAX = 1e-1
D_FF = 2816
QKV_COLS = 3 * N_GROUPS * GROUP_WIDTH
IN_COLS = QKV_COLS + SSM_WIDTH + 2 * D_MODEL
N_MOD = 9
EPS = 1e-6

kernel_name = "hybrid_dilated_attn_s5_macaron_adaln"


def rmsnorm(x, g):
    xf = x.astype(jnp.float32)
    y = xf * lax.rsqrt(jnp.mean(xf * xf, axis=-1, keepdims=True) + EPS)
    return (y * g.astype(jnp.float32)).astype(x.dtype)


def modulate(h, shift, scale):
    return h * (1 + scale) + shift


def swiglu(h, w_in, w_out):
    a, b = jnp.split(h @ w_in, 2, axis=-1)
    return (jax.nn.silu(a) * b) @ w_out


def t5_bucket(dist):
    max_exact = REL_BUCKETS // 2
    d = np.maximum(dist, max_exact).astype(np.float32)
    large = max_exact + (np.log(d / max_exact) / np.log(REL_MAX_DIST / max_exact)
                         * (REL_BUCKETS - max_exact)).astype(np.int32)
    large = np.minimum(large, REL_BUCKETS - 1)
    return np.where(dist < max_exact, dist, large).astype(np.int32)


def dilated_window_attention(q, k, v, bias_table, window, dilation):
    Bsz, S, H, E = q.shape
    steps = window // dilation
    L = S // dilation
    nb = -(-L // BLOCK)
    Lp = nb * BLOCK

    def to_sub(t):
        return t.reshape(Bsz, L, dilation, H, E).transpose(0, 2, 1, 3, 4)

    qs = jnp.pad(to_sub(q), ((0, 0), (0, 0), (0, Lp - L), (0, 0), (0, 0)))
    qs = qs.reshape(Bsz, dilation, nb, BLOCK, H, E)

    def key_blocks(t):
        t = jnp.pad(to_sub(t), ((0, 0), (0, 0), (BLOCK, Lp - L), (0, 0), (0, 0)))
        t = t.reshape(Bsz, dilation, nb + 1, BLOCK, H, E)
        return jnp.concatenate([t[:, :, :-1], t[:, :, 1:]], axis=3)

    kb, vb = key_blocks(k), key_blocks(v)

    qi = np.arange(BLOCK)[:, None]
    kj = np.arange(2 * BLOCK)[None, :]
    rel = BLOCK + qi - kj
    band = (rel >= 0) & (rel <= steps)
    kpos = (np.arange(nb)[:, None, None] - 1) * BLOCK + kj[None]
    mask = band[None] & (kpos >= 0)
    bucket = t5_bucket(np.clip(rel, 0, None) * dilation)
    bias = jnp.transpose(bias_table[bucket], (2, 0, 1)).astype(jnp.float32)

    logits = jnp.einsum('bdnqhe,bdnkhe->bdnhqk', qs, kb).astype(jnp.float32) * (E ** -0.5) + bias
    logits = jnp.where(mask[:, None], logits, NEG)
    m = jnp.max(logits, axis=-1, keepdims=True)
    p = jnp.exp(logits - m)
    s = jnp.sum(p, axis=-1, keepdims=True)
    out = jnp.einsum('bdnhqk,bdnkhe->bdnqhe', (p / s).astype(v.dtype), vb)
    lse = (m + jnp.log(s))[..., 0]

    out = out.reshape(Bsz, dilation, Lp, H, E)[:, :, :L]
    out = out.transpose(0, 2, 1, 3, 4).reshape(Bsz, S, H, E)
    lse = lse.transpose(0, 1, 2, 4, 3).reshape(Bsz, dilation, Lp, H)[:, :, :L]
    lse = lse.transpose(0, 2, 1, 3).reshape(Bsz, S, H)
    return out, lse


def s5_branch(u, lam_re, lam_im, log_dt, b_re, b_im, c_re, c_im, d_skip, w_glu):
    Bsz, S, _ = u.shape
    u = u.reshape(Bsz, S, SSM_GROUPS, SSM_GROUP)
    dt = jnp.exp(log_dt)[:, None]
    mag = jnp.exp(lam_re * dt)
    ang = lam_im * dt
    a_re = mag * jnp.cos(ang)
    a_im = mag * jnp.sin(ang)
    den = lam_re * lam_re + lam_im * lam_im
    f_re = ((a_re - 1) * lam_re + a_im * lam_im) / den
    f_im = (a_im * lam_re - (a_re - 1) * lam_im) / den
    bb_re = f_re[..., None] * b_re - f_im[..., None] * b_im
    bb_im = f_re[..., None] * b_im + f_im[..., None] * b_re
    bu_re = jnp.einsum('bsgc,gpc->bsgp', u, bb_re)
    bu_im = jnp.einsum('bsgc,gpc->bsgp', u, bb_im)
    shape_a = (1, S, SSM_GROUPS, SSM_STATE)
    ar = jnp.broadcast_to(a_re[None, None], shape_a)
    ai = jnp.broadcast_to(a_im[None, None], shape_a)

    def combine(e1, e2):
        a1r, a1i, b1r, b1i = e1
        a2r, a2i, b2r, b2i = e2
        return (a2r * a1r - a2i * a1i,
                a2r * a1i + a2i * a1r,
                a2r * b1r - a2i * b1i + b2r,
                a2r * b1i + a2i * b1r + b2i)

    _, _, xr, xi = lax.associative_scan(combine, (ar, ai, bu_re, bu_im), axis=1)
    y = (jnp.einsum('bsgp,gcp->bsgc', xr, c_re) - jnp.einsum('bsgp,gcp->bsgc', xi, c_im)
         + d_skip * u)
    y = jax.nn.gelu(y.reshape(Bsz, S, SSM_WIDTH))
    ga, gb = jnp.split(y @ w_glu, 2, axis=-1)
    return ga * jax.nn.sigmoid(gb)


def mixing(h, w_in, rel_bias, lam_re, lam_im, log_dt, b_re, b_im, c_re, c_im, d_skip,
           w_glu, w_attn_proj, w_out):
    Bsz, S, _ = h.shape
    proj = h @ w_in
    qkv, u, gates = jnp.split(proj, [QKV_COLS, QKV_COLS + SSM_WIDTH], axis=-1)
    qkv = qkv.reshape(Bsz, S, 3, N_GROUPS, HEADS_PER_GROUP, HEAD_DIM)
    outs, lses = [], []
    for g, (window, dilation) in enumerate(ATTN_GROUPS):
        tbl = rel_bias[:, g * HEADS_PER_GROUP:(g + 1) * HEADS_PER_GROUP]
        o, l = dilated_window_attention(qkv[:, :, 0, g], qkv[:, :, 1, g], qkv[:, :, 2, g],
                                        tbl, window, dilation)
        outs.append(o)
        lses.append(l)
    w = jax.nn.softmax(jnp.stack(lses), axis=0)[..., None]
    o = jnp.sum(w * jnp.stack(outs).astype(jnp.float32), axis=0).astype(h.dtype)
    y_attn = o.reshape(Bsz, S, ATTN_OUT) @ w_attn_proj
    y_ssm = s5_branch(u, lam_re, lam_im, log_dt, b_re, b_im, c_re, c_im, d_skip, w_glu)
    g_attn, g_ssm = jnp.split(jax.nn.sigmoid(gates), 2, axis=-1)
    return (g_attn * y_attn + g_ssm * y_ssm) @ w_out


def setup_inputs(seed: int = 0) -> dict:
    key = jax.random.key(seed)
    ks = jax.random.split(key, 26)
    f32 = jnp.float32

    def nrm(k, shape, scale):
        return jax.random.normal(k, shape, f32) * scale

    G, P = SSM_GROUPS, SSM_STATE
    return {
        'x': nrm(ks[0], (BATCH, SEQ, D_MODEL), 1.0),
        'c': nrm(ks[1], (BATCH, D_MODEL), 1.0),
        'w_ada': nrm(ks[2], (DEPTH, D_MODEL, N_MOD * D_MODEL), 0.5 * D_MODEL ** -0.5),
        'b_ada': nrm(ks[3], (DEPTH, N_MOD * D_MODEL), 0.02),
        'norm_ffn1': 1.0 + nrm(ks[4], (DEPTH, D_MODEL), 0.05),
        'w_ffn1_in': nrm(ks[5], (DEPTH, D_MODEL, 2 * D_FF), D_MODEL ** -0.5),
        'w_ffn1_out': nrm(ks[6], (DEPTH, D_FF, D_MODEL), D_FF ** -0.5),
        'norm_mix': 1.0 + nrm(ks[7], (DEPTH, D_MODEL), 0.05),
        'w_in': nrm(ks[8], (DEPTH, D_MODEL, IN_COLS), D_MODEL ** -0.5),
        'rel_bias': nrm(ks[9], (REL_BUCKETS, N_ATTN_HEADS), 0.5),
        'lam_re': -0.5 + nrm(ks[10], (DEPTH, G, P), 0.01),
        'lam_im': jnp.tile(math.pi * jnp.arange(P, dtype=f32), (DEPTH, G, 1)),
        'log_dt': jax.random.uniform(ks[11], (DEPTH, G), f32, math.log(DT_MIN), math.log(DT_MAX)),
        'b_re': nrm(ks[12], (DEPTH, G, P, SSM_GROUP), (2 * SSM_GROUP) ** -0.5),
        'b_im': nrm(ks[13], (DEPTH, G, P, SSM_GROUP), (2 * SSM_GROUP) ** -0.5),
        'c_re': nrm(ks[14], (DEPTH, G, SSM_GROUP, P), SSM_STATE ** -0.5),
        'c_im': nrm(ks[15], (DEPTH, G, SSM_GROUP, P), SSM_STATE ** -0.5),
        'd_skip': nrm(ks[16], (DEPTH, G, SSM_GROUP), 1.0),
        'w_glu': nrm(ks[17], (DEPTH, SSM_WIDTH, 2 * D_MODEL), SSM_WIDTH ** -0.5),
        'w_attn_proj': nrm(ks[18], (DEPTH, ATTN_OUT, D_MODEL), ATTN_OUT ** -0.5),
        'w_out': nrm(ks[19], (DEPTH, D_MODEL, D_MODEL), D_MODEL ** -0.5),
        'norm_ffn2': 1.0 + nrm(ks[20], (DEPTH, D_MODEL), 0.05),
        'w_ffn2_in': nrm(ks[21], (DEPTH, D_MODEL, 2 * D_FF), D_MODEL ** -0.5),
        'w_ffn2_out': nrm(ks[22], (DEPTH, D_FF, D_MODEL), D_FF ** -0.5),
        'final_norm': 1.0 + nrm(ks[23], (D_MODEL,), 0.05),
    }


def reference(x, c, w_ada, b_ada, norm_ffn1, w_ffn1_in, w_ffn1_out, norm_mix, w_in,
              rel_bias, lam_re, lam_im, log_dt, b_re, b_im, c_re, c_im, d_skip, w_glu,
              w_attn_proj, w_out, norm_ffn2, w_ffn2_in, w_ffn2_out, final_norm):
    Bsz = x.shape[0]
    c_act = jax.nn.silu(c)
    for l in range(DEPTH):
        mod = (c_act @ w_ada[l] + b_ada[l]).reshape(Bsz, N_MOD, 1, D_MODEL)
        sh1, sc1, g1, sh2, sc2, g2, sh3, sc3, g3 = [mod[:, i] for i in range(N_MOD)]
        h = modulate(rmsnorm(x, norm_ffn1[l]), sh1, sc1)
        x = x + 0.5 * g1 * swiglu(h, w_ffn1_in[l], w_ffn1_out[l])
        h = modulate(rmsnorm(x, norm_mix[l]), sh2, sc2)
        x = x + g2 * mixing(h, w_in[l], rel_bias, lam_re[l], lam_im[l], log_dt[l],
                            b_re[l], b_im[l], c_re[l], c_im[l], d_skip[l], w_glu[l],
                            w_attn_proj[l], w_out[l])
        h = modulate(rmsnorm(x, norm_ffn2[l]), sh3, sc3)
        x = x + 0.5 * g3 * swiglu(h, w_ffn2_in[l], w_ffn2_out[l])
    return rmsnorm(x, final_norm)
```

```python
import functools
import math

import jax
import jax.numpy as jnp
import numpy as np
from jax import lax
from jax.experimental import pallas as pl
from jax.experimental.pallas import tpu as pltpu

D_MODEL = 1024
BATCH = 4
SEQ = 4096
DEPTH = 2
HEAD_DIM = 64
HEADS_PER_GROUP = 8
ATTN_GROUPS = ((128, 1), (512, 4), (2048, 16))
N_GROUPS = len(ATTN_GROUPS)
GROUP_WIDTH = HEADS_PER_GROUP * HEAD_DIM
BLOCK = 128
REL_BUCKETS = 32
REL_MAX_DIST = 2048
NEG = -1e30
SSM_WIDTH = 512
SSM_GROUP = 16
SSM_GROUPS = 32
SSM_STATE = 64
D_FF = 2816
QKV_COLS = 3 * N_GROUPS * GROUP_WIDTH
IN_COLS = QKV_COLS + SSM_WIDTH + 2 * D_MODEL
N_MOD = 9
EPS = 1e-6

LANES = 128
SUBLANES = 8
MXU_DIM = 256
VMEM_LIMIT = 56 * 1024 * 1024

ADA_TN = 1152
FFN_TM = 512
FFN_TF = MXU_DIM
PROJ_TM = 1024
PROJ_TN = 1024
PROJ_NBLK = IN_COLS // PROJ_TN
MIX_TM = 512
HEAD_PAIRS = GROUP_WIDTH // LANES
SSM_TC = 512
SSM_PITCH = SSM_TC + SUBLANES
SSM_SLABS = SSM_WIDTH // LANES
SLAB_STATES = (SSM_GROUPS // SSM_SLABS) * SSM_STATE
SLAB_TILES = SLAB_STATES // LANES

U_COL0 = QKV_COLS // LANES
GATE_COL0 = (QKV_COLS + SSM_WIDTH) // D_MODEL

BF16 = jnp.bfloat16
F32 = jnp.float32


def _cparams(sem):
    return pltpu.CompilerParams(dimension_semantics=sem, vmem_limit_bytes=VMEM_LIMIT)


def _resident(shape, index_map):
    return pl.BlockSpec(shape, index_map, pipeline_mode=pl.Buffered(1))


def _bdot(a, b):
    return jnp.dot(a.astype(BF16), b.astype(BF16), preferred_element_type=F32)


def _norm_mod(x, g, shift, scale):
    ms = jnp.mean(x * x, axis=-1, keepdims=True)
    y = x * lax.rsqrt(ms + EPS) * g
    return y * (1.0 + scale) + shift


def _ada_kernel(c_ref, w_ref, b_ref, o_ref):
    c = c_ref[...]
    ca = c * jax.nn.sigmoid(c)
    o_ref[...] = _bdot(ca, w_ref[...]) + b_ref[...]


def _ada(c_pad, w_ada, b_ada):
    n = N_MOD * D_MODEL
    return pl.pallas_call(
        _ada_kernel,
        grid=(DEPTH, n // ADA_TN),
        in_specs=[
            pl.BlockSpec((SUBLANES, D_MODEL), lambda l, j: (0, 0)),
            pl.BlockSpec((None, D_MODEL, ADA_TN), lambda l, j: (l, 0, j)),
            pl.BlockSpec((None, 1, ADA_TN), lambda l, j: (l, 0, j)),
        ],
        out_specs=pl.BlockSpec((None, SUBLANES, ADA_TN), lambda l, j: (l, 0, j)),
        out_shape=jax.ShapeDtypeStruct((DEPTH, SUBLANES, n), F32),
        compiler_params=_cparams(("arbitrary", "arbitrary")),
        name="ada_mod",
    )(c_pad, w_ada, b_ada.reshape(DEPTH, 1, n))


def _ffn_kernel(x_ref, mod_ref, g_ref, win_ref, wout_ref, *rest, row0, final):
    if final:
        fn_ref, o_ref, act_ref = rest
    else:
        o_ref, act_ref = rest
    x = x_ref[...]
    h = _norm_mod(x, g_ref[...], mod_ref[row0:row0 + 1, :], mod_ref[row0 + 1:row0 + 2, :])
    hb = h.astype(BF16)
    for j in range(D_FF // FFN_TF):
        a = jnp.dot(hb, win_ref[:, j * FFN_TF:(j + 1) * FFN_TF], preferred_element_type=F32)
        b = jnp.dot(hb, win_ref[:, D_FF + j * FFN_TF:D_FF + (j + 1) * FFN_TF],
                    preferred_element_type=F32)
        act_ref[:, j * FFN_TF:(j + 1) * FFN_TF] = (a * jax.nn.sigmoid(a) * b).astype(BF16)
    y = jnp.dot(act_ref[...], wout_ref[...], preferred_element_type=F32)
    out = x + (0.5 * mod_ref[row0 + 2:row0 + 3, :]) * y
    if final:
        ms = jnp.mean(out * out, axis=-1, keepdims=True)
        out = out * lax.rsqrt(ms + EPS) * fn_ref[...]
    o_ref[...] = out


def _ffn(x, mod_l, norm_g, w_in, w_out, row0, final_norm=None):
    final = final_norm is not None
    in_specs = [
        pl.BlockSpec((None, FFN_TM, D_MODEL), lambda b, i: (b, i, 0)),
        pl.BlockSpec((None, N_MOD, D_MODEL), lambda b, i: (b, 0, 0)),
        pl.BlockSpec((1, D_MODEL), lambda b, i: (0, 0)),
        _resident((D_MODEL, 2 * D_FF), lambda b, i: (0, 0)),
        _resident((D_FF, D_MODEL), lambda b, i: (0, 0)),
    ]
    args = [x, mod_l, norm_g.reshape(1, D_MODEL), w_in, w_out]
    if final:
        in_specs.append(pl.BlockSpec((1, D_MODEL), lambda b, i: (0, 0)))
        args.append(final_norm.reshape(1, D_MODEL))
    return pl.pallas_call(
        functools.partial(_ffn_kernel, row0=row0, final=final),
        grid=(BATCH, SEQ // FFN_TM),
        in_specs=in_specs,
        out_specs=pl.BlockSpec((None, FFN_TM, D_MODEL), lambda b, i: (b, i, 0)),
        out_shape=jax.ShapeDtypeStruct((BATCH, SEQ, D_MODEL), F32),
        scratch_shapes=[pltpu.VMEM((FFN_TM, D_FF), BF16)],
        compiler_params=_cparams(("arbitrary", "arbitrary")),
        name="ffn_final" if final else "ffn",
    )(*args)


def _inproj_kernel(x_ref, mod_ref, g_ref, w_ref, o_ref, h_ref):
    j = pl.program_id(2)

    @pl.when(j == 0)
    def _():
        h = _norm_mod(x_ref[...], g_ref[...], mod_ref[3:4, :], mod_ref[4:5, :])
        h_ref[...] = h.astype(BF16)

    o_ref[...] = jnp.dot(h_ref[...], w_ref[j], preferred_element_type=F32)


def _inproj(x, mod_l, norm_g, w_blocks):
    return pl.pallas_call(
        _inproj_kernel,
        grid=(BATCH, SEQ // PROJ_TM, PROJ_NBLK),
        in_specs=[
            pl.BlockSpec((None, PROJ_TM, D_MODEL), lambda b, i, j: (b, i, 0)),
            pl.BlockSpec((None, N_MOD, D_MODEL), lambda b, i, j: (b, 0, 0)),
            pl.BlockSpec((1, D_MODEL), lambda b, i, j: (0, 0)),
            _resident((PROJ_NBLK, D_MODEL, PROJ_TN), lambda b, i, j: (0, 0, 0)),
        ],
        out_specs=pl.BlockSpec((None, PROJ_TM, PROJ_TN), lambda b, i, j: (b, i, j)),
        out_shape=jax.ShapeDtypeStruct((BATCH, SEQ, IN_COLS), F32),
        scratch_shapes=[pltpu.VMEM((PROJ_TM, D_MODEL), BF16)],
        compiler_params=_cparams(("arbitrary", "arbitrary", "arbitrary")),
        name="inproj",
    )(x, mod_l, norm_g.reshape(1, D_MODEL), w_blocks)


def _t5_bucket(dist):
    max_exact = REL_BUCKETS // 2
    d = np.maximum(dist, max_exact).astype(np.float32)
    large = max_exact + (np.log(d / max_exact) / np.log(REL_MAX_DIST / max_exact)
                         * (REL_BUCKETS - max_exact)).astype(np.int32)
    large = np.minimum(large, REL_BUCKETS - 1)
    return np.where(dist < max_exact, dist, large).astype(np.int32)


def _attn_bias_tables(rel_bias):
    qi = np.arange(BLOCK)[:, None]
    kj = np.arange(2 * BLOCK)[None, :]
    rel = BLOCK + qi - kj
    tabs = []
    for g, (window, dilation) in enumerate(ATTN_GROUPS):
        band = (rel >= 0) & (rel <= window // dilation)
        bucket = _t5_bucket(np.clip(rel, 0, None) * dilation)
        tbl = rel_bias[:, g * HEADS_PER_GROUP:(g + 1) * HEADS_PER_GROUP]
        bias = jnp.transpose(tbl[bucket], (2, 0, 1)).astype(F32)
        other = jnp.where(band[None], bias, NEG)
        first = jnp.concatenate(
            [other[:, :, BLOCK:], jnp.full((HEADS_PER_GROUP, BLOCK, BLOCK), NEG, F32)], axis=2)
        tabs.append(jnp.stack([first, other]))
    return jnp.stack(tabs)


def _attn_group(q_ref, k_ref, v_ref, bias_ref, o_ref, m_ref, s_ref, gi, dilation):
    nb = SEQ // dilation // BLOCK
    last = gi == N_GROUPS - 1
    lane = lax.broadcasted_iota(jnp.int32, (BLOCK, LANES), 1)
    head0 = lane < HEAD_DIM
    ones_cols = jnp.ones((2 * BLOCK, LANES), BF16)
    scale = HEAD_DIM ** -0.5
    contract_last = (((1,), (1,)), ((), ()))

    def body(n, carry):
        r = n // nb
        j = n % nb
        q_start = r + j * (BLOCK * dilation)
        kv_start = r + jnp.maximum(j - 1, 0) * (BLOCK * dilation)
        if dilation == 1:
            q_start = pl.multiple_of(q_start, BLOCK)
            kv_start = pl.multiple_of(kv_start, BLOCK)
        rows = pl.ds(q_start, BLOCK, stride=dilation)
        kv_rows = pl.ds(kv_start, 2 * BLOCK, stride=dilation)
        qf = q_ref[rows, :] * scale
        kb = k_ref[kv_rows, :].astype(BF16)
        v_aug = jnp.concatenate([v_ref[kv_rows, :].astype(BF16), ones_cols], axis=1)
        q0 = jnp.where(head0, qf, 0.0).astype(BF16)
        q1 = jnp.where(head0, 0.0, qf).astype(BF16)
        bias = bias_ref[jnp.minimum(j, 1)]
        l0 = lax.dot_general(q0, kb, contract_last, preferred_element_type=F32) + bias[0]
        l1 = lax.dot_general(q1, kb, contract_last, preferred_element_type=F32) + bias[1]
        m_blk = jnp.where(head0, jnp.max(l0, axis=1, keepdims=True),
                          jnp.max(l1, axis=1, keepdims=True))
        if gi == 0:
            m_new = m_blk
        else:
            m_old = m_ref[rows, :]
            m_new = jnp.maximum(m_old, m_blk)
            alpha = jnp.exp(m_old - m_new)
        p0 = jnp.exp(l0 - m_new[:, 0:1]).astype(BF16)
        p1 = jnp.exp(l1 - m_new[:, HEAD_DIM:HEAD_DIM + 1]).astype(BF16)
        r0 = jnp.dot(p0, v_aug, preferred_element_type=F32)
        r1 = jnp.dot(p1, v_aug, preferred_element_type=F32)
        acc = jnp.where(head0, r0[:, :LANES], r1[:, :LANES])
        den = jnp.where(head0, r0[:, LANES:], r1[:, LANES:])
        if gi > 0:
            acc = alpha * o_ref[rows, :] + acc
            den = alpha * s_ref[rows, :] + den
        if last:
            o_ref[rows, :] = acc / den
        else:
            o_ref[rows, :] = acc
            s_ref[rows, :] = den
            m_ref[rows, :] = m_new
        return carry

    lax.fori_loop(0, SEQ // BLOCK, body, 0)


def _attn_kernel(q_ref, k_ref, v_ref, bias_ref, o_ref, m_ref, s_ref):
    g = pl.program_id(2)
    for gi, (_, dilation) in enumerate(ATTN_GROUPS):
        pl.when(g == gi)(functools.partial(
            _attn_group, q_ref, k_ref, v_ref, bias_ref, o_ref, m_ref, s_ref, gi, dilation))


def _attention(proj, bias_tabs):
    def col(which):
        return lambda b, hp, g: (b, 0, which * N_GROUPS * HEAD_PAIRS + g * HEAD_PAIRS + hp)

    return pl.pallas_call(
        _attn_kernel,
        grid=(BATCH, HEAD_PAIRS, N_GROUPS),
        in_specs=[
            pl.BlockSpec((None, SEQ, LANES), col(0)),
            pl.BlockSpec((None, SEQ, LANES), col(1)),
            pl.BlockSpec((None, SEQ, LANES), col(2)),
            pl.BlockSpec((None, 2, 2, BLOCK, 2 * BLOCK), lambda b, hp, g: (g, 0, hp, 0, 0)),
        ],
        out_specs=pl.BlockSpec((None, SEQ, LANES), lambda b, hp, g: (b, 0, hp)),
        out_shape=jax.ShapeDtypeStruct((BATCH, SEQ, GROUP_WIDTH), F32),
        scratch_shapes=[pltpu.VMEM((SEQ, LANES), F32), pltpu.VMEM((SEQ, LANES), F32)],
        compiler_params=_cparams(("arbitrary", "arbitrary", "arbitrary")),
        name="dilated_attn",
    )(proj, proj, proj, bias_tabs)


def _ssm_param_kernel(lre_ref, lim_ref, ldt_ref, are_ref, aim_ref, fre_ref, fim_ref):
    lam_re = lre_ref[...]
    lam_im = lim_ref[...]
    dt = jnp.exp(ldt_ref[...])
    mag = jnp.exp(lam_re * dt)
    ang = lam_im * dt
    a_re = mag * jnp.cos(ang)
    a_im = mag * jnp.sin(ang)
    den = lam_re * lam_re + lam_im * lam_im
    are_ref[...] = a_re
    aim_ref[...] = a_im
    fre_ref[...] = ((a_re - 1) * lam_re + a_im * lam_im) / den
    fim_ref[...] = (a_im * lam_re - (a_re - 1) * lam_im) / den


def _ssm_params(lam_re, lam_im, log_dt):
    shp = jax.ShapeDtypeStruct((SSM_GROUPS, SSM_STATE), F32)
    return pl.pallas_call(
        _ssm_param_kernel, out_shape=(shp, shp, shp, shp), name="ssm_discretise",
    )(lam_re, lam_im, log_dt.reshape(SSM_GROUPS, 1))


def _ssm_kernel(u_ref, bre_ref, bim_ref, cre_ref, cim_ref, are_ref, aim_ref, d_ref,
                y_ref, sre_ref, sim_ref, xre_ref, xim_ref):
    c = pl.program_id(1)

    @pl.when(c == 0)
    def _():
        xre_ref[...] = jnp.zeros_like(xre_ref)
        xim_ref[...] = jnp.zeros_like(xim_ref)

    for b in range(BATCH):
        ub = u_ref[b].astype(BF16)
        bu_re = jnp.dot(ub, bre_ref[...], preferred_element_type=F32)
        bu_im = jnp.dot(ub, bim_ref[...], preferred_element_type=F32)
        for t in range(SLAB_TILES):
            sre_ref[t, b * SSM_PITCH:b * SSM_PITCH + SSM_TC, :] = bu_re[:, t * LANES:(t + 1) * LANES]
            sim_ref[t, b * SSM_PITCH:b * SSM_PITCH + SSM_TC, :] = bu_im[:, t * LANES:(t + 1) * LANES]

    a_re = [jnp.broadcast_to(are_ref[t:t + 1, :], (BATCH, LANES)) for t in range(SLAB_TILES)]
    a_im = [jnp.broadcast_to(aim_ref[t:t + 1, :], (BATCH, LANES)) for t in range(SLAB_TILES)]

    def step(i, carry):
        xs = list(carry)
        rows = pl.ds(i, BATCH, stride=SSM_PITCH)
        for t in range(SLAB_TILES):
            xr, xi = xs[2 * t], xs[2 * t + 1]
            nr = a_re[t] * xr - a_im[t] * xi + sre_ref[t, rows, :]
            ni = a_re[t] * xi + a_im[t] * xr + sim_ref[t, rows, :]
            sre_ref[t, rows, :] = nr
            sim_ref[t, rows, :] = ni
            xs[2 * t], xs[2 * t + 1] = nr, ni
        return tuple(xs)

    init = []
    for t in range(SLAB_TILES):
        init += [xre_ref[t], xim_ref[t]]
    fin = lax.fori_loop(0, SSM_TC, step, tuple(init), unroll=4)
    for t in range(SLAB_TILES):
        xre_ref[t] = fin[2 * t]
        xim_ref[t] = fin[2 * t + 1]

    for b in range(BATCH):
        sl = slice(b * SSM_PITCH, b * SSM_PITCH + SSM_TC)
        x_re = jnp.concatenate([sre_ref[t, sl, :] for t in range(SLAB_TILES)], axis=1)
        x_im = jnp.concatenate([sim_ref[t, sl, :] for t in range(SLAB_TILES)], axis=1)
        y = (_bdot(x_re, cre_ref[...]) + _bdot(x_im, cim_ref[...])
             + d_ref[...] * u_ref[b])
        y_ref[b] = jax.nn.gelu(y).astype(BF16)


def _ssm(proj, bb_re, bb_im, cc_re, cc_im, a_re, a_im, d_skip):
    slab = lambda s, c: (s, 0, 0)
    return pl.pallas_call(
        _ssm_kernel,
        grid=(SSM_SLABS, SEQ // SSM_TC),
        in_specs=[
            pl.BlockSpec((BATCH, SSM_TC, LANES), lambda s, c: (0, c, U_COL0 + s)),
            pl.BlockSpec((None, LANES, SLAB_STATES), slab),
            pl.BlockSpec((None, LANES, SLAB_STATES), slab),
            pl.BlockSpec((None, SLAB_STATES, LANES), slab),
            pl.BlockSpec((None, SLAB_STATES, LANES), slab),
            pl.BlockSpec((None, SLAB_TILES, LANES), slab),
            pl.BlockSpec((None, SLAB_TILES, LANES), slab),
            pl.BlockSpec((None, 1, LANES), slab),
        ],
        out_specs=pl.BlockSpec((BATCH, SSM_TC, LANES), lambda s, c: (0, c, s)),
        out_shape=jax.ShapeDtypeStruct((BATCH, SEQ, SSM_WIDTH), BF16),
        scratch_shapes=[
            pltpu.VMEM((SLAB_TILES, BATCH * SSM_PITCH, LANES), F32),
            pltpu.VMEM((SLAB_TILES, BATCH * SSM_PITCH, LANES), F32),
            pltpu.VMEM((SLAB_TILES, BATCH, LANES), F32),
            pltpu.VMEM((SLAB_TILES, BATCH, LANES), F32),
        ],
        compiler_params=_cparams(("arbitrary", "arbitrary")),
        name="s5_scan",
    )(proj, bb_re, bb_im, cc_re, cc_im, a_re, a_im, d_skip)


def _ssm_matrices(f_re, f_im, b_re, b_im, c_re, c_im):
    bb_re = f_re[..., None] * b_re - f_im[..., None] * b_im
    bb_im = f_re[..., None] * b_im + f_im[..., None] * b_re
    gps = SSM_GROUPS // SSM_SLABS
    eye = jnp.eye(gps, dtype=F32)

    def in_map(bb):
        bb = bb.reshape(SSM_SLABS, gps, SSM_STATE, SSM_GROUP)
        return jnp.einsum('sgpi,gh->sgihp', bb, eye).reshape(SSM_SLABS, LANES, SLAB_STATES)

    def out_map(cc):
        cc = cc.reshape(SSM_SLABS, gps, SSM_GROUP, SSM_STATE)
        return jnp.einsum('sgcp,gh->sgphc', cc, eye).reshape(SSM_SLABS, SLAB_STATES, LANES)

    return (in_map(bb_re).astype(BF16), in_map(bb_im).astype(BF16),
            out_map(c_re).astype(BF16), out_map(-c_im).astype(BF16))


def _mix_kernel(x_ref, mod_ref, o_ref_in, ys_ref, ga_ref, gs_ref, wap_ref, wglu_ref, wout_ref,
                out_ref):
    y_attn = _bdot(o_ref_in[...], wap_ref[...])
    gl = jnp.dot(ys_ref[...], wglu_ref[...], preferred_element_type=F32)
    y_ssm = gl[:, :D_MODEL] * jax.nn.sigmoid(gl[:, D_MODEL:])
    mixed = jax.nn.sigmoid(ga_ref[...]) * y_attn + jax.nn.sigmoid(gs_ref[...]) * y_ssm
    out_ref[...] = x_ref[...] + mod_ref[5:6, :] * _bdot(mixed, wout_ref[...])


def _mix(x, mod_l, attn_o, y_ssm, proj, w_ap, w_glu, w_out):
    rows = lambda b, i: (b, i, 0)
    return pl.pallas_call(
        _mix_kernel,
        grid=(BATCH, SEQ // MIX_TM),
        in_specs=[
            pl.BlockSpec((None, MIX_TM, D_MODEL), rows),
            pl.BlockSpec((None, N_MOD, D_MODEL), lambda b, i: (b, 0, 0)),
            pl.BlockSpec((None, MIX_TM, GROUP_WIDTH), rows),
            pl.BlockSpec((None, MIX_TM, SSM_WIDTH), rows),
            pl.BlockSpec((None, MIX_TM, D_MODEL), lambda b, i: (b, i, GATE_COL0)),
            pl.BlockSpec((None, MIX_TM, D_MODEL), lambda b, i: (b, i, GATE_COL0 + 1)),
            _resident((GROUP_WIDTH, D_MODEL), lambda b, i: (0, 0)),
            _resident((SSM_WIDTH, 2 * D_MODEL), lambda b, i: (0, 0)),
            _resident((D_MODEL, D_MODEL), lambda b, i: (0, 0)),
        ],
        out_specs=pl.BlockSpec((None, MIX_TM, D_MODEL), rows),
        out_shape=jax.ShapeDtypeStruct((BATCH, SEQ, D_MODEL), F32),
        compiler_params=_cparams(("arbitrary", "arbitrary")),
        name="mix_out",
    )(x, mod_l, attn_o, y_ssm, proj, proj, w_ap, w_glu, w_out)


def kernel(x, c, w_ada, b_ada, norm_ffn1, w_ffn1_in, w_ffn1_out, norm_mix, w_in, rel_bias, lam_re, lam_im, log_dt, b_re, b_im, c_re, c_im, d_skip, w_glu, w_attn_proj, w_out, norm_ffn2, w_ffn2_in, w_ffn2_out, final_norm):
    c_pad = jnp.zeros((SUBLANES, D_MODEL), F32).at[:BATCH].set(c)
    mod = _ada(c_pad, w_ada, b_ada).reshape(DEPTH, SUBLANES, N_MOD, D_MODEL)
    bias_tabs = _attn_bias_tables(rel_bias)
    for l in range(DEPTH):
        mod_l = mod[l]
        x = _ffn(x, mod_l, norm_ffn1[l], w_ffn1_in[l].astype(BF16), w_ffn1_out[l].astype(BF16), 0)

        w_blocks = w_in[l].reshape(D_MODEL, PROJ_NBLK, PROJ_TN).transpose(1, 0, 2).astype(BF16)
        proj = _inproj(x, mod_l, norm_mix[l], w_blocks)
        attn_o = _attention(proj, bias_tabs)
        a_re, a_im, f_re, f_im = _ssm_params(lam_re[l], lam_im[l], log_dt[l])
        bb_re, bb_im, cc_re, cc_im = _ssm_matrices(f_re, f_im, b_re[l], b_im[l], c_re[l], c_im[l])
        y_ssm = _ssm(proj, bb_re, bb_im, cc_re, cc_im,
                     a_re.reshape(SSM_SLABS, SLAB_TILES, LANES),
                     a_im.reshape(SSM_SLABS, SLAB_TILES, LANES),
                     d_skip[l].reshape(SSM_SLABS, 1, LANES))
        x = _mix(x, mod_l, attn_o, y_ssm, proj, w_attn_proj[l].astype(BF16),
                 w_glu[l].astype(BF16), w_out[l].astype(BF16))

        x = _ffn(x, mod_l, norm_ffn2[l], w_ffn2_in[l].astype(BF16), w_ffn2_out[l].astype(BF16), 6,
                 final_norm=final_norm if l == DEPTH - 1 else None)
    return x
```

```python
import functools
import math

import jax
import jax.numpy as jnp
import numpy as np
from jax import lax
from jax.experimental import pallas as pl
from jax.experimental.pallas import tpu as pltpu

D_MODEL = 1024
BATCH = 4
SEQ = 4096
DEPTH = 2
HEAD_DIM = 64
HEADS_PER_GROUP = 8
ATTN_GROUPS = ((128, 1), (512, 4), (2048, 16))
N_GROUPS = len(ATTN_GROUPS)
GROUP_WIDTH = HEADS_PER_GROUP * HEAD_DIM
BLOCK = 128
REL_BUCKETS = 32
REL_MAX_DIST = 2048
NEG = -1e30
SSM_WIDTH = 512
SSM_GROUP = 16
SSM_GROUPS = 32
SSM_STATE = 64
D_FF = 2816
QKV_COLS = 3 * N_GROUPS * GROUP_WIDTH
IN_COLS = QKV_COLS + SSM_WIDTH + 2 * D_MODEL
N_MOD = 9
EPS = 1e-6

LANES = 128
SUBLANES = 8
MXU_DIM = 256
VMEM_LIMIT = 56 * 1024 * 1024

ADA_TN = 1152
FFN_TM = 512
FFN_TF = MXU_DIM
PROJ_TM = 1024
PROJ_TN = 1024
PROJ_NBLK = IN_COLS // PROJ_TN
MIX_TM = 512
HEAD_PAIRS = GROUP_WIDTH // LANES
ATTN_UNROLL = 4
SSM_TC = 512
SSM_PITCH = SSM_TC + SUBLANES
SSM_SLABS = SSM_WIDTH // LANES
SLAB_STATES = (SSM_GROUPS // SSM_SLABS) * SSM_STATE
SLAB_TILES = SLAB_STATES // LANES

U_COL0 = QKV_COLS // LANES
GATE_COL0 = (QKV_COLS + SSM_WIDTH) // D_MODEL

BF16 = jnp.bfloat16
F32 = jnp.float32


def _cparams(sem):
    return pltpu.CompilerParams(dimension_semantics=sem, vmem_limit_bytes=VMEM_LIMIT)


def _resident(shape, index_map):
    return pl.BlockSpec(shape, index_map, pipeline_mode=pl.Buffered(1))


def _bdot(a, b):
    return jnp.dot(a.astype(BF16), b.astype(BF16), preferred_element_type=F32)


def _norm_mod(x, g, shift, scale):
    ms = jnp.mean(x * x, axis=-1, keepdims=True)
    y = x * lax.rsqrt(ms + EPS) * g
    return y * (1.0 + scale) + shift


def _ada_kernel(c_ref, w_ref, b_ref, o_ref):
    c = c_ref[...]
    ca = c * jax.nn.sigmoid(c)
    o_ref[...] = _bdot(ca, w_ref[...]) + b_ref[...]


def _ada(c_pad, w_ada, b_ada):
    n = N_MOD * D_MODEL
    return pl.pallas_call(
        _ada_kernel,
        grid=(DEPTH, n // ADA_TN),
        in_specs=[
            pl.BlockSpec((SUBLANES, D_MODEL), lambda l, j: (0, 0)),
            pl.BlockSpec((None, D_MODEL, ADA_TN), lambda l, j: (l, 0, j)),
            pl.BlockSpec((None, 1, ADA_TN), lambda l, j: (l, 0, j)),
        ],
        out_specs=pl.BlockSpec((None, SUBLANES, ADA_TN), lambda l, j: (l, 0, j)),
        out_shape=jax.ShapeDtypeStruct((DEPTH, SUBLANES, n), F32),
        compiler_params=_cparams(("arbitrary", "arbitrary")),
        name="ada_mod",
    )(c_pad, w_ada, b_ada.reshape(DEPTH, 1, n))


def _ffn_kernel(x_ref, mod_ref, g_ref, win_ref, wout_ref, *rest, row0, final):
    if final:
        fn_ref, o_ref, act_ref = rest
    else:
        o_ref, act_ref = rest
    x = x_ref[...]
    h = _norm_mod(x, g_ref[...], mod_ref[row0:row0 + 1, :], mod_ref[row0 + 1:row0 + 2, :])
    hb = h.astype(BF16)
    for j in range(D_FF // FFN_TF):
        a = jnp.dot(hb, win_ref[:, j * FFN_TF:(j + 1) * FFN_TF], preferred_element_type=F32)
        b = jnp.dot(hb, win_ref[:, D_FF + j * FFN_TF:D_FF + (j + 1) * FFN_TF],
                    preferred_element_type=F32)
        act_ref[:, j * FFN_TF:(j + 1) * FFN_TF] = (a * jax.nn.sigmoid(a) * b).astype(BF16)
    y = jnp.dot(act_ref[...], wout_ref[...], preferred_element_type=F32)
    out = x + (0.5 * mod_ref[row0 + 2:row0 + 3, :]) * y
    if final:
        ms = jnp.mean(out * out, axis=-1, keepdims=True)
        out = out * lax.rsqrt(ms + EPS) * fn_ref[...]
    o_ref[...] = out


def _ffn(x, mod_l, norm_g, w_in, w_out, row0, final_norm=None):
    final = final_norm is not None
    in_specs = [
        pl.BlockSpec((None, FFN_TM, D_MODEL), lambda b, i: (b, i, 0)),
        pl.BlockSpec((None, N_MOD, D_MODEL), lambda b, i: (b, 0, 0)),
        pl.BlockSpec((1, D_MODEL), lambda b, i: (0, 0)),
        _resident((D_MODEL, 2 * D_FF), lambda b, i: (0, 0)),
        _resident((D_FF, D_MODEL), lambda b, i: (0, 0)),
    ]
    args = [x, mod_l, norm_g.reshape(1, D_MODEL), w_in, w_out]
    if final:
        in_specs.append(pl.BlockSpec((1, D_MODEL), lambda b, i: (0, 0)))
        args.append(final_norm.reshape(1, D_MODEL))
    return pl.pallas_call(
        functools.partial(_ffn_kernel, row0=row0, final=final),
        grid=(BATCH, SEQ // FFN_TM),
        in_specs=in_specs,
        out_specs=pl.BlockSpec((None, FFN_TM, D_MODEL), lambda b, i: (b, i, 0)),
        out_shape=jax.ShapeDtypeStruct((BATCH, SEQ, D_MODEL), F32),
        scratch_shapes=[pltpu.VMEM((FFN_TM, D_FF), BF16)],
        compiler_params=_cparams(("arbitrary", "arbitrary")),
        name="ffn_final" if final else "ffn",
    )(*args)


def _inproj_kernel(x_ref, mod_ref, g_ref, w_ref, o_ref, h_ref):
    j = pl.program_id(2)

    @pl.when(j == 0)
    def _():
        h = _norm_mod(x_ref[...], g_ref[...], mod_ref[3:4, :], mod_ref[4:5, :])
        h_ref[...] = h.astype(BF16)

    o_ref[...] = jnp.dot(h_ref[...], w_ref[j], preferred_element_type=F32)


def _inproj(x, mod_l, norm_g, w_blocks):
    return pl.pallas_call(
        _inproj_kernel,
        grid=(BATCH, SEQ // PROJ_TM, PROJ_NBLK),
        in_specs=[
            pl.BlockSpec((None, PROJ_TM, D_MODEL), lambda b, i, j: (b, i, 0)),
            pl.BlockSpec((None, N_MOD, D_MODEL), lambda b, i, j: (b, 0, 0)),
            pl.BlockSpec((1, D_MODEL), lambda b, i, j: (0, 0)),
            _resident((PROJ_NBLK, D_MODEL, PROJ_TN), lambda b, i, j: (0, 0, 0)),
        ],
        out_specs=pl.BlockSpec((None, PROJ_TM, PROJ_TN), lambda b, i, j: (b, i, j)),
        out_shape=jax.ShapeDtypeStruct((BATCH, SEQ, IN_COLS), F32),
        scratch_shapes=[pltpu.VMEM((PROJ_TM, D_MODEL), BF16)],
        compiler_params=_cparams(("arbitrary", "arbitrary", "arbitrary")),
        name="inproj",
    )(x, mod_l, norm_g.reshape(1, D_MODEL), w_blocks)


def _t5_bucket(dist):
    max_exact = REL_BUCKETS // 2
    d = np.maximum(dist, max_exact).astype(np.float32)
    large = max_exact + (np.log(d / max_exact) / np.log(REL_MAX_DIST / max_exact)
                         * (REL_BUCKETS - max_exact)).astype(np.int32)
    large = np.minimum(large, REL_BUCKETS - 1)
    return np.where(dist < max_exact, dist, large).astype(np.int32)


def _attn_bias_tables(rel_bias):
    qi = np.arange(BLOCK)[:, None]
    kj = np.arange(2 * BLOCK)[None, :]
    rel = BLOCK + qi - kj
    tabs = []
    for g, (window, dilation) in enumerate(ATTN_GROUPS):
        band = (rel >= 0) & (rel <= window // dilation)
        bucket = _t5_bucket(np.clip(rel, 0, None) * dilation)
        tbl = rel_bias[:, g * HEADS_PER_GROUP:(g + 1) * HEADS_PER_GROUP]
        onehot = jnp.asarray(bucket[None] == np.arange(REL_BUCKETS)[:, None, None], F32)
        bias = jnp.einsum('rqk,rh->hqk', onehot, tbl.astype(F32),
                          precision=lax.Precision.HIGHEST)
        other = jnp.where(band[None], bias, NEG)
        first = jnp.concatenate(
            [other[:, :, BLOCK:], jnp.full((HEADS_PER_GROUP, BLOCK, BLOCK), NEG, F32)], axis=2)
        tabs.append(jnp.stack([first, other]))
    return jnp.stack(tabs)


def _attn_group(q_ref, k_ref, v_ref, bias_ref, o_ref, m_ref, s_ref, gi, dilation):
    nb = SEQ // dilation // BLOCK
    last = gi == N_GROUPS - 1
    lane = lax.broadcasted_iota(jnp.int32, (BLOCK, LANES), 1)
    head0 = lane < HEAD_DIM
    ones_cols = jnp.ones((2 * BLOCK, LANES), BF16)
    scale = HEAD_DIM ** -0.5
    contract_last = (((1,), (1,)), ((), ()))

    def block_rows(n):
        r = n // nb
        j = n % nb
        q_start = r + j * (BLOCK * dilation)
        kv_start = r + jnp.maximum(j - 1, 0) * (BLOCK * dilation)
        if dilation == 1:
            q_start = pl.multiple_of(q_start, BLOCK)
            kv_start = pl.multiple_of(kv_start, BLOCK)
        return (pl.ds(q_start, BLOCK, stride=dilation),
                pl.ds(kv_start, 2 * BLOCK, stride=dilation), jnp.minimum(j, 1))

    def logits(rows, kv_rows, tab):
        qf = q_ref[rows, :] * scale
        kb = k_ref[kv_rows, :].astype(BF16)
        q0 = jnp.where(head0, qf, 0.0).astype(BF16)
        q1 = jnp.where(head0, 0.0, qf).astype(BF16)
        bias = bias_ref[tab]
        l0 = lax.dot_general(q0, kb, contract_last, preferred_element_type=F32) + bias[0]
        l1 = lax.dot_general(q1, kb, contract_last, preferred_element_type=F32) + bias[1]
        m_blk = jnp.where(head0, jnp.max(l0, axis=1, keepdims=True),
                          jnp.max(l1, axis=1, keepdims=True))
        return l0, l1, m_blk

    def weighted(kv_rows, l0, l1, m_new):
        v_aug = jnp.concatenate([v_ref[kv_rows, :].astype(BF16), ones_cols], axis=1)
        p0 = jnp.exp(l0 - m_new[:, 0:1]).astype(BF16)
        p1 = jnp.exp(l1 - m_new[:, HEAD_DIM:HEAD_DIM + 1]).astype(BF16)
        r0 = jnp.dot(p0, v_aug, preferred_element_type=F32)
        r1 = jnp.dot(p1, v_aug, preferred_element_type=F32)
        return (jnp.where(head0, r0[:, :LANES], r1[:, :LANES]),
                jnp.where(head0, r0[:, LANES:], r1[:, LANES:]))

    def body(step, carry):
        blocks = [block_rows(step * ATTN_UNROLL + u) for u in range(ATTN_UNROLL)]
        scores = [logits(*blk) for blk in blocks]
        results = []
        for (rows, kv_rows, _), (l0, l1, m_new) in zip(blocks, scores):
            if gi > 0:
                m_old = m_ref[rows, :]
                m_new = jnp.maximum(m_old, m_new)
                alpha = jnp.exp(m_old - m_new)
            acc, den = weighted(kv_rows, l0, l1, m_new)
            if gi > 0:
                acc = alpha * o_ref[rows, :] + acc
                den = alpha * s_ref[rows, :] + den
            results.append((rows, acc, den, m_new))
        for rows, acc, den, m_new in results:
            if last:
                o_ref[rows, :] = acc / den
            else:
                o_ref[rows, :] = acc
                s_ref[rows, :] = den
                m_ref[rows, :] = m_new
        return carry

    lax.fori_loop(0, SEQ // BLOCK // ATTN_UNROLL, body, 0)


def _attn_kernel(q_ref, k_ref, v_ref, bias_ref, o_ref, m_ref, s_ref):
    g = pl.program_id(2)
    for gi, (_, dilation) in enumerate(ATTN_GROUPS):
        pl.when(g == gi)(functools.partial(
            _attn_group, q_ref, k_ref, v_ref, bias_ref, o_ref, m_ref, s_ref, gi, dilation))


def _attention(proj, bias_tabs):
    def col(which):
        return lambda b, hp, g: (b, 0, which * N_GROUPS * HEAD_PAIRS + g * HEAD_PAIRS + hp)

    return pl.pallas_call(
        _attn_kernel,
        grid=(BATCH, HEAD_PAIRS, N_GROUPS),
        in_specs=[
            pl.BlockSpec((None, SEQ, LANES), col(0)),
            pl.BlockSpec((None, SEQ, LANES), col(1)),
            pl.BlockSpec((None, SEQ, LANES), col(2)),
            pl.BlockSpec((None, 2, 2, BLOCK, 2 * BLOCK), lambda b, hp, g: (g, 0, hp, 0, 0)),
        ],
        out_specs=pl.BlockSpec((None, SEQ, LANES), lambda b, hp, g: (b, 0, hp)),
        out_shape=jax.ShapeDtypeStruct((BATCH, SEQ, GROUP_WIDTH), F32),
        scratch_shapes=[pltpu.VMEM((SEQ, LANES), F32), pltpu.VMEM((SEQ, LANES), F32)],
        compiler_params=_cparams(("arbitrary", "arbitrary", "arbitrary")),
        name="dilated_attn",
    )(proj, proj, proj, bias_tabs)


def _ssm_param_kernel(lre_ref, lim_ref, ldt_ref, are_ref, aim_ref, fre_ref, fim_ref):
    lam_re = lre_ref[...]
    lam_im = lim_ref[...]
    dt = jnp.exp(ldt_ref[...])
    mag = jnp.exp(lam_re * dt)
    ang = lam_im * dt
    a_re = mag * jnp.cos(ang)
    a_im = mag * jnp.sin(ang)
    den = lam_re * lam_re + lam_im * lam_im
    are_ref[...] = a_re
    aim_ref[...] = a_im
    fre_ref[...] = ((a_re - 1) * lam_re + a_im * lam_im) / den
    fim_ref[...] = (a_im * lam_re - (a_re - 1) * lam_im) / den


def _ssm_params(lam_re, lam_im, log_dt):
    shp = jax.ShapeDtypeStruct((SSM_GROUPS, SSM_STATE), F32)
    return pl.pallas_call(
        _ssm_param_kernel, out_shape=(shp, shp, shp, shp), name="ssm_discretise",
    )(lam_re, lam_im, log_dt.reshape(SSM_GROUPS, 1))


def _ssm_kernel(u_ref, bre_ref, bim_ref, cre_ref, cim_ref, are_ref, aim_ref, d_ref,
                y_ref, sre_ref, sim_ref, xre_ref, xim_ref):
    c = pl.program_id(1)

    @pl.when(c == 0)
    def _():
        xre_ref[...] = jnp.zeros_like(xre_ref)
        xim_ref[...] = jnp.zeros_like(xim_ref)

    for b in range(BATCH):
        ub = u_ref[b].astype(BF16)
        bu_re = jnp.dot(ub, bre_ref[...], preferred_element_type=F32)
        bu_im = jnp.dot(ub, bim_ref[...], preferred_element_type=F32)
        for t in range(SLAB_TILES):
            sre_ref[t, b * SSM_PITCH:b * SSM_PITCH + SSM_TC, :] = bu_re[:, t * LANES:(t + 1) * LANES]
            sim_ref[t, b * SSM_PITCH:b * SSM_PITCH + SSM_TC, :] = bu_im[:, t * LANES:(t + 1) * LANES]

    a_re = [jnp.broadcast_to(are_ref[t:t + 1, :], (BATCH, LANES)) for t in range(SLAB_TILES)]
    a_im = [jnp.broadcast_to(aim_ref[t:t + 1, :], (BATCH, LANES)) for t in range(SLAB_TILES)]

    def step(i, carry):
        xs = list(carry)
        rows = pl.ds(i, BATCH, stride=SSM_PITCH)
        for t in range(SLAB_TILES):
            xr, xi = xs[2 * t], xs[2 * t + 1]
            nr = a_re[t] * xr - a_im[t] * xi + sre_ref[t, rows, :]
            ni = a_re[t] * xi + a_im[t] * xr + sim_ref[t, rows, :]
            sre_ref[t, rows, :] = nr
            sim_ref[t, rows, :] = ni
            xs[2 * t], xs[2 * t + 1] = nr, ni
        return tuple(xs)

    init = []
    for t in range(SLAB_TILES):
        init += [xre_ref[t], xim_ref[t]]
    fin = lax.fori_loop(0, SSM_TC, step, tuple(init), unroll=4)
    for t in range(SLAB_TILES):
        xre_ref[t] = fin[2 * t]
        xim_ref[t] = fin[2 * t + 1]

    for b in range(BATCH):
        sl = slice(b * SSM_PITCH, b * SSM_PITCH + SSM_TC)
        x_re = jnp.concatenate([sre_ref[t, sl, :] for t in range(SLAB_TILES)], axis=1)
        x_im = jnp.concatenate([sim_ref[t, sl, :] for t in range(SLAB_TILES)], axis=1)
        y = (_bdot(x_re, cre_ref[...]) + _bdot(x_im, cim_ref[...])
             + d_ref[...] * u_ref[b])
        y_ref[b] = jax.nn.gelu(y).astype(BF16)


def _ssm(proj, bb_re, bb_im, cc_re, cc_im, a_re, a_im, d_skip):
    slab = lambda s, c: (s, 0, 0)
    return pl.pallas_call(
        _ssm_kernel,
        grid=(SSM_SLABS, SEQ // SSM_TC),
        in_specs=[
            pl.BlockSpec((BATCH, SSM_TC, LANES), lambda s, c: (0, c, U_COL0 + s)),
            pl.BlockSpec((None, LANES, SLAB_STATES), slab),
            pl.BlockSpec((None, LANES, SLAB_STATES), slab),
            pl.BlockSpec((None, SLAB_STATES, LANES), slab),
            pl.BlockSpec((None, SLAB_STATES, LANES), slab),
            pl.BlockSpec((None, SLAB_TILES, LANES), slab),
            pl.BlockSpec((None, SLAB_TILES, LANES), slab),
            pl.BlockSpec((None, 1, LANES), slab),
        ],
        out_specs=pl.BlockSpec((BATCH, SSM_TC, LANES), lambda s, c: (0, c, s)),
        out_shape=jax.ShapeDtypeStruct((BATCH, SEQ, SSM_WIDTH), BF16),
        scratch_shapes=[
            pltpu.VMEM((SLAB_TILES, BATCH * SSM_PITCH, LANES), F32),
            pltpu.VMEM((SLAB_TILES, BATCH * SSM_PITCH, LANES), F32),
            pltpu.VMEM((SLAB_TILES, BATCH, LANES), F32),
            pltpu.VMEM((SLAB_TILES, BATCH, LANES), F32),
        ],
        compiler_params=_cparams(("arbitrary", "arbitrary")),
        name="s5_scan",
    )(proj, bb_re, bb_im, cc_re, cc_im, a_re, a_im, d_skip)


def _ssm_matrices(f_re, f_im, b_re, b_im, c_re, c_im):
    bb_re = f_re[..., None] * b_re - f_im[..., None] * b_im
    bb_im = f_re[..., None] * b_im + f_im[..., None] * b_re
    gps = SSM_GROUPS // SSM_SLABS
    eye = jnp.eye(gps, dtype=F32)

    def in_map(bb):
        bb = bb.reshape(SSM_SLABS, gps, SSM_STATE, SSM_GROUP)
        return jnp.einsum('sgpi,gh->sgihp', bb, eye).reshape(SSM_SLABS, LANES, SLAB_STATES)

    def out_map(cc):
        cc = cc.reshape(SSM_SLABS, gps, SSM_GROUP, SSM_STATE)
        return jnp.einsum('sgcp,gh->sgphc', cc, eye).reshape(SSM_SLABS, SLAB_STATES, LANES)

    return (in_map(bb_re).astype(BF16), in_map(bb_im).astype(BF16),
            out_map(c_re).astype(BF16), out_map(-c_im).astype(BF16))


def _mix_kernel(x_ref, mod_ref, o_ref_in, ys_ref, ga_ref, gs_ref, wap_ref, wglu_ref, wout_ref,
                out_ref):
    y_attn = _bdot(o_ref_in[...], wap_ref[...])
    gl = jnp.dot(ys_ref[...], wglu_ref[...], preferred_element_type=F32)
    y_ssm = gl[:, :D_MODEL] * jax.nn.sigmoid(gl[:, D_MODEL:])
    mixed = jax.nn.sigmoid(ga_ref[...]) * y_attn + jax.nn.sigmoid(gs_ref[...]) * y_ssm
    out_ref[...] = x_ref[...] + mod_ref[5:6, :] * _bdot(mixed, wout_ref[...])


def _mix(x, mod_l, attn_o, y_ssm, proj, w_ap, w_glu, w_out):
    rows = lambda b, i: (b, i, 0)
    return pl.pallas_call(
        _mix_kernel,
        grid=(BATCH, SEQ // MIX_TM),
        in_specs=[
            pl.BlockSpec((None, MIX_TM, D_MODEL), rows),
            pl.BlockSpec((None, N_MOD, D_MODEL), lambda b, i: (b, 0, 0)),
            pl.BlockSpec((None, MIX_TM, GROUP_WIDTH), rows),
            pl.BlockSpec((None, MIX_TM, SSM_WIDTH), rows),
            pl.BlockSpec((None, MIX_TM, D_MODEL), lambda b, i: (b, i, GATE_COL0)),
            pl.BlockSpec((None, MIX_TM, D_MODEL), lambda b, i: (b, i, GATE_COL0 + 1)),
            _resident((GROUP_WIDTH, D_MODEL), lambda b, i: (0, 0)),
            _resident((SSM_WIDTH, 2 * D_MODEL), lambda b, i: (0, 0)),
            _resident((D_MODEL, D_MODEL), lambda b, i: (0, 0)),
        ],
        out_specs=pl.BlockSpec((None, MIX_TM, D_MODEL), rows),
        out_shape=jax.ShapeDtypeStruct((BATCH, SEQ, D_MODEL), F32),
        compiler_params=_cparams(("arbitrary", "arbitrary")),
        name="mix_out",
    )(x, mod_l, attn_o, y_ssm, proj, proj, w_ap, w_glu, w_out)


def kernel(x, c, w_ada, b_ada, norm_ffn1, w_ffn1_in, w_ffn1_out, norm_mix, w_in, rel_bias, lam_re, lam_im, log_dt, b_re, b_im, c_re, c_im, d_skip, w_glu, w_attn_proj, w_out, norm_ffn2, w_ffn2_in, w_ffn2_out, final_norm):
    c_pad = jnp.zeros((SUBLANES, D_MODEL), F32).at[:BATCH].set(c)
    mod = _ada(c_pad, w_ada, b_ada).reshape(DEPTH, SUBLANES, N_MOD, D_MODEL)
    bias_tabs = _attn_bias_tables(rel_bias)
    for l in range(DEPTH):
        mod_l = mod[l]
        x = _ffn(x, mod_l, norm_ffn1[l], w_ffn1_in[l].astype(BF16), w_ffn1_out[l].astype(BF16), 0)

        w_blocks = w_in[l].reshape(D_MODEL, PROJ_NBLK, PROJ_TN).transpose(1, 0, 2).astype(BF16)
        proj = _inproj(x, mod_l, norm_mix[l], w_blocks)
        attn_o = _attention(proj, bias_tabs)
        a_re, a_im, f_re, f_im = _ssm_params(lam_re[l], lam_im[l], log_dt[l])
        bb_re, bb_im, cc_re, cc_im = _ssm_matrices(f_re, f_im, b_re[l], b_im[l], c_re[l], c_im[l])
        y_ssm = _ssm(proj, bb_re, bb_im, cc_re, cc_im,
                     a_re.reshape(SSM_SLABS, SLAB_TILES, LANES),
                     a_im.reshape(SSM_SLABS, SLAB_TILES, LANES),
                     d_skip[l].reshape(SSM_SLABS, 1, LANES))
        x = _mix(x, mod_l, attn_o, y_ssm, proj, w_attn_proj[l].astype(BF16),
                 w_glu[l].astype(BF16), w_out[l].astype(BF16))

        x = _ffn(x, mod_l, norm_ffn2[l], w_ffn2_in[l].astype(BF16), w_ffn2_out[l].astype(BF16), 6,
                 final_norm=final_norm if l == DEPTH - 1 else None)
    return x
```

```python
import functools
import math

import jax
import jax.numpy as jnp
import numpy as np
from jax import lax
from jax.experimental import pallas as pl
from jax.experimental.pallas import tpu as pltpu

D_MODEL = 1024
BATCH = 4
SEQ = 4096
DEPTH = 2
HEAD_DIM = 64
HEADS_PER_GROUP = 8
ATTN_GROUPS = ((128, 1), (512, 4), (2048, 16))
N_GROUPS = len(ATTN_GROUPS)
GROUP_WIDTH = HEADS_PER_GROUP * HEAD_DIM
BLOCK = 128
REL_BUCKETS = 32
REL_MAX_DIST = 2048
NEG = -1e30
SSM_WIDTH = 512
SSM_GROUP = 16
SSM_GROUPS = 32
SSM_STATE = 64
D_FF = 2816
QKV_COLS = 3 * N_GROUPS * GROUP_WIDTH
IN_COLS = QKV_COLS + SSM_WIDTH + 2 * D_MODEL
N_MOD = 9
EPS = 1e-6

LANES = 128
SUBLANES = 8
MXU_DIM = 256
VMEM_LIMIT = 56 * 1024 * 1024

ADA_TN = 1152
FFN_TM = 512
FFN_TF = MXU_DIM
PROJ_TM = 512
PROJ_TN = 1024
PROJ_F32_COLS = QKV_COLS + SSM_WIDTH
MIX_TM = 512
HEAD_PAIRS = GROUP_WIDTH // LANES
ATTN_UNROLL = 4
ATTN_ORDER = (2, 1, 0)
N_QBLOCKS = SEQ // BLOCK
LOG2E = math.log2(math.e)
SSM_TC = 512
SSM_PITCH = SSM_TC + SUBLANES
SSM_SLABS = SSM_WIDTH // LANES
SLAB_STATES = (SSM_GROUPS // SSM_SLABS) * SSM_STATE
SLAB_TILES = SLAB_STATES // LANES

U_COL0 = QKV_COLS // LANES

BF16 = jnp.bfloat16
F32 = jnp.float32


def _cparams(sem):
    return pltpu.CompilerParams(dimension_semantics=sem, vmem_limit_bytes=VMEM_LIMIT)


def _resident(shape, index_map):
    return pl.BlockSpec(shape, index_map, pipeline_mode=pl.Buffered(1))


def _bdot(a, b):
    return jnp.dot(a.astype(BF16), b.astype(BF16), preferred_element_type=F32)


def _norm_mod(x, g, shift, scale):
    ms = jnp.mean(x * x, axis=-1, keepdims=True)
    y = x * lax.rsqrt(ms + EPS) * g
    return y * (1.0 + scale) + shift


def _ada_kernel(c_ref, w_ref, b_ref, o_ref):
    c = c_ref[...]
    ca = c * jax.nn.sigmoid(c)
    o_ref[...] = _bdot(ca, w_ref[...]) + b_ref[...]


def _ada(c_pad, w_ada, b_ada):
    n = N_MOD * D_MODEL
    return pl.pallas_call(
        _ada_kernel,
        grid=(DEPTH, n // ADA_TN),
        in_specs=[
            pl.BlockSpec((SUBLANES, D_MODEL), lambda l, j: (0, 0)),
            pl.BlockSpec((None, D_MODEL, ADA_TN), lambda l, j: (l, 0, j)),
            pl.BlockSpec((None, 1, ADA_TN), lambda l, j: (l, 0, j)),
        ],
        out_specs=pl.BlockSpec((None, SUBLANES, ADA_TN), lambda l, j: (l, 0, j)),
        out_shape=jax.ShapeDtypeStruct((DEPTH, SUBLANES, n), F32),
        compiler_params=_cparams(("arbitrary", "arbitrary")),
        name="ada_mod",
    )(c_pad, w_ada, b_ada.reshape(DEPTH, 1, n))


def _ffn_kernel(x_ref, mod_ref, g_ref, win_ref, wout_ref, *rest, row0, final):
    if final:
        fn_ref, o_ref, act_ref = rest
    else:
        o_ref, act_ref = rest
    x = x_ref[...]
    h = _norm_mod(x, g_ref[...], mod_ref[row0:row0 + 1, :], mod_ref[row0 + 1:row0 + 2, :])
    hb = h.astype(BF16)
    for j in range(D_FF // FFN_TF):
        a = jnp.dot(hb, win_ref[:, j * FFN_TF:(j + 1) * FFN_TF], preferred_element_type=F32)
        b = jnp.dot(hb, win_ref[:, D_FF + j * FFN_TF:D_FF + (j + 1) * FFN_TF],
                    preferred_element_type=F32)
        act_ref[:, j * FFN_TF:(j + 1) * FFN_TF] = (a * jax.nn.sigmoid(a) * b).astype(BF16)
    y = jnp.dot(act_ref[...], wout_ref[...], preferred_element_type=F32)
    out = x + (0.5 * mod_ref[row0 + 2:row0 + 3, :]) * y
    if final:
        ms = jnp.mean(out * out, axis=-1, keepdims=True)
        out = out * lax.rsqrt(ms + EPS) * fn_ref[...]
    o_ref[...] = out


def _ffn(x, mod_l, norm_g, w_in, w_out, row0, final_norm=None):
    final = final_norm is not None
    in_specs = [
        pl.BlockSpec((None, FFN_TM, D_MODEL), lambda b, i: (b, i, 0)),
        pl.BlockSpec((None, N_MOD, D_MODEL), lambda b, i: (b, 0, 0)),
        pl.BlockSpec((1, D_MODEL), lambda b, i: (0, 0)),
        _resident((D_MODEL, 2 * D_FF), lambda b, i: (0, 0)),
        _resident((D_FF, D_MODEL), lambda b, i: (0, 0)),
    ]
    args = [x, mod_l, norm_g.reshape(1, D_MODEL), w_in, w_out]
    if final:
        in_specs.append(pl.BlockSpec((1, D_MODEL), lambda b, i: (0, 0)))
        args.append(final_norm.reshape(1, D_MODEL))
    return pl.pallas_call(
        functools.partial(_ffn_kernel, row0=row0, final=final),
        grid=(BATCH, SEQ // FFN_TM),
        in_specs=in_specs,
        out_specs=pl.BlockSpec((None, FFN_TM, D_MODEL), lambda b, i: (b, i, 0)),
        out_shape=jax.ShapeDtypeStruct((BATCH, SEQ, D_MODEL), F32),
        scratch_shapes=[pltpu.VMEM((FFN_TM, D_FF), BF16)],
        compiler_params=_cparams(("arbitrary", "arbitrary")),
        name="ffn_final" if final else "ffn",
    )(*args)


def _inproj_kernel(x_ref, mod_ref, g_ref, w_ref, o_ref, gate_ref):
    h = _norm_mod(x_ref[...], g_ref[...], mod_ref[3:4, :], mod_ref[4:5, :]).astype(BF16)
    for c0 in range(0, IN_COLS, PROJ_TN):
        y = jnp.dot(h, w_ref[:, c0:c0 + PROJ_TN], preferred_element_type=F32)
        n_f32 = min(max(PROJ_F32_COLS - c0, 0), PROJ_TN)
        if n_f32:
            o_ref[:, c0:c0 + n_f32] = y[:, :n_f32]
        if n_f32 < PROJ_TN:
            g0 = c0 + n_f32 - PROJ_F32_COLS
            gate_ref[:, g0:g0 + PROJ_TN - n_f32] = y[:, n_f32:].astype(BF16)


def _inproj(x, mod_l, norm_g, w_in):
    rows = lambda b, i: (b, i, 0)
    return pl.pallas_call(
        _inproj_kernel,
        grid=(BATCH, SEQ // PROJ_TM),
        in_specs=[
            pl.BlockSpec((None, PROJ_TM, D_MODEL), rows),
            pl.BlockSpec((None, N_MOD, D_MODEL), lambda b, i: (b, 0, 0)),
            pl.BlockSpec((1, D_MODEL), lambda b, i: (0, 0)),
            _resident((D_MODEL, IN_COLS), lambda b, i: (0, 0)),
        ],
        out_specs=(pl.BlockSpec((None, PROJ_TM, PROJ_F32_COLS), rows),
                   pl.BlockSpec((None, PROJ_TM, 2 * D_MODEL), rows)),
        out_shape=(jax.ShapeDtypeStruct((BATCH, SEQ, PROJ_F32_COLS), F32),
                   jax.ShapeDtypeStruct((BATCH, SEQ, 2 * D_MODEL), BF16)),
        compiler_params=_cparams(("arbitrary", "arbitrary")),
        name="inproj",
    )(x, mod_l, norm_g.reshape(1, D_MODEL), w_in)


def _t5_bucket(dist):
    max_exact = REL_BUCKETS // 2
    d = np.maximum(dist, max_exact).astype(np.float32)
    large = max_exact + (np.log(d / max_exact) / np.log(REL_MAX_DIST / max_exact)
                         * (REL_BUCKETS - max_exact)).astype(np.int32)
    large = np.minimum(large, REL_BUCKETS - 1)
    return np.where(dist < max_exact, dist, large).astype(np.int32)


def _attn_bias_tables(rel_bias):
    qi = np.arange(BLOCK)[:, None]
    kj = np.arange(2 * BLOCK)[None, :]
    rel = BLOCK + qi - kj
    tabs = []
    for g, (window, dilation) in enumerate(ATTN_GROUPS):
        band = (rel >= 0) & (rel <= window // dilation)
        bucket = _t5_bucket(np.clip(rel, 0, None) * dilation)
        tbl = rel_bias[:, g * HEADS_PER_GROUP:(g + 1) * HEADS_PER_GROUP]
        onehot = jnp.asarray(bucket[None] == np.arange(REL_BUCKETS)[:, None, None], F32)
        bias = jnp.einsum('rqk,rh->hqk', onehot, tbl.astype(F32),
                          precision=lax.Precision.HIGHEST)
        other = jnp.where(band[None], bias * LOG2E, NEG)
        first = jnp.where(kj[None] >= BLOCK, other, NEG)
        tabs.append(jnp.stack([first, other]))
    return jnp.stack(tabs).reshape(N_GROUPS, 2, HEAD_PAIRS, 2 * BLOCK, 2 * BLOCK)


def _attn_group(q_ref, k_ref, v_ref, bias_ref, o_ref, part_ref, qs_ref, ks_ref, vs_ref,
                *, dilation, slot):
    nb = SEQ // dilation // BLOCK
    lane = lax.broadcasted_iota(jnp.int32, (BLOCK, LANES), 1)
    head0 = lane < HEAD_DIM
    ones_cols = jnp.ones((2 * BLOCK, LANES), BF16)
    qscale = HEAD_DIM ** -0.5 * LOG2E
    contract_last = (((1,), (1,)), ((), ()))

    def token_rows(n):
        start = n // nb + (n % nb) * (BLOCK * dilation)
        if dilation == 1:
            start = pl.multiple_of(start, BLOCK)
        return pl.ds(start, BLOCK, stride=dilation)

    def dense_rows(n, size):
        return pl.ds(pl.multiple_of(n * BLOCK, BLOCK), size)

    ks_ref[0:BLOCK, :] = jnp.zeros((BLOCK, LANES), BF16)
    vs_ref[0:BLOCK, :] = jnp.zeros((BLOCK, LANES), BF16)

    def stage(n, carry):
        rows = token_rows(n)
        qf = q_ref[rows, :] * qscale
        qs_ref[n, 0:BLOCK, :] = jnp.where(head0, qf, 0.0).astype(BF16)
        qs_ref[n, BLOCK:2 * BLOCK, :] = jnp.where(head0, 0.0, qf).astype(BF16)
        ks_ref[dense_rows(n + 1, BLOCK), :] = k_ref[rows, :].astype(BF16)
        vs_ref[dense_rows(n + 1, BLOCK), :] = v_ref[rows, :].astype(BF16)
        return carry

    lax.fori_loop(0, N_QBLOCKS, stage, 0, unroll=2)

    def logits(n):
        tab = jnp.where(n % nb == 0, 0, 1)
        l = lax.dot_general(qs_ref[n], ks_ref[dense_rows(n, 2 * BLOCK), :], contract_last,
                            preferred_element_type=F32) + bias_ref[tab]
        return l, jnp.max(l, axis=1, keepdims=True)

    def weighted(n, l, m_rows):
        v_aug = jnp.concatenate([vs_ref[dense_rows(n, 2 * BLOCK), :], ones_cols], axis=1)
        p = jnp.exp2(l - m_rows).astype(BF16)
        r = jnp.dot(p, v_aug, preferred_element_type=F32)
        return (jnp.where(head0, r[:BLOCK, :LANES], r[BLOCK:, :LANES]),
                jnp.where(head0, r[:BLOCK, LANES:], r[BLOCK:, LANES:]))

    def body(step, carry):
        ns = [step * ATTN_UNROLL + u for u in range(ATTN_UNROLL)]
        scores = [logits(n) for n in ns]
        results = []
        for n, (l, m_rows) in zip(ns, scores):
            acc, den = weighted(n, l, m_rows)
            m_blk = jnp.where(head0, m_rows[:BLOCK], m_rows[BLOCK:])
            results.append((token_rows(n), acc, m_blk, den))
        for rows, acc, m_blk, den in results:
            if slot is not None:
                part_ref[slot, 0, rows, :] = acc / den
                part_ref[slot, 1, rows, :] = m_blk + jnp.log2(den)
            else:
                others = [(part_ref[g, 0, rows, :], part_ref[g, 1, rows, :])
                          for g in range(N_GROUPS - 1)]
                m_all = functools.reduce(jnp.maximum, [m_blk] + [lse for _, lse in others])
                w_own = jnp.exp2(m_blk - m_all)
                num = w_own * acc
                den_all = w_own * den
                for out_g, lse in others:
                    w = jnp.exp2(lse - m_all)
                    num = num + w * out_g
                    den_all = den_all + w
                o_ref[rows, :] = num / den_all
        return carry

    lax.fori_loop(0, N_QBLOCKS // ATTN_UNROLL, body, 0)


def _attn_kernel(q_ref, k_ref, v_ref, bias_ref, o_ref, part_ref, qs_ref, ks_ref, vs_ref):
    step = pl.program_id(2)
    for i, gi in enumerate(ATTN_ORDER):
        pl.when(step == i)(functools.partial(
            _attn_group, q_ref, k_ref, v_ref, bias_ref, o_ref, part_ref, qs_ref, ks_ref, vs_ref,
            dilation=ATTN_GROUPS[gi][1], slot=i if i < N_GROUPS - 1 else None))


def _attention(proj, bias_tabs):
    def group(step):
        return (N_GROUPS - 1) - step

    def col(which):
        return lambda b, hp, s: (b, 0, (which * N_GROUPS + group(s)) * HEAD_PAIRS + hp)

    return pl.pallas_call(
        _attn_kernel,
        grid=(BATCH, HEAD_PAIRS, N_GROUPS),
        in_specs=[
            pl.BlockSpec((None, SEQ, LANES), col(0)),
            pl.BlockSpec((None, SEQ, LANES), col(1)),
            pl.BlockSpec((None, SEQ, LANES), col(2)),
            pl.BlockSpec((None, 2, None, 2 * BLOCK, 2 * BLOCK),
                         lambda b, hp, s: (group(s), 0, hp, 0, 0)),
        ],
        out_specs=pl.BlockSpec((None, SEQ, LANES), lambda b, hp, s: (b, 0, hp)),
        out_shape=jax.ShapeDtypeStruct((BATCH, SEQ, GROUP_WIDTH), F32),
        scratch_shapes=[
            pltpu.VMEM((N_GROUPS - 1, 2, SEQ, LANES), F32),
            pltpu.VMEM((N_QBLOCKS, 2 * BLOCK, LANES), BF16),
            pltpu.VMEM((SEQ + BLOCK, LANES), BF16),
            pltpu.VMEM((SEQ + BLOCK, LANES), BF16),
        ],
        compiler_params=_cparams(("arbitrary", "arbitrary", "arbitrary")),
        name="dilated_attn",
    )(proj, proj, proj, bias_tabs)


def _ssm_param_kernel(lre_ref, lim_ref, ldt_ref, are_ref, aim_ref, fre_ref, fim_ref):
    lam_re = lre_ref[...]
    lam_im = lim_ref[...]
    dt = jnp.exp(ldt_ref[...])
    mag = jnp.exp(lam_re * dt)
    ang = lam_im * dt
    a_re = mag * jnp.cos(ang)
    a_im = mag * jnp.sin(ang)
    den = lam_re * lam_re + lam_im * lam_im
    are_ref[...] = a_re
    aim_ref[...] = a_im
    fre_ref[...] = ((a_re - 1) * lam_re + a_im * lam_im) / den
    fim_ref[...] = (a_im * lam_re - (a_re - 1) * lam_im) / den


def _ssm_params(lam_re, lam_im, log_dt):
    shp = jax.ShapeDtypeStruct((SSM_GROUPS, SSM_STATE), F32)
    return pl.pallas_call(
        _ssm_param_kernel, out_shape=(shp, shp, shp, shp), name="ssm_discretise",
    )(lam_re, lam_im, log_dt.reshape(SSM_GROUPS, 1))


def _ssm_kernel(u_ref, bre_ref, bim_ref, cre_ref, cim_ref, are_ref, aim_ref, d_ref,
                y_ref, sre_ref, sim_ref, xre_ref, xim_ref):
    c = pl.program_id(1)

    @pl.when(c == 0)
    def _():
        xre_ref[...] = jnp.zeros_like(xre_ref)
        xim_ref[...] = jnp.zeros_like(xim_ref)

    for b in range(BATCH):
        ub = u_ref[b].astype(BF16)
        bu_re = jnp.dot(ub, bre_ref[...], preferred_element_type=F32)
        bu_im = jnp.dot(ub, bim_ref[...], preferred_element_type=F32)
        for t in range(SLAB_TILES):
            sre_ref[t, b * SSM_PITCH:b * SSM_PITCH + SSM_TC, :] = bu_re[:, t * LANES:(t + 1) * LANES]
            sim_ref[t, b * SSM_PITCH:b * SSM_PITCH + SSM_TC, :] = bu_im[:, t * LANES:(t + 1) * LANES]

    a_re = [jnp.broadcast_to(are_ref[t:t + 1, :], (BATCH, LANES)) for t in range(SLAB_TILES)]
    a_im = [jnp.broadcast_to(aim_ref[t:t + 1, :], (BATCH, LANES)) for t in range(SLAB_TILES)]

    def step(i, carry):
        xs = list(carry)
        rows = pl.ds(i, BATCH, stride=SSM_PITCH)
        for t in range(SLAB_TILES):
            xr, xi = xs[2 * t], xs[2 * t + 1]
            nr = a_re[t] * xr - a_im[t] * xi + sre_ref[t, rows, :]
            ni = a_re[t] * xi + a_im[t] * xr + sim_ref[t, rows, :]
            sre_ref[t, rows, :] = nr
            sim_ref[t, rows, :] = ni
            xs[2 * t], xs[2 * t + 1] = nr, ni
        return tuple(xs)

    init = []
    for t in range(SLAB_TILES):
        init += [xre_ref[t], xim_ref[t]]
    fin = lax.fori_loop(0, SSM_TC, step, tuple(init), unroll=4)
    for t in range(SLAB_TILES):
        xre_ref[t] = fin[2 * t]
        xim_ref[t] = fin[2 * t + 1]

    for b in range(BATCH):
        sl = slice(b * SSM_PITCH, b * SSM_PITCH + SSM_TC)
        x_re = jnp.concatenate([sre_ref[t, sl, :] for t in range(SLAB_TILES)], axis=1)
        x_im = jnp.concatenate([sim_ref[t, sl, :] for t in range(SLAB_TILES)], axis=1)
        y = (_bdot(x_re, cre_ref[...]) + _bdot(x_im, cim_ref[...])
             + d_ref[...] * u_ref[b])
        y_ref[b] = jax.nn.gelu(y).astype(BF16)


def _ssm(proj, bb_re, bb_im, cc_re, cc_im, a_re, a_im, d_skip):
    slab = lambda s, c: (s, 0, 0)
    return pl.pallas_call(
        _ssm_kernel,
        grid=(SSM_SLABS, SEQ // SSM_TC),
        in_specs=[
            pl.BlockSpec((BATCH, SSM_TC, LANES), lambda s, c: (0, c, U_COL0 + s)),
            pl.BlockSpec((None, LANES, SLAB_STATES), slab),
            pl.BlockSpec((None, LANES, SLAB_STATES), slab),
            pl.BlockSpec((None, SLAB_STATES, LANES), slab),
            pl.BlockSpec((None, SLAB_STATES, LANES), slab),
            pl.BlockSpec((None, SLAB_TILES, LANES), slab),
            pl.BlockSpec((None, SLAB_TILES, LANES), slab),
            pl.BlockSpec((None, 1, LANES), slab),
        ],
        out_specs=pl.BlockSpec((BATCH, SSM_TC, LANES), lambda s, c: (0, c, s)),
        out_shape=jax.ShapeDtypeStruct((BATCH, SEQ, SSM_WIDTH), BF16),
        scratch_shapes=[
            pltpu.VMEM((SLAB_TILES, BATCH * SSM_PITCH, LANES), F32),
            pltpu.VMEM((SLAB_TILES, BATCH * SSM_PITCH, LANES), F32),
            pltpu.VMEM((SLAB_TILES, BATCH, LANES), F32),
            pltpu.VMEM((SLAB_TILES, BATCH, LANES), F32),
        ],
        compiler_params=_cparams(("arbitrary", "arbitrary")),
        name="s5_scan",
    )(proj, bb_re, bb_im, cc_re, cc_im, a_re, a_im, d_skip)


def _ssm_matrices(f_re, f_im, b_re, b_im, c_re, c_im):
    bb_re = f_re[..., None] * b_re - f_im[..., None] * b_im
    bb_im = f_re[..., None] * b_im + f_im[..., None] * b_re
    gps = SSM_GROUPS // SSM_SLABS
    eye = jnp.eye(gps, dtype=F32)

    def in_map(bb):
        bb = bb.reshape(SSM_SLABS, gps, SSM_STATE, SSM_GROUP)
        return jnp.einsum('sgpi,gh->sgihp', bb, eye).reshape(SSM_SLABS, LANES, SLAB_STATES)

    def out_map(cc):
        cc = cc.reshape(SSM_SLABS, gps, SSM_GROUP, SSM_STATE)
        return jnp.einsum('sgcp,gh->sgphc', cc, eye).reshape(SSM_SLABS, SLAB_STATES, LANES)

    return (in_map(bb_re).astype(BF16), in_map(bb_im).astype(BF16),
            out_map(c_re).astype(BF16), out_map(-c_im).astype(BF16))


def _mix_kernel(x_ref, mod_ref, o_ref_in, ys_ref, ga_ref, gs_ref, wap_ref, wglu_ref, wout_ref,
                out_ref):
    y_attn = _bdot(o_ref_in[...], wap_ref[...])
    gl = jnp.dot(ys_ref[...], wglu_ref[...], preferred_element_type=F32)
    y_ssm = gl[:, :D_MODEL] * jax.nn.sigmoid(gl[:, D_MODEL:])
    mixed = (jax.nn.sigmoid(ga_ref[...].astype(F32)) * y_attn
             + jax.nn.sigmoid(gs_ref[...].astype(F32)) * y_ssm)
    out_ref[...] = x_ref[...] + mod_ref[5:6, :] * _bdot(mixed, wout_ref[...])


def _mix(x, mod_l, attn_o, y_ssm, gates, w_ap, w_glu, w_out):
    rows = lambda b, i: (b, i, 0)
    return pl.pallas_call(
        _mix_kernel,
        grid=(BATCH, SEQ // MIX_TM),
        in_specs=[
            pl.BlockSpec((None, MIX_TM, D_MODEL), rows),
            pl.BlockSpec((None, N_MOD, D_MODEL), lambda b, i: (b, 0, 0)),
            pl.BlockSpec((None, MIX_TM, GROUP_WIDTH), rows),
            pl.BlockSpec((None, MIX_TM, SSM_WIDTH), rows),
            pl.BlockSpec((None, MIX_TM, D_MODEL), rows),
            pl.BlockSpec((None, MIX_TM, D_MODEL), lambda b, i: (b, i, 1)),
            _resident((GROUP_WIDTH, D_MODEL), lambda b, i: (0, 0)),
            _resident((SSM_WIDTH, 2 * D_MODEL), lambda b, i: (0, 0)),
            _resident((D_MODEL, D_MODEL), lambda b, i: (0, 0)),
        ],
        out_specs=pl.BlockSpec((None, MIX_TM, D_MODEL), rows),
        out_shape=jax.ShapeDtypeStruct((BATCH, SEQ, D_MODEL), F32),
        compiler_params=_cparams(("arbitrary", "arbitrary")),
        name="mix_out",
    )(x, mod_l, attn_o, y_ssm, gates, gates, w_ap, w_glu, w_out)


def kernel(x, c, w_ada, b_ada, norm_ffn1, w_ffn1_in, w_ffn1_out, norm_mix, w_in, rel_bias, lam_re, lam_im, log_dt, b_re, b_im, c_re, c_im, d_skip, w_glu, w_attn_proj, w_out, norm_ffn2, w_ffn2_in, w_ffn2_out, final_norm):
    c_pad = jnp.zeros((SUBLANES, D_MODEL), F32).at[:BATCH].set(c)
    mod = _ada(c_pad, w_ada, b_ada).reshape(DEPTH, SUBLANES, N_MOD, D_MODEL)
    bias_tabs = _attn_bias_tables(rel_bias)
    for l in range(DEPTH):
        mod_l = mod[l]
        x = _ffn(x, mod_l, norm_ffn1[l], w_ffn1_in[l].astype(BF16), w_ffn1_out[l].astype(BF16), 0)

        proj, gates = _inproj(x, mod_l, norm_mix[l], w_in[l].astype(BF16))
        attn_o = _attention(proj, bias_tabs)
        a_re, a_im, f_re, f_im = _ssm_params(lam_re[l], lam_im[l], log_dt[l])
        bb_re, bb_im, cc_re, cc_im = _ssm_matrices(f_re, f_im, b_re[l], b_im[l], c_re[l], c_im[l])
        y_ssm = _ssm(proj, bb_re, bb_im, cc_re, cc_im,
                     a_re.reshape(SSM_SLABS, SLAB_TILES, LANES),
                     a_im.reshape(SSM_SLABS, SLAB_TILES, LANES),
                     d_skip[l].reshape(SSM_SLABS, 1, LANES))
        x = _mix(x, mod_l, attn_o, y_ssm, gates, w_attn_proj[l].astype(BF16),
                 w_glu[l].astype(BF16), w_out[l].astype(BF16))

        x = _ffn(x, mod_l, norm_ffn2[l], w_ffn2_in[l].astype(BF16), w_ffn2_out[l].astype(BF16), 6,
                 final_norm=final_norm if l == DEPTH - 1 else None)
    return x
```

```python
import functools
import math

import jax
import jax.numpy as jnp
import numpy as np
from jax import lax
from jax.experimental import pallas as pl
from jax.experimental.pallas import tpu as pltpu

D_MODEL = 1024
BATCH = 4
SEQ = 4096
DEPTH = 2
HEAD_DIM = 64
HEADS_PER_GROUP = 8
ATTN_GROUPS = ((128, 1), (512, 4), (2048, 16))
N_GROUPS = len(ATTN_GROUPS)
GROUP_WIDTH = HEADS_PER_GROUP * HEAD_DIM
BLOCK = 128
REL_BUCKETS = 32
REL_MAX_DIST = 2048
NEG = -1e30
SSM_WIDTH = 512
SSM_GROUP = 16
SSM_GROUPS = 32
SSM_STATE = 64
D_FF = 2816
QKV_COLS = 3 * N_GROUPS * GROUP_WIDTH
IN_COLS = QKV_COLS + SSM_WIDTH + 2 * D_MODEL
N_MOD = 9
EPS = 1e-6

LANES = 128
SUBLANES = 8
MXU_DIM = 256
VMEM_LIMIT = 56 * 1024 * 1024

ADA_TN = 1152
FFN_TM = 512
FFN_TF = MXU_DIM
PROJ_TM = 512
PROJ_TN = 1024
PROJ_F32_COLS = QKV_COLS + SSM_WIDTH
MIX_TM = 512
WEIGHT_CHUNKS = 8
HEAD_PAIRS = GROUP_WIDTH // LANES
ATTN_UNROLL = 4
ATTN_ORDER = (2, 1, 0)
N_QBLOCKS = SEQ // BLOCK
LOG2E = math.log2(math.e)
SSM_TC = 512
SSM_PITCH = SSM_TC + SUBLANES
SLAB_CH = MXU_DIM
SSM_SLABS = SSM_WIDTH // SLAB_CH
SLAB_STATES = (SSM_GROUPS // SSM_SLABS) * SSM_STATE
SLAB_TILES = SLAB_STATES // LANES
HALF_TILES = SLAB_TILES * BATCH // SUBLANES

U_COL0 = QKV_COLS // SLAB_CH

BF16 = jnp.bfloat16
F32 = jnp.float32


def _cparams(sem):
    return pltpu.CompilerParams(dimension_semantics=sem, vmem_limit_bytes=VMEM_LIMIT)


def _first_grid_step():
    return (pl.program_id(0) == 0) & (pl.program_id(1) == 0)


def _weight_scratch(k, n, chunks):
    return [pltpu.VMEM((k, n), BF16), pltpu.VMEM((2, k // chunks, n), F32),
            pltpu.SemaphoreType.DMA((2,))]


def _load_weight(w_hbm, layer, w_ref, stage_ref, sem):
    rows = stage_ref.shape[1]
    n_chunks = w_ref.shape[0] // rows

    def chunk(i, slot):
        return pltpu.make_async_copy(w_hbm.at[layer, pl.ds(i * rows, rows), :],
                                     stage_ref.at[slot], sem.at[slot])

    chunk(0, 0).start()

    def body(i, carry):
        slot = i % 2

        @pl.when(i + 1 < n_chunks)
        def _():
            chunk(i + 1, 1 - slot).start()

        chunk(i, slot).wait()
        w_ref[pl.ds(pl.multiple_of(i * rows, rows), rows), :] = stage_ref[slot].astype(BF16)
        return carry

    lax.fori_loop(0, n_chunks, body, 0)


def _bdot(a, b):
    return jnp.dot(a.astype(BF16), b.astype(BF16), preferred_element_type=F32)


def _norm_mod(x, g, shift, scale):
    ms = jnp.mean(x * x, axis=-1, keepdims=True)
    y = x * lax.rsqrt(ms + EPS) * g
    return y * (1.0 + scale) + shift


def _ada_kernel(c_ref, w_ref, b_ref, o_ref):
    c = c_ref[...]
    ca = c * jax.nn.sigmoid(c)
    o_ref[...] = _bdot(ca, w_ref[...]) + b_ref[...]


def _ada(c_pad, w_ada, b_ada):
    n = N_MOD * D_MODEL
    return pl.pallas_call(
        _ada_kernel,
        grid=(DEPTH, n // ADA_TN),
        in_specs=[
            pl.BlockSpec((SUBLANES, D_MODEL), lambda l, j: (0, 0)),
            pl.BlockSpec((None, D_MODEL, ADA_TN), lambda l, j: (l, 0, j)),
            pl.BlockSpec((None, 1, ADA_TN), lambda l, j: (l, 0, j)),
        ],
        out_specs=pl.BlockSpec((None, SUBLANES, ADA_TN), lambda l, j: (l, 0, j)),
        out_shape=jax.ShapeDtypeStruct((DEPTH, SUBLANES, n), F32),
        compiler_params=_cparams(("arbitrary", "arbitrary")),
        name="ada_mod",
    )(c_pad, w_ada, b_ada.reshape(DEPTH, 1, n))


def _ffn_kernel(x_ref, mod_ref, g_ref, win_hbm, wout_hbm, *rest, layer, row0, final):
    if final:
        fn_ref, rest = rest[0], rest[1:]
    o_ref, act_ref, win_ref, win_stage, win_sem, wout_ref, wout_stage, wout_sem = rest

    @pl.when(_first_grid_step())
    def _():
        _load_weight(win_hbm, layer, win_ref, win_stage, win_sem)
        _load_weight(wout_hbm, layer, wout_ref, wout_stage, wout_sem)

    x = x_ref[...]
    h = _norm_mod(x, g_ref[...], mod_ref[row0:row0 + 1, :], mod_ref[row0 + 1:row0 + 2, :])
    hb = h.astype(BF16)
    for j in range(D_FF // FFN_TF):
        a = jnp.dot(hb, win_ref[:, j * FFN_TF:(j + 1) * FFN_TF], preferred_element_type=F32)
        b = jnp.dot(hb, win_ref[:, D_FF + j * FFN_TF:D_FF + (j + 1) * FFN_TF],
                    preferred_element_type=F32)
        act_ref[:, j * FFN_TF:(j + 1) * FFN_TF] = (a * jax.nn.sigmoid(a) * b).astype(BF16)
    y = jnp.dot(act_ref[...], wout_ref[...], preferred_element_type=F32)
    out = x + (0.5 * mod_ref[row0 + 2:row0 + 3, :]) * y
    if final:
        ms = jnp.mean(out * out, axis=-1, keepdims=True)
        out = out * lax.rsqrt(ms + EPS) * fn_ref[...]
    o_ref[...] = out


def _ffn(x, mod_l, norm_g, w_in, w_out, layer, row0, final_norm=None):
    final = final_norm is not None
    in_specs = [
        pl.BlockSpec((None, FFN_TM, D_MODEL), lambda b, i: (b, i, 0)),
        pl.BlockSpec((None, N_MOD, D_MODEL), lambda b, i: (b, 0, 0)),
        pl.BlockSpec((1, D_MODEL), lambda b, i: (0, 0)),
        pl.BlockSpec(memory_space=pl.ANY),
        pl.BlockSpec(memory_space=pl.ANY),
    ]
    args = [x, mod_l, norm_g.reshape(1, D_MODEL), w_in, w_out]
    if final:
        in_specs.append(pl.BlockSpec((1, D_MODEL), lambda b, i: (0, 0)))
        args.append(final_norm.reshape(1, D_MODEL))
    return pl.pallas_call(
        functools.partial(_ffn_kernel, layer=layer, row0=row0, final=final),
        grid=(BATCH, SEQ // FFN_TM),
        in_specs=in_specs,
        out_specs=pl.BlockSpec((None, FFN_TM, D_MODEL), lambda b, i: (b, i, 0)),
        out_shape=jax.ShapeDtypeStruct((BATCH, SEQ, D_MODEL), F32),
        scratch_shapes=([pltpu.VMEM((FFN_TM, D_FF), BF16)]
                        + _weight_scratch(D_MODEL, 2 * D_FF, WEIGHT_CHUNKS)
                        + _weight_scratch(D_FF, D_MODEL, WEIGHT_CHUNKS)),
        compiler_params=_cparams(("arbitrary", "arbitrary")),
        name="ffn_final" if final else "ffn",
    )(*args)


def _inproj_kernel(x_ref, mod_ref, g_ref, w_hbm, o_ref, gate_ref, w_ref, w_stage, w_sem, *, layer):
    @pl.when(_first_grid_step())
    def _():
        _load_weight(w_hbm, layer, w_ref, w_stage, w_sem)

    h = _norm_mod(x_ref[...], g_ref[...], mod_ref[3:4, :], mod_ref[4:5, :]).astype(BF16)
    for c0 in range(0, IN_COLS, PROJ_TN):
        y = jnp.dot(h, w_ref[:, c0:c0 + PROJ_TN], preferred_element_type=F32)
        n_f32 = min(max(PROJ_F32_COLS - c0, 0), PROJ_TN)
        if n_f32:
            o_ref[:, c0:c0 + n_f32] = y[:, :n_f32]
        if n_f32 < PROJ_TN:
            g0 = c0 + n_f32 - PROJ_F32_COLS
            gate_ref[:, g0:g0 + PROJ_TN - n_f32] = y[:, n_f32:].astype(BF16)


def _inproj(x, mod_l, norm_g, w_in, layer):
    rows = lambda b, i: (b, i, 0)
    return pl.pallas_call(
        functools.partial(_inproj_kernel, layer=layer),
        grid=(BATCH, SEQ // PROJ_TM),
        in_specs=[
            pl.BlockSpec((None, PROJ_TM, D_MODEL), rows),
            pl.BlockSpec((None, N_MOD, D_MODEL), lambda b, i: (b, 0, 0)),
            pl.BlockSpec((1, D_MODEL), lambda b, i: (0, 0)),
            pl.BlockSpec(memory_space=pl.ANY),
        ],
        scratch_shapes=_weight_scratch(D_MODEL, IN_COLS, 2 * WEIGHT_CHUNKS),
        out_specs=(pl.BlockSpec((None, PROJ_TM, PROJ_F32_COLS), rows),
                   pl.BlockSpec((None, PROJ_TM, 2 * D_MODEL), rows)),
        out_shape=(jax.ShapeDtypeStruct((BATCH, SEQ, PROJ_F32_COLS), F32),
                   jax.ShapeDtypeStruct((BATCH, SEQ, 2 * D_MODEL), BF16)),
        compiler_params=_cparams(("arbitrary", "arbitrary")),
        name="inproj",
    )(x, mod_l, norm_g.reshape(1, D_MODEL), w_in)


def _t5_bucket(dist):
    max_exact = REL_BUCKETS // 2
    d = np.maximum(dist, max_exact).astype(np.float32)
    large = max_exact + (np.log(d / max_exact) / np.log(REL_MAX_DIST / max_exact)
                         * (REL_BUCKETS - max_exact)).astype(np.int32)
    large = np.minimum(large, REL_BUCKETS - 1)
    return np.where(dist < max_exact, dist, large).astype(np.int32)


def _attn_bias_tables(rel_bias):
    qi = np.arange(BLOCK)[:, None]
    kj = np.arange(2 * BLOCK)[None, :]
    rel = BLOCK + qi - kj
    tabs = []
    for g, (window, dilation) in enumerate(ATTN_GROUPS):
        band = (rel >= 0) & (rel <= window // dilation)
        bucket = _t5_bucket(np.clip(rel, 0, None) * dilation)
        tbl = rel_bias[:, g * HEADS_PER_GROUP:(g + 1) * HEADS_PER_GROUP]
        onehot = jnp.asarray(bucket[None] == np.arange(REL_BUCKETS)[:, None, None], F32)
        bias = jnp.einsum('rqk,rh->hqk', onehot, tbl.astype(F32),
                          precision=lax.Precision.HIGHEST)
        other = jnp.where(band[None], bias * LOG2E, NEG)
        first = jnp.where(kj[None] >= BLOCK, other, NEG)
        tabs.append(jnp.stack([first, other]))
    return jnp.stack(tabs).reshape(N_GROUPS, 2, HEAD_PAIRS, 2 * BLOCK, 2 * BLOCK)


def _attn_group(q_ref, k_ref, v_ref, bias_ref, o_ref, part_ref, qs_ref, ks_ref, vs_ref,
                *, dilation, slot):
    nb = SEQ // dilation // BLOCK
    lane = lax.broadcasted_iota(jnp.int32, (BLOCK, LANES), 1)
    head0 = lane < HEAD_DIM
    ones_cols = jnp.ones((2 * BLOCK, LANES), BF16)
    qscale = HEAD_DIM ** -0.5 * LOG2E
    contract_last = (((1,), (1,)), ((), ()))

    def token_rows(n):
        start = n // nb + (n % nb) * (BLOCK * dilation)
        if dilation == 1:
            start = pl.multiple_of(start, BLOCK)
        return pl.ds(start, BLOCK, stride=dilation)

    def dense_rows(n, size):
        return pl.ds(pl.multiple_of(n * BLOCK, BLOCK), size)

    ks_ref[0:BLOCK, :] = jnp.zeros((BLOCK, LANES), BF16)
    vs_ref[0:BLOCK, :] = jnp.zeros((BLOCK, LANES), BF16)

    def stage(n, carry):
        rows = token_rows(n)
        qf = q_ref[rows, :] * qscale
        qs_ref[n, 0:BLOCK, :] = jnp.where(head0, qf, 0.0).astype(BF16)
        qs_ref[n, BLOCK:2 * BLOCK, :] = jnp.where(head0, 0.0, qf).astype(BF16)
        ks_ref[dense_rows(n + 1, BLOCK), :] = k_ref[rows, :].astype(BF16)
        vs_ref[dense_rows(n + 1, BLOCK), :] = v_ref[rows, :].astype(BF16)
        return carry

    lax.fori_loop(0, N_QBLOCKS, stage, 0, unroll=2)

    def logits(n):
        tab = jnp.where(n % nb == 0, 0, 1)
        l = lax.dot_general(qs_ref[n], ks_ref[dense_rows(n, 2 * BLOCK), :], contract_last,
                            preferred_element_type=F32) + bias_ref[tab]
        return l, jnp.max(l, axis=1, keepdims=True)

    def weighted(n, l, m_rows):
        v_aug = jnp.concatenate([vs_ref[dense_rows(n, 2 * BLOCK), :], ones_cols], axis=1)
        p = jnp.exp2(l - m_rows).astype(BF16)
        r = jnp.dot(p, v_aug, preferred_element_type=F32)
        return (jnp.where(head0, r[:BLOCK, :LANES], r[BLOCK:, :LANES]),
                jnp.where(head0, r[:BLOCK, LANES:], r[BLOCK:, LANES:]))

    def body(step, carry):
        ns = [step * ATTN_UNROLL + u for u in range(ATTN_UNROLL)]
        scores = [logits(n) for n in ns]
        results = []
        for n, (l, m_rows) in zip(ns, scores):
            acc, den = weighted(n, l, m_rows)
            m_blk = jnp.where(head0, m_rows[:BLOCK], m_rows[BLOCK:])
            results.append((token_rows(n), acc, m_blk, den))
        for rows, acc, m_blk, den in results:
            if slot is not None:
                part_ref[slot, 0, rows, :] = acc / den
                part_ref[slot, 1, rows, :] = m_blk + jnp.log2(den)
            else:
                others = [(part_ref[g, 0, rows, :], part_ref[g, 1, rows, :])
                          for g in range(N_GROUPS - 1)]
                m_all = functools.reduce(jnp.maximum, [m_blk] + [lse for _, lse in others])
                w_own = jnp.exp2(m_blk - m_all)
                num = w_own * acc
                den_all = w_own * den
                for out_g, lse in others:
                    w = jnp.exp2(lse - m_all)
                    num = num + w * out_g
                    den_all = den_all + w
                o_ref[rows, :] = num / den_all
        return carry

    lax.fori_loop(0, N_QBLOCKS // ATTN_UNROLL, body, 0)


def _attn_kernel(q_ref, k_ref, v_ref, bias_ref, o_ref, part_ref, qs_ref, ks_ref, vs_ref):
    step = pl.program_id(2)
    for i, gi in enumerate(ATTN_ORDER):
        pl.when(step == i)(functools.partial(
            _attn_group, q_ref, k_ref, v_ref, bias_ref, o_ref, part_ref, qs_ref, ks_ref, vs_ref,
            dilation=ATTN_GROUPS[gi][1], slot=i if i < N_GROUPS - 1 else None))


def _attention(proj, bias_tabs):
    def group(step):
        return (N_GROUPS - 1) - step

    def col(which):
        return lambda b, hp, s: (b, 0, (which * N_GROUPS + group(s)) * HEAD_PAIRS + hp)

    return pl.pallas_call(
        _attn_kernel,
        grid=(BATCH, HEAD_PAIRS, N_GROUPS),
        in_specs=[
            pl.BlockSpec((None, SEQ, LANES), col(0)),
            pl.BlockSpec((None, SEQ, LANES), col(1)),
            pl.BlockSpec((None, SEQ, LANES), col(2)),
            pl.BlockSpec((None, 2, None, 2 * BLOCK, 2 * BLOCK),
                         lambda b, hp, s: (group(s), 0, hp, 0, 0)),
        ],
        out_specs=pl.BlockSpec((None, SEQ, LANES), lambda b, hp, s: (b, 0, hp)),
        out_shape=jax.ShapeDtypeStruct((BATCH, SEQ, GROUP_WIDTH), F32),
        scratch_shapes=[
            pltpu.VMEM((N_GROUPS - 1, 2, SEQ, LANES), F32),
            pltpu.VMEM((N_QBLOCKS, 2 * BLOCK, LANES), BF16),
            pltpu.VMEM((SEQ + BLOCK, LANES), BF16),
            pltpu.VMEM((SEQ + BLOCK, LANES), BF16),
        ],
        compiler_params=_cparams(("arbitrary", "arbitrary", "arbitrary")),
        name="dilated_attn",
    )(proj, proj, proj, bias_tabs)


def _ssm_param_kernel(lre_ref, lim_ref, ldt_ref, are_ref, aim_ref, fre_ref, fim_ref):
    lam_re = lre_ref[...]
    lam_im = lim_ref[...]
    dt = jnp.exp(ldt_ref[...])
    mag = jnp.exp(lam_re * dt)
    ang = lam_im * dt
    a_re = mag * jnp.cos(ang)
    a_im = mag * jnp.sin(ang)
    den = lam_re * lam_re + lam_im * lam_im
    are_ref[...] = a_re
    aim_ref[...] = a_im
    fre_ref[...] = ((a_re - 1) * lam_re + a_im * lam_im) / den
    fim_ref[...] = (a_im * lam_re - (a_re - 1) * lam_im) / den


def _ssm_params(lam_re, lam_im, log_dt):
    shp = jax.ShapeDtypeStruct((SSM_GROUPS, SSM_STATE), F32)
    return pl.pallas_call(
        _ssm_param_kernel, out_shape=(shp, shp, shp, shp), name="ssm_discretise",
    )(lam_re, lam_im, log_dt.reshape(SSM_GROUPS, 1))


def _ssm_kernel(u_ref, bre_ref, bim_ref, cre_ref, cim_ref, are_ref, aim_ref, d_ref,
                y_ref, sre_ref, sim_ref, xre_ref, xim_ref):
    c = pl.program_id(1)

    @pl.when(c == 0)
    def _():
        xre_ref[...] = jnp.zeros_like(xre_ref)
        xim_ref[...] = jnp.zeros_like(xim_ref)

    def plane_rows(t, b):
        half, pair = divmod(t, HALF_TILES)
        r0 = (half * BATCH + b) * SSM_PITCH
        return pair, slice(r0, r0 + SSM_TC)

    for b in range(BATCH):
        ub = u_ref[b].astype(BF16)
        bu_re = jnp.dot(ub, bre_ref[...], preferred_element_type=F32)
        bu_im = jnp.dot(ub, bim_ref[...], preferred_element_type=F32)
        for t in range(SLAB_TILES):
            pair, sl = plane_rows(t, b)
            sre_ref[pair, sl, :] = bu_re[:, t * LANES:(t + 1) * LANES]
            sim_ref[pair, sl, :] = bu_im[:, t * LANES:(t + 1) * LANES]

    def coeff(ref, pair):
        return jnp.concatenate(
            [jnp.broadcast_to(ref[half * HALF_TILES + pair:half * HALF_TILES + pair + 1, :],
                              (BATCH, LANES)) for half in range(2)], axis=0)

    a_re = [coeff(are_ref, p) for p in range(HALF_TILES)]
    a_im = [coeff(aim_ref, p) for p in range(HALF_TILES)]

    def step(i, carry):
        xs = list(carry)
        rows = pl.ds(i, SUBLANES, stride=SSM_PITCH)
        for p in range(HALF_TILES):
            xr, xi = xs[2 * p], xs[2 * p + 1]
            nr = a_re[p] * xr - a_im[p] * xi + sre_ref[p, rows, :]
            ni = a_re[p] * xi + a_im[p] * xr + sim_ref[p, rows, :]
            sre_ref[p, rows, :] = nr
            sim_ref[p, rows, :] = ni
            xs[2 * p], xs[2 * p + 1] = nr, ni
        return tuple(xs)

    init = []
    for p in range(HALF_TILES):
        init += [xre_ref[p], xim_ref[p]]
    fin = lax.fori_loop(0, SSM_TC, step, tuple(init), unroll=8)
    for p in range(HALF_TILES):
        xre_ref[p] = fin[2 * p]
        xim_ref[p] = fin[2 * p + 1]

    for b in range(BATCH):
        tiles = [plane_rows(t, b) for t in range(SLAB_TILES)]
        x_re = jnp.concatenate([sre_ref[pair, sl, :] for pair, sl in tiles], axis=1)
        x_im = jnp.concatenate([sim_ref[pair, sl, :] for pair, sl in tiles], axis=1)
        y = (_bdot(x_re, cre_ref[...]) + _bdot(x_im, cim_ref[...])
             + d_ref[...] * u_ref[b])
        y_ref[b] = jax.nn.gelu(y).astype(BF16)


def _ssm(proj, bb_re, bb_im, cc_re, cc_im, a_re, a_im, d_skip):
    slab = lambda s, c: (s, 0, 0)
    return pl.pallas_call(
        _ssm_kernel,
        grid=(SSM_SLABS, SEQ // SSM_TC),
        in_specs=[
            pl.BlockSpec((BATCH, SSM_TC, SLAB_CH), lambda s, c: (0, c, U_COL0 + s)),
            pl.BlockSpec((None, SLAB_CH, SLAB_STATES), slab),
            pl.BlockSpec((None, SLAB_CH, SLAB_STATES), slab),
            pl.BlockSpec((None, SLAB_STATES, SLAB_CH), slab),
            pl.BlockSpec((None, SLAB_STATES, SLAB_CH), slab),
            pl.BlockSpec((None, SLAB_TILES, LANES), slab),
            pl.BlockSpec((None, SLAB_TILES, LANES), slab),
            pl.BlockSpec((None, 1, SLAB_CH), slab),
        ],
        out_specs=pl.BlockSpec((BATCH, SSM_TC, SLAB_CH), lambda s, c: (0, c, s)),
        out_shape=jax.ShapeDtypeStruct((BATCH, SEQ, SSM_WIDTH), BF16),
        scratch_shapes=[
            pltpu.VMEM((HALF_TILES, SUBLANES * SSM_PITCH, LANES), F32),
            pltpu.VMEM((HALF_TILES, SUBLANES * SSM_PITCH, LANES), F32),
            pltpu.VMEM((HALF_TILES, SUBLANES, LANES), F32),
            pltpu.VMEM((HALF_TILES, SUBLANES, LANES), F32),
        ],
        compiler_params=_cparams(("arbitrary", "arbitrary")),
        name="s5_scan",
    )(proj, bb_re, bb_im, cc_re, cc_im, a_re, a_im, d_skip)


def _ssm_matrices(f_re, f_im, b_re, b_im, c_re, c_im):
    bb_re = f_re[..., None] * b_re - f_im[..., None] * b_im
    bb_im = f_re[..., None] * b_im + f_im[..., None] * b_re
    gps = SSM_GROUPS // SSM_SLABS
    eye = jnp.eye(gps, dtype=F32)

    def in_map(bb):
        bb = bb.reshape(SSM_SLABS, gps, SSM_STATE, SSM_GROUP)
        return jnp.einsum('sgpi,gh->sgihp', bb, eye).reshape(SSM_SLABS, SLAB_CH, SLAB_STATES)

    def out_map(cc):
        cc = cc.reshape(SSM_SLABS, gps, SSM_GROUP, SSM_STATE)
        return jnp.einsum('sgcp,gh->sgphc', cc, eye).reshape(SSM_SLABS, SLAB_STATES, SLAB_CH)

    return (in_map(bb_re).astype(BF16), in_map(bb_im).astype(BF16),
            out_map(c_re).astype(BF16), out_map(-c_im).astype(BF16))


def _mix_kernel(x_ref, mod_ref, o_ref_in, ys_ref, ga_ref, gs_ref, wap_hbm, wglu_hbm, wout_hbm,
                out_ref, wap_ref, wap_stage, wap_sem, wglu_ref, wglu_stage, wglu_sem,
                wout_ref, wout_stage, wout_sem, *, layer):
    @pl.when(_first_grid_step())
    def _():
        _load_weight(wap_hbm, layer, wap_ref, wap_stage, wap_sem)
        _load_weight(wglu_hbm, layer, wglu_ref, wglu_stage, wglu_sem)
        _load_weight(wout_hbm, layer, wout_ref, wout_stage, wout_sem)

    y_attn = _bdot(o_ref_in[...], wap_ref[...])
    gl = jnp.dot(ys_ref[...], wglu_ref[...], preferred_element_type=F32)
    y_ssm = gl[:, :D_MODEL] * jax.nn.sigmoid(gl[:, D_MODEL:])
    mixed = (jax.nn.sigmoid(ga_ref[...].astype(F32)) * y_attn
             + jax.nn.sigmoid(gs_ref[...].astype(F32)) * y_ssm)
    out_ref[...] = x_ref[...] + mod_ref[5:6, :] * _bdot(mixed, wout_ref[...])


def _mix(x, mod_l, attn_o, y_ssm, gates, w_ap, w_glu, w_out, layer):
    rows = lambda b, i: (b, i, 0)
    return pl.pallas_call(
        functools.partial(_mix_kernel, layer=layer),
        grid=(BATCH, SEQ // MIX_TM),
        in_specs=[
            pl.BlockSpec((None, MIX_TM, D_MODEL), rows),
            pl.BlockSpec((None, N_MOD, D_MODEL), lambda b, i: (b, 0, 0)),
            pl.BlockSpec((None, MIX_TM, GROUP_WIDTH), rows),
            pl.BlockSpec((None, MIX_TM, SSM_WIDTH), rows),
            pl.BlockSpec((None, MIX_TM, D_MODEL), rows),
            pl.BlockSpec((None, MIX_TM, D_MODEL), lambda b, i: (b, i, 1)),
            pl.BlockSpec(memory_space=pl.ANY),
            pl.BlockSpec(memory_space=pl.ANY),
            pl.BlockSpec(memory_space=pl.ANY),
        ],
        out_specs=pl.BlockSpec((None, MIX_TM, D_MODEL), rows),
        out_shape=jax.ShapeDtypeStruct((BATCH, SEQ, D_MODEL), F32),
        scratch_shapes=(_weight_scratch(GROUP_WIDTH, D_MODEL, WEIGHT_CHUNKS)
                        + _weight_scratch(SSM_WIDTH, 2 * D_MODEL, WEIGHT_CHUNKS)
                        + _weight_scratch(D_MODEL, D_MODEL, WEIGHT_CHUNKS)),
        compiler_params=_cparams(("arbitrary", "arbitrary")),
        name="mix_out",
    )(x, mod_l, attn_o, y_ssm, gates, gates, w_ap, w_glu, w_out)


def kernel(x, c, w_ada, b_ada, norm_ffn1, w_ffn1_in, w_ffn1_out, norm_mix, w_in, rel_bias, lam_re, lam_im, log_dt, b_re, b_im, c_re, c_im, d_skip, w_glu, w_attn_proj, w_out, norm_ffn2, w_ffn2_in, w_ffn2_out, final_norm):
    c_pad = jnp.zeros((SUBLANES, D_MODEL), F32).at[:BATCH].set(c)
    mod = _ada(c_pad, w_ada, b_ada).reshape(DEPTH, SUBLANES, N_MOD, D_MODEL)
    bias_tabs = _attn_bias_tables(rel_bias)
    for l in range(DEPTH):
        mod_l = mod[l]
        x = _ffn(x, mod_l, norm_ffn1[l], w_ffn1_in, w_ffn1_out, l, 0)

        proj, gates = _inproj(x, mod_l, norm_mix[l], w_in, l)
        attn_o = _attention(proj, bias_tabs)
        a_re, a_im, f_re, f_im = _ssm_params(lam_re[l], lam_im[l], log_dt[l])
        bb_re, bb_im, cc_re, cc_im = _ssm_matrices(f_re, f_im, b_re[l], b_im[l], c_re[l], c_im[l])
        y_ssm = _ssm(proj, bb_re, bb_im, cc_re, cc_im,
                     a_re.reshape(SSM_SLABS, SLAB_TILES, LANES),
                     a_im.reshape(SSM_SLABS, SLAB_TILES, LANES),
                     d_skip[l].reshape(SSM_SLABS, 1, SLAB_CH))
        x = _mix(x, mod_l, attn_o, y_ssm, gates, w_attn_proj, w_glu, w_out, l)

        x = _ffn(x, mod_l, norm_ffn2[l], w_ffn2_in, w_ffn2_out, l, 6,
                 final_norm=final_norm if l == DEPTH - 1 else None)
    return x
```

```python
import functools
import math

import jax
import jax.numpy as jnp
import numpy as np
from jax import lax
from jax.experimental import pallas as pl
from jax.experimental.pallas import tpu as pltpu

D_MODEL = 1024
BATCH = 4
SEQ = 4096
DEPTH = 2
HEAD_DIM = 64
HEADS_PER_GROUP = 8
ATTN_GROUPS = ((128, 1), (512, 4), (2048, 16))
N_GROUPS = len(ATTN_GROUPS)
GROUP_WIDTH = HEADS_PER_GROUP * HEAD_DIM
BLOCK = 128
REL_BUCKETS = 32
REL_MAX_DIST = 2048
NEG = -1e30
SSM_WIDTH = 512
SSM_GROUP = 16
SSM_GROUPS = 32
SSM_STATE = 64
D_FF = 2816
QKV_COLS = 3 * N_GROUPS * GROUP_WIDTH
IN_COLS = QKV_COLS + SSM_WIDTH + 2 * D_MODEL
N_MOD = 9
EPS = 1e-6

LANES = 128
SUBLANES = 8
MXU_DIM = 256
FREE_STRIDE = 4
VMEM_LIMIT = 56 * 1024 * 1024

ADA_TN = 1152
FFN_TM = 512
FFN_TF = MXU_DIM
PROJ_TM = 512
PROJ_TN = 1024
PROJ_F32_COLS = QKV_COLS + SSM_WIDTH
MIX_TM = 512
WEIGHT_CHUNKS = 8
HEAD_PAIRS = GROUP_WIDTH // LANES
ATTN_UNROLL = 8
ATTN_ORDER = (2, 1, 0)
N_QBLOCKS = SEQ // BLOCK
LOG2E = math.log2(math.e)
SSM_TC = 512
SSM_PITCH = SSM_TC + SUBLANES
SLAB_CH = MXU_DIM
SSM_SLABS = SSM_WIDTH // SLAB_CH
SLAB_STATES = (SSM_GROUPS // SSM_SLABS) * SSM_STATE
SLAB_TILES = SLAB_STATES // LANES
HALF_TILES = SLAB_TILES * BATCH // SUBLANES

U_COL0 = QKV_COLS // SLAB_CH

BF16 = jnp.bfloat16
F32 = jnp.float32


def _cparams(sem):
    return pltpu.CompilerParams(dimension_semantics=sem, vmem_limit_bytes=VMEM_LIMIT)


def _first_grid_step():
    return (pl.program_id(0) == 0) & (pl.program_id(1) == 0)


def _weight_scratch(k, n, chunks):
    return [pltpu.VMEM((k, n), BF16), pltpu.VMEM((2, k // chunks, n), F32),
            pltpu.SemaphoreType.DMA((2,))]


def _load_weight(w_hbm, layer, w_ref, stage_ref, sem):
    rows = stage_ref.shape[1]
    n_chunks = w_ref.shape[0] // rows

    def chunk(i, slot):
        return pltpu.make_async_copy(w_hbm.at[layer, pl.ds(i * rows, rows), :],
                                     stage_ref.at[slot], sem.at[slot])

    chunk(0, 0).start()

    def body(i, carry):
        slot = i % 2

        @pl.when(i + 1 < n_chunks)
        def _():
            chunk(i + 1, 1 - slot).start()

        chunk(i, slot).wait()
        w_ref[pl.ds(pl.multiple_of(i * rows, rows), rows), :] = stage_ref[slot].astype(BF16)
        return carry

    lax.fori_loop(0, n_chunks, body, 0)


def _bdot(a, b):
    return jnp.dot(a.astype(BF16), b.astype(BF16), preferred_element_type=F32)


def _norm_mod(x, g, shift, scale):
    ms = jnp.mean(x * x, axis=-1, keepdims=True)
    y = x * lax.rsqrt(ms + EPS) * g
    return y * (1.0 + scale) + shift


def _ada_kernel(c_ref, w_ref, b_ref, o_ref):
    c = c_ref[...]
    ca = c * jax.nn.sigmoid(c)
    o_ref[...] = _bdot(ca, w_ref[...]) + b_ref[...]


def _ada(c_pad, w_ada, b_ada):
    n = N_MOD * D_MODEL
    return pl.pallas_call(
        _ada_kernel,
        grid=(DEPTH, n // ADA_TN),
        in_specs=[
            pl.BlockSpec((SUBLANES, D_MODEL), lambda l, j: (0, 0)),
            pl.BlockSpec((None, D_MODEL, ADA_TN), lambda l, j: (l, 0, j)),
            pl.BlockSpec((None, 1, ADA_TN), lambda l, j: (l, 0, j)),
        ],
        out_specs=pl.BlockSpec((None, SUBLANES, ADA_TN), lambda l, j: (l, 0, j)),
        out_shape=jax.ShapeDtypeStruct((DEPTH, SUBLANES, n), F32),
        compiler_params=_cparams(("arbitrary", "arbitrary")),
        name="ada_mod",
    )(c_pad, w_ada, b_ada.reshape(DEPTH, 1, n))


def _ffn_kernel(x_ref, mod_ref, g_ref, win_hbm, wout_hbm, *rest, layer, row0, final):
    if final:
        fn_ref, rest = rest[0], rest[1:]
    o_ref, act_ref, win_ref, win_stage, win_sem, wout_ref, wout_stage, wout_sem = rest

    @pl.when(_first_grid_step())
    def _():
        _load_weight(win_hbm, layer, win_ref, win_stage, win_sem)
        _load_weight(wout_hbm, layer, wout_ref, wout_stage, wout_sem)

    x = x_ref[...]
    h = _norm_mod(x, g_ref[...], mod_ref[row0:row0 + 1, :], mod_ref[row0 + 1:row0 + 2, :])
    hb = h.astype(BF16)
    for j in range(D_FF // FFN_TF):
        a = jnp.dot(hb, win_ref[:, j * FFN_TF:(j + 1) * FFN_TF], preferred_element_type=F32)
        b = jnp.dot(hb, win_ref[:, D_FF + j * FFN_TF:D_FF + (j + 1) * FFN_TF],
                    preferred_element_type=F32)
        act_ref[:, j * FFN_TF:(j + 1) * FFN_TF] = (a * jax.nn.sigmoid(a) * b).astype(BF16)
    y = jnp.dot(act_ref[...], wout_ref[...], preferred_element_type=F32)
    out = x + (0.5 * mod_ref[row0 + 2:row0 + 3, :]) * y
    if final:
        ms = jnp.mean(out * out, axis=-1, keepdims=True)
        out = out * lax.rsqrt(ms + EPS) * fn_ref[...]
    o_ref[...] = out


def _ffn(x, mod_l, norm_g, w_in, w_out, layer, row0, final_norm=None):
    final = final_norm is not None
    in_specs = [
        pl.BlockSpec((None, FFN_TM, D_MODEL), lambda b, i: (b, i, 0)),
        pl.BlockSpec((None, N_MOD, D_MODEL), lambda b, i: (b, 0, 0)),
        pl.BlockSpec((1, D_MODEL), lambda b, i: (0, 0)),
        pl.BlockSpec(memory_space=pl.ANY),
        pl.BlockSpec(memory_space=pl.ANY),
    ]
    args = [x, mod_l, norm_g.reshape(1, D_MODEL), w_in, w_out]
    if final:
        in_specs.append(pl.BlockSpec((1, D_MODEL), lambda b, i: (0, 0)))
        args.append(final_norm.reshape(1, D_MODEL))
    return pl.pallas_call(
        functools.partial(_ffn_kernel, layer=layer, row0=row0, final=final),
        grid=(BATCH, SEQ // FFN_TM),
        in_specs=in_specs,
        out_specs=pl.BlockSpec((None, FFN_TM, D_MODEL), lambda b, i: (b, i, 0)),
        out_shape=jax.ShapeDtypeStruct((BATCH, SEQ, D_MODEL), F32),
        scratch_shapes=([pltpu.VMEM((FFN_TM, D_FF), BF16)]
                        + _weight_scratch(D_MODEL, 2 * D_FF, WEIGHT_CHUNKS)
                        + _weight_scratch(D_FF, D_MODEL, WEIGHT_CHUNKS)),
        compiler_params=_cparams(("arbitrary", "arbitrary")),
        name="ffn_final" if final else "ffn",
    )(*args)


def _inproj_kernel(x_ref, mod_ref, g_ref, w_hbm, o_ref, gate_ref, w_ref, w_stage, w_sem, *, layer):
    @pl.when(_first_grid_step())
    def _():
        _load_weight(w_hbm, layer, w_ref, w_stage, w_sem)

    h = _norm_mod(x_ref[...], g_ref[...], mod_ref[3:4, :], mod_ref[4:5, :]).astype(BF16)
    for c0 in range(0, IN_COLS, PROJ_TN):
        y = jnp.dot(h, w_ref[:, c0:c0 + PROJ_TN], preferred_element_type=F32)
        n_f32 = min(max(PROJ_F32_COLS - c0, 0), PROJ_TN)
        if n_f32:
            o_ref[:, c0:c0 + n_f32] = y[:, :n_f32]
        if n_f32 < PROJ_TN:
            g0 = c0 + n_f32 - PROJ_F32_COLS
            gate_ref[:, g0:g0 + PROJ_TN - n_f32] = y[:, n_f32:].astype(BF16)


def _inproj(x, mod_l, norm_g, w_in, layer):
    rows = lambda b, i: (b, i, 0)
    return pl.pallas_call(
        functools.partial(_inproj_kernel, layer=layer),
        grid=(BATCH, SEQ // PROJ_TM),
        in_specs=[
            pl.BlockSpec((None, PROJ_TM, D_MODEL), rows),
            pl.BlockSpec((None, N_MOD, D_MODEL), lambda b, i: (b, 0, 0)),
            pl.BlockSpec((1, D_MODEL), lambda b, i: (0, 0)),
            pl.BlockSpec(memory_space=pl.ANY),
        ],
        scratch_shapes=_weight_scratch(D_MODEL, IN_COLS, 2 * WEIGHT_CHUNKS),
        out_specs=(pl.BlockSpec((None, PROJ_TM, PROJ_F32_COLS), rows),
                   pl.BlockSpec((None, PROJ_TM, 2 * D_MODEL), rows)),
        out_shape=(jax.ShapeDtypeStruct((BATCH, SEQ, PROJ_F32_COLS), F32),
                   jax.ShapeDtypeStruct((BATCH, SEQ, 2 * D_MODEL), BF16)),
        compiler_params=_cparams(("arbitrary", "arbitrary")),
        name="inproj",
    )(x, mod_l, norm_g.reshape(1, D_MODEL), w_in)


def _t5_bucket(dist):
    max_exact = REL_BUCKETS // 2
    d = np.maximum(dist, max_exact).astype(np.float32)
    large = max_exact + (np.log(d / max_exact) / np.log(REL_MAX_DIST / max_exact)
                         * (REL_BUCKETS - max_exact)).astype(np.int32)
    large = np.minimum(large, REL_BUCKETS - 1)
    return np.where(dist < max_exact, dist, large).astype(np.int32)


def _attn_bias_tables(rel_bias):
    qi = np.arange(BLOCK)[:, None]
    kj = np.arange(2 * BLOCK)[None, :]
    rel = BLOCK + qi - kj
    tabs = []
    for g, (window, dilation) in enumerate(ATTN_GROUPS):
        band = (rel >= 0) & (rel <= window // dilation)
        bucket = _t5_bucket(np.clip(rel, 0, None) * dilation)
        tbl = rel_bias[:, g * HEADS_PER_GROUP:(g + 1) * HEADS_PER_GROUP]
        onehot = jnp.asarray(bucket[None] == np.arange(REL_BUCKETS)[:, None, None], F32)
        bias = jnp.einsum('rqk,rh->hqk', onehot, tbl.astype(F32),
                          precision=lax.Precision.HIGHEST)
        other = jnp.where(band[None], bias * LOG2E, NEG)
        first = jnp.where(kj[None] >= BLOCK, other, NEG)
        tabs.append(jnp.stack([first, other]))
    return jnp.stack(tabs).reshape(N_GROUPS, 2, HEAD_PAIRS, 2 * BLOCK, 2 * BLOCK)


def _attn_group(q_ref, k_ref, v_ref, bias_ref, o_ref, part_ref, qs_ref, ks_ref, vs_ref, tmp_ref,
                *, dilation, slot):
    nb = SEQ // dilation // BLOCK
    lane = lax.broadcasted_iota(jnp.int32, (BLOCK, LANES), 1)
    head0 = lane < HEAD_DIM
    ones_cols = jnp.ones((2 * BLOCK, LANES), BF16)
    qscale = HEAD_DIM ** -0.5 * LOG2E
    contract_last = (((1,), (1,)), ((), ()))

    def token_rows(n):
        start = n // nb + (n % nb) * (BLOCK * dilation)
        if dilation == 1:
            start = pl.multiple_of(start, BLOCK)
        return pl.ds(start, BLOCK, stride=dilation)

    def dense_rows(n, size):
        return pl.ds(pl.multiple_of(n * BLOCK, BLOCK), size)

    if dilation > FREE_STRIDE:
        outer = dilation // FREE_STRIDE
        assert outer <= FREE_STRIDE
        per_residue = N_QBLOCKS // FREE_STRIDE

        def split(c, carry):
            rows = pl.ds(c // per_residue + (c % per_residue) * (BLOCK * FREE_STRIDE), BLOCK,
                         stride=FREE_STRIDE)
            for w, ref in enumerate((q_ref, k_ref, v_ref)):
                tmp_ref[w, dense_rows(c, BLOCK), :] = ref[rows, :]
            return carry

        lax.fori_loop(0, N_QBLOCKS, split, 0, unroll=2)
        q_src, k_src, v_src = tmp_ref.at[0], tmp_ref.at[1], tmp_ref.at[2]

        def source_rows(n):
            r, j = n // nb, n % nb
            start = (r % FREE_STRIDE) * (SEQ // FREE_STRIDE) + r // FREE_STRIDE + j * (BLOCK * outer)
            return pl.ds(start, BLOCK, stride=outer)
    else:
        q_src, k_src, v_src = q_ref, k_ref, v_ref
        source_rows = token_rows

    ks_ref[0:BLOCK, :] = jnp.zeros((BLOCK, LANES), BF16)
    vs_ref[0:BLOCK, :] = jnp.zeros((BLOCK, LANES), BF16)

    def stage(n, carry):
        rows = source_rows(n)
        qf = q_src[rows, :] * qscale
        qs_ref[n, 0:BLOCK, :] = jnp.where(head0, qf, 0.0).astype(BF16)
        qs_ref[n, BLOCK:2 * BLOCK, :] = jnp.where(head0, 0.0, qf).astype(BF16)
        ks_ref[dense_rows(n + 1, BLOCK), :] = k_src[rows, :].astype(BF16)
        vs_ref[dense_rows(n + 1, BLOCK), :] = v_src[rows, :].astype(BF16)
        return carry

    lax.fori_loop(0, N_QBLOCKS, stage, 0, unroll=2)

    def logits(n):
        tab = jnp.where(n % nb == 0, 0, 1)
        l = lax.dot_general(qs_ref[n], ks_ref[dense_rows(n, 2 * BLOCK), :], contract_last,
                            preferred_element_type=F32) + bias_ref[tab]
        return l, jnp.max(l, axis=1, keepdims=True)

    def weighted(n, l, m_rows):
        v_aug = jnp.concatenate([vs_ref[dense_rows(n, 2 * BLOCK), :], ones_cols], axis=1)
        p = jnp.exp2(l - m_rows).astype(BF16)
        r = jnp.dot(p, v_aug, preferred_element_type=F32)
        return (jnp.where(head0, r[:BLOCK, :LANES], r[BLOCK:, :LANES]),
                jnp.where(head0, r[:BLOCK, LANES:], r[BLOCK:, LANES:]))

    def body(step, carry):
        ns = [step * ATTN_UNROLL + u for u in range(ATTN_UNROLL)]
        scores = [logits(n) for n in ns]
        results = []
        for n, (l, m_rows) in zip(ns, scores):
            acc, den = weighted(n, l, m_rows)
            m_blk = jnp.where(head0, m_rows[:BLOCK], m_rows[BLOCK:])
            results.append((token_rows(n), acc, m_blk, den))
        for rows, acc, m_blk, den in results:
            if slot is not None:
                part_ref[slot, 0, rows, :] = acc / den
                part_ref[slot, 1, rows, :] = m_blk + jnp.log2(den)
            else:
                others = [(part_ref[g, 0, rows, :], part_ref[g, 1, rows, :])
                          for g in range(N_GROUPS - 1)]
                m_all = functools.reduce(jnp.maximum, [m_blk] + [lse for _, lse in others])
                w_own = jnp.exp2(m_blk - m_all)
                num = w_own * acc
                den_all = w_own * den
                for out_g, lse in others:
                    w = jnp.exp2(lse - m_all)
                    num = num + w * out_g
                    den_all = den_all + w
                o_ref[rows, :] = num / den_all
        return carry

    lax.fori_loop(0, N_QBLOCKS // ATTN_UNROLL, body, 0)


def _attn_kernel(q_ref, k_ref, v_ref, bias_ref, o_ref, part_ref, qs_ref, ks_ref, vs_ref, tmp_ref):
    step = pl.program_id(2)
    for i, gi in enumerate(ATTN_ORDER):
        pl.when(step == i)(functools.partial(
            _attn_group, q_ref, k_ref, v_ref, bias_ref, o_ref, part_ref, qs_ref, ks_ref, vs_ref,
            tmp_ref,
            dilation=ATTN_GROUPS[gi][1], slot=i if i < N_GROUPS - 1 else None))


def _attention(proj, bias_tabs):
    def group(step):
        return (N_GROUPS - 1) - step

    def col(which):
        return lambda b, hp, s: (b, 0, (which * N_GROUPS + group(s)) * HEAD_PAIRS + hp)

    return pl.pallas_call(
        _attn_kernel,
        grid=(BATCH, HEAD_PAIRS, N_GROUPS),
        in_specs=[
            pl.BlockSpec((None, SEQ, LANES), col(0)),
            pl.BlockSpec((None, SEQ, LANES), col(1)),
            pl.BlockSpec((None, SEQ, LANES), col(2)),
            pl.BlockSpec((None, 2, None, 2 * BLOCK, 2 * BLOCK),
                         lambda b, hp, s: (group(s), 0, hp, 0, 0)),
        ],
        out_specs=pl.BlockSpec((None, SEQ, LANES), lambda b, hp, s: (b, 0, hp)),
        out_shape=jax.ShapeDtypeStruct((BATCH, SEQ, GROUP_WIDTH), F32),
        scratch_shapes=[
            pltpu.VMEM((N_GROUPS - 1, 2, SEQ, LANES), F32),
            pltpu.VMEM((N_QBLOCKS, 2 * BLOCK, LANES), BF16),
            pltpu.VMEM((SEQ + BLOCK, LANES), BF16),
            pltpu.VMEM((SEQ + BLOCK, LANES), BF16),
            pltpu.VMEM((3, SEQ, LANES), F32),
        ],
        compiler_params=_cparams(("arbitrary", "arbitrary", "arbitrary")),
        name="dilated_attn",
    )(proj, proj, proj, bias_tabs)


def _ssm_param_kernel(lre_ref, lim_ref, ldt_ref, are_ref, aim_ref, fre_ref, fim_ref):
    lam_re = lre_ref[...]
    lam_im = lim_ref[...]
    dt = jnp.exp(ldt_ref[...])
    mag = jnp.exp(lam_re * dt)
    ang = lam_im * dt
    a_re = mag * jnp.cos(ang)
    a_im = mag * jnp.sin(ang)
    den = lam_re * lam_re + lam_im * lam_im
    are_ref[...] = a_re
    aim_ref[...] = a_im
    fre_ref[...] = ((a_re - 1) * lam_re + a_im * lam_im) / den
    fim_ref[...] = (a_im * lam_re - (a_re - 1) * lam_im) / den


def _ssm_params(lam_re, lam_im, log_dt):
    shp = jax.ShapeDtypeStruct((SSM_GROUPS, SSM_STATE), F32)
    return pl.pallas_call(
        _ssm_param_kernel, out_shape=(shp, shp, shp, shp), name="ssm_discretise",
    )(lam_re, lam_im, log_dt.reshape(SSM_GROUPS, 1))


def _ssm_kernel(u_ref, bre_ref, bim_ref, cre_ref, cim_ref, are_ref, aim_ref, d_ref,
                y_ref, sre_ref, sim_ref, xre_ref, xim_ref):
    c = pl.program_id(1)

    @pl.when(c == 0)
    def _():
        xre_ref[...] = jnp.zeros_like(xre_ref)
        xim_ref[...] = jnp.zeros_like(xim_ref)

    def plane_rows(t, b):
        half, pair = divmod(t, HALF_TILES)
        r0 = (half * BATCH + b) * SSM_PITCH
        return pair, slice(r0, r0 + SSM_TC)

    for b in range(BATCH):
        ub = u_ref[b].astype(BF16)
        bu_re = jnp.dot(ub, bre_ref[...], preferred_element_type=F32)
        bu_im = jnp.dot(ub, bim_ref[...], preferred_element_type=F32)
        for t in range(SLAB_TILES):
            pair, sl = plane_rows(t, b)
            sre_ref[pair, sl, :] = bu_re[:, t * LANES:(t + 1) * LANES]
            sim_ref[pair, sl, :] = bu_im[:, t * LANES:(t + 1) * LANES]

    def coeff(ref, pair):
        return jnp.concatenate(
            [jnp.broadcast_to(ref[half * HALF_TILES + pair:half * HALF_TILES + pair + 1, :],
                              (BATCH, LANES)) for half in range(2)], axis=0)

    a_re = [coeff(are_ref, p) for p in range(HALF_TILES)]
    a_im = [coeff(aim_ref, p) for p in range(HALF_TILES)]

    def step(i, carry):
        xs = list(carry)
        rows = pl.ds(i, SUBLANES, stride=SSM_PITCH)
        for p in range(HALF_TILES):
            xr, xi = xs[2 * p], xs[2 * p + 1]
            nr = a_re[p] * xr - a_im[p] * xi + sre_ref[p, rows, :]
            ni = a_re[p] * xi + a_im[p] * xr + sim_ref[p, rows, :]
            sre_ref[p, rows, :] = nr
            sim_ref[p, rows, :] = ni
            xs[2 * p], xs[2 * p + 1] = nr, ni
        return tuple(xs)

    init = []
    for p in range(HALF_TILES):
        init += [xre_ref[p], xim_ref[p]]
    fin = lax.fori_loop(0, SSM_TC, step, tuple(init), unroll=8)
    for p in range(HALF_TILES):
        xre_ref[p] = fin[2 * p]
        xim_ref[p] = fin[2 * p + 1]

    for b in range(BATCH):
        tiles = [plane_rows(t, b) for t in range(SLAB_TILES)]
        x_re = jnp.concatenate([sre_ref[pair, sl, :] for pair, sl in tiles], axis=1)
        x_im = jnp.concatenate([sim_ref[pair, sl, :] for pair, sl in tiles], axis=1)
        y = (_bdot(x_re, cre_ref[...]) + _bdot(x_im, cim_ref[...])
             + d_ref[...] * u_ref[b])
        y_ref[b] = jax.nn.gelu(y).astype(BF16)


def _ssm(proj, bb_re, bb_im, cc_re, cc_im, a_re, a_im, d_skip):
    slab = lambda s, c: (s, 0, 0)
    return pl.pallas_call(
        _ssm_kernel,
        grid=(SSM_SLABS, SEQ // SSM_TC),
        in_specs=[
            pl.BlockSpec((BATCH, SSM_TC, SLAB_CH), lambda s, c: (0, c, U_COL0 + s)),
            pl.BlockSpec((None, SLAB_CH, SLAB_STATES), slab),
            pl.BlockSpec((None, SLAB_CH, SLAB_STATES), slab),
            pl.BlockSpec((None, SLAB_STATES, SLAB_CH), slab),
            pl.BlockSpec((None, SLAB_STATES, SLAB_CH), slab),
            pl.BlockSpec((None, SLAB_TILES, LANES), slab),
            pl.BlockSpec((None, SLAB_TILES, LANES), slab),
            pl.BlockSpec((None, 1, SLAB_CH), slab),
        ],
        out_specs=pl.BlockSpec((BATCH, SSM_TC, SLAB_CH), lambda s, c: (0, c, s)),
        out_shape=jax.ShapeDtypeStruct((BATCH, SEQ, SSM_WIDTH), BF16),
        scratch_shapes=[
            pltpu.VMEM((HALF_TILES, SUBLANES * SSM_PITCH, LANES), F32),
            pltpu.VMEM((HALF_TILES, SUBLANES * SSM_PITCH, LANES), F32),
            pltpu.VMEM((HALF_TILES, SUBLANES, LANES), F32),
            pltpu.VMEM((HALF_TILES, SUBLANES, LANES), F32),
        ],
        compiler_params=_cparams(("arbitrary", "arbitrary")),
        name="s5_scan",
    )(proj, bb_re, bb_im, cc_re, cc_im, a_re, a_im, d_skip)


def _ssm_matrices(f_re, f_im, b_re, b_im, c_re, c_im):
    bb_re = f_re[..., None] * b_re - f_im[..., None] * b_im
    bb_im = f_re[..., None] * b_im + f_im[..., None] * b_re
    gps = SSM_GROUPS // SSM_SLABS
    eye = jnp.eye(gps, dtype=F32)

    def in_map(bb):
        bb = bb.reshape(SSM_SLABS, gps, SSM_STATE, SSM_GROUP)
        return jnp.einsum('sgpi,gh->sgihp', bb, eye).reshape(SSM_SLABS, SLAB_CH, SLAB_STATES)

    def out_map(cc):
        cc = cc.reshape(SSM_SLABS, gps, SSM_GROUP, SSM_STATE)
        return jnp.einsum('sgcp,gh->sgphc', cc, eye).reshape(SSM_SLABS, SLAB_STATES, SLAB_CH)

    return (in_map(bb_re).astype(BF16), in_map(bb_im).astype(BF16),
            out_map(c_re).astype(BF16), out_map(-c_im).astype(BF16))


def _mix_kernel(x_ref, mod_ref, o_ref_in, ys_ref, ga_ref, gs_ref, wap_hbm, wglu_hbm, wout_hbm,
                out_ref, wap_ref, wap_stage, wap_sem, wglu_ref, wglu_stage, wglu_sem,
                wout_ref, wout_stage, wout_sem, *, layer):
    @pl.when(_first_grid_step())
    def _():
        _load_weight(wap_hbm, layer, wap_ref, wap_stage, wap_sem)
        _load_weight(wglu_hbm, layer, wglu_ref, wglu_stage, wglu_sem)
        _load_weight(wout_hbm, layer, wout_ref, wout_stage, wout_sem)

    y_attn = _bdot(o_ref_in[...], wap_ref[...])
    gl = jnp.dot(ys_ref[...], wglu_ref[...], preferred_element_type=F32)
    y_ssm = gl[:, :D_MODEL] * jax.nn.sigmoid(gl[:, D_MODEL:])
    mixed = (jax.nn.sigmoid(ga_ref[...].astype(F32)) * y_attn
             + jax.nn.sigmoid(gs_ref[...].astype(F32)) * y_ssm)
    out_ref[...] = x_ref[...] + mod_ref[5:6, :] * _bdot(mixed, wout_ref[...])


def _mix(x, mod_l, attn_o, y_ssm, gates, w_ap, w_glu, w_out, layer):
    rows = lambda b, i: (b, i, 0)
    return pl.pallas_call(
        functools.partial(_mix_kernel, layer=layer),
        grid=(BATCH, SEQ // MIX_TM),
        in_specs=[
            pl.BlockSpec((None, MIX_TM, D_MODEL), rows),
            pl.BlockSpec((None, N_MOD, D_MODEL), lambda b, i: (b, 0, 0)),
            pl.BlockSpec((None, MIX_TM, GROUP_WIDTH), rows),
            pl.BlockSpec((None, MIX_TM, SSM_WIDTH), rows),
            pl.BlockSpec((None, MIX_TM, D_MODEL), rows),
            pl.BlockSpec((None, MIX_TM, D_MODEL), lambda b, i: (b, i, 1)),
            pl.BlockSpec(memory_space=pl.ANY),
            pl.BlockSpec(memory_space=pl.ANY),
            pl.BlockSpec(memory_space=pl.ANY),
        ],
        out_specs=pl.BlockSpec((None, MIX_TM, D_MODEL), rows),
        out_shape=jax.ShapeDtypeStruct((BATCH, SEQ, D_MODEL), F32),
        scratch_shapes=(_weight_scratch(GROUP_WIDTH, D_MODEL, WEIGHT_CHUNKS)
                        + _weight_scratch(SSM_WIDTH, 2 * D_MODEL, WEIGHT_CHUNKS)
                        + _weight_scratch(D_MODEL, D_MODEL, WEIGHT_CHUNKS)),
        compiler_params=_cparams(("arbitrary", "arbitrary")),
        name="mix_out",
    )(x, mod_l, attn_o, y_ssm, gates, gates, w_ap, w_glu, w_out)


def kernel(x, c, w_ada, b_ada, norm_ffn1, w_ffn1_in, w_ffn1_out, norm_mix, w_in, rel_bias, lam_re, lam_im, log_dt, b_re, b_im, c_re, c_im, d_skip, w_glu, w_attn_proj, w_out, norm_ffn2, w_ffn2_in, w_ffn2_out, final_norm):
    c_pad = jnp.zeros((SUBLANES, D_MODEL), F32).at[:BATCH].set(c)
    mod = _ada(c_pad, w_ada, b_ada).reshape(DEPTH, SUBLANES, N_MOD, D_MODEL)
    bias_tabs = _attn_bias_tables(rel_bias)
    for l in range(DEPTH):
        mod_l = mod[l]
        x = _ffn(x, mod_l, norm_ffn1[l], w_ffn1_in, w_ffn1_out, l, 0)

        proj, gates = _inproj(x, mod_l, norm_mix[l], w_in, l)
        attn_o = _attention(proj, bias_tabs)
        a_re, a_im, f_re, f_im = _ssm_params(lam_re[l], lam_im[l], log_dt[l])
        bb_re, bb_im, cc_re, cc_im = _ssm_matrices(f_re, f_im, b_re[l], b_im[l], c_re[l], c_im[l])
        y_ssm = _ssm(proj, bb_re, bb_im, cc_re, cc_im,
                     a_re.reshape(SSM_SLABS, SLAB_TILES, LANES),
                     a_im.reshape(SSM_SLABS, SLAB_TILES, LANES),
                     d_skip[l].reshape(SSM_SLABS, 1, SLAB_CH))
        x = _mix(x, mod_l, attn_o, y_ssm, gates, w_attn_proj, w_glu, w_out, l)

        x = _ffn(x, mod_l, norm_ffn2[l], w_ffn2_in, w_ffn2_out, l, 6,
                 final_norm=final_norm if l == DEPTH - 1 else None)
    return x
```

```python
import functools
import math

import jax
import jax.numpy as jnp
import numpy as np
from jax import lax
from jax.experimental import pallas as pl
from jax.experimental.pallas import tpu as pltpu

D_MODEL = 1024
BATCH = 4
SEQ = 4096
DEPTH = 2
HEAD_DIM = 64
HEADS_PER_GROUP = 8
ATTN_GROUPS = ((128, 1), (512, 4), (2048, 16))
N_GROUPS = len(ATTN_GROUPS)
GROUP_WIDTH = HEADS_PER_GROUP * HEAD_DIM
BLOCK = 128
REL_BUCKETS = 32
REL_MAX_DIST = 2048
NEG = -1e30
SSM_WIDTH = 512
SSM_GROUP = 16
SSM_GROUPS = 32
SSM_STATE = 64
D_FF = 2816
QKV_COLS = 3 * N_GROUPS * GROUP_WIDTH
IN_COLS = QKV_COLS + SSM_WIDTH + 2 * D_MODEL
N_MOD = 9
EPS = 1e-6

LANES = 128
SUBLANES = 8
MXU_DIM = 256
FREE_STRIDE = 4
VMEM_LIMIT = 56 * 1024 * 1024

ADA_TN = 1152
FFN_TM = 1024
FFN_TF = MXU_DIM
PROJ_TM = 512
PROJ_TN = 1024
QKV_WIDTH = 3 * GROUP_WIDTH
MIX_TM = 1024
WEIGHT_CHUNKS = 8
HEAD_PAIRS = GROUP_WIDTH // LANES
ATTN_UNROLL = 16
ATTN_ORDER = (2, 1, 0)
LOG2E = math.log2(math.e)
QSCALE = HEAD_DIM ** -0.5 * LOG2E
N_QBLOCKS = SEQ // BLOCK
SSM_TC = 512
SSM_PITCH = SSM_TC + SUBLANES
SLAB_CH = MXU_DIM
SSM_SLABS = SSM_WIDTH // SLAB_CH
SLAB_STATES = (SSM_GROUPS // SSM_SLABS) * SSM_STATE
SLAB_TILES = SLAB_STATES // LANES
HALF_TILES = SLAB_TILES * BATCH // SUBLANES

BF16 = jnp.bfloat16
F32 = jnp.float32


def _cparams(sem):
    return pltpu.CompilerParams(dimension_semantics=sem, vmem_limit_bytes=VMEM_LIMIT)


def _first_grid_step():
    return (pl.program_id(0) == 0) & (pl.program_id(1) == 0)


def _weight_scratch(k, n, chunks):
    return [pltpu.VMEM((k, n), BF16), pltpu.VMEM((2, k // chunks, n), F32),
            pltpu.SemaphoreType.DMA((2,))]


def _load_weight(w_hbm, layer, w_ref, stage_ref, sem):
    rows = stage_ref.shape[1]
    n_chunks = w_ref.shape[0] // rows

    def chunk(i, slot):
        return pltpu.make_async_copy(w_hbm.at[layer, pl.ds(i * rows, rows), :],
                                     stage_ref.at[slot], sem.at[slot])

    chunk(0, 0).start()

    def body(i, carry):
        slot = i % 2

        @pl.when(i + 1 < n_chunks)
        def _():
            chunk(i + 1, 1 - slot).start()

        chunk(i, slot).wait()
        w_ref[pl.ds(pl.multiple_of(i * rows, rows), rows), :] = stage_ref[slot].astype(BF16)
        return carry

    lax.fori_loop(0, n_chunks, body, 0)


def _bdot(a, b):
    return jnp.dot(a.astype(BF16), b.astype(BF16), preferred_element_type=F32)


def _norm_mod(x, g, shift, scale):
    ms = jnp.mean(x * x, axis=-1, keepdims=True)
    y = x * lax.rsqrt(ms + EPS) * g
    return y * (1.0 + scale) + shift


def _ada_kernel(c_ref, w_ref, b_ref, o_ref):
    c = c_ref[...]
    ca = c * jax.nn.sigmoid(c)
    o_ref[...] = _bdot(ca, w_ref[...]) + b_ref[...]


def _ada(c_pad, w_ada, b_ada):
    n = N_MOD * D_MODEL
    return pl.pallas_call(
        _ada_kernel,
        grid=(DEPTH, n // ADA_TN),
        in_specs=[
            pl.BlockSpec((SUBLANES, D_MODEL), lambda l, j: (0, 0)),
            pl.BlockSpec((None, D_MODEL, ADA_TN), lambda l, j: (l, 0, j)),
            pl.BlockSpec((None, 1, ADA_TN), lambda l, j: (l, 0, j)),
        ],
        out_specs=pl.BlockSpec((None, SUBLANES, ADA_TN), lambda l, j: (l, 0, j)),
        out_shape=jax.ShapeDtypeStruct((DEPTH, SUBLANES, n), F32),
        compiler_params=_cparams(("arbitrary", "arbitrary")),
        name="ada_mod",
    )(c_pad, w_ada, b_ada.reshape(DEPTH, 1, n))


def _ffn_kernel(x_ref, mod_ref, g_ref, win_hbm, wout_hbm, *rest, layer, row0, final):
    if final:
        fn_ref, rest = rest[0], rest[1:]
    o_ref, act_ref, win_ref, win_stage, win_sem, wout_ref, wout_stage, wout_sem = rest

    @pl.when(_first_grid_step())
    def _():
        _load_weight(win_hbm, layer, win_ref, win_stage, win_sem)
        _load_weight(wout_hbm, layer, wout_ref, wout_stage, wout_sem)

    x = x_ref[...]
    h = _norm_mod(x, g_ref[...], mod_ref[row0:row0 + 1, :], mod_ref[row0 + 1:row0 + 2, :])
    hb = h.astype(BF16)
    for j in range(D_FF // FFN_TF):
        a = jnp.dot(hb, win_ref[:, j * FFN_TF:(j + 1) * FFN_TF], preferred_element_type=F32)
        b = jnp.dot(hb, win_ref[:, D_FF + j * FFN_TF:D_FF + (j + 1) * FFN_TF],
                    preferred_element_type=F32)
        act_ref[:, j * FFN_TF:(j + 1) * FFN_TF] = (a * jax.nn.sigmoid(a) * b).astype(BF16)
    y = jnp.dot(act_ref[...], wout_ref[...], preferred_element_type=F32)
    out = x + (0.5 * mod_ref[row0 + 2:row0 + 3, :]) * y
    if final:
        ms = jnp.mean(out * out, axis=-1, keepdims=True)
        out = out * lax.rsqrt(ms + EPS) * fn_ref[...]
    o_ref[...] = out


def _ffn(x, mod_l, norm_g, w_in, w_out, layer, row0, final_norm=None):
    final = final_norm is not None
    in_specs = [
        pl.BlockSpec((None, FFN_TM, D_MODEL), lambda b, i: (b, i, 0)),
        pl.BlockSpec((None, N_MOD, D_MODEL), lambda b, i: (b, 0, 0)),
        pl.BlockSpec((1, D_MODEL), lambda b, i: (0, 0)),
        pl.BlockSpec(memory_space=pl.ANY),
        pl.BlockSpec(memory_space=pl.ANY),
    ]
    args = [x, mod_l, norm_g.reshape(1, D_MODEL), w_in, w_out]
    if final:
        in_specs.append(pl.BlockSpec((1, D_MODEL), lambda b, i: (0, 0)))
        args.append(final_norm.reshape(1, D_MODEL))
    return pl.pallas_call(
        functools.partial(_ffn_kernel, layer=layer, row0=row0, final=final),
        grid=(BATCH, SEQ // FFN_TM),
        in_specs=in_specs,
        out_specs=pl.BlockSpec((None, FFN_TM, D_MODEL), lambda b, i: (b, i, 0)),
        out_shape=jax.ShapeDtypeStruct((BATCH, SEQ, D_MODEL), F32),
        scratch_shapes=([pltpu.VMEM((FFN_TM, D_FF), BF16)]
                        + _weight_scratch(D_MODEL, 2 * D_FF, WEIGHT_CHUNKS)
                        + _weight_scratch(D_FF, D_MODEL, WEIGHT_CHUNKS)),
        compiler_params=_cparams(("arbitrary", "arbitrary")),
        name="ffn_final" if final else "ffn",
    )(*args)


def _store_dilated(out_ref, col0, ys, dilation, rows_ref, half_ref):
    cols = slice(col0, col0 + GROUP_WIDTH)
    if dilation == 1:
        out_ref[0, :, cols] = ys.astype(BF16)
        return
    tiles = GROUP_WIDTH // LANES
    for t in range(tiles):
        rows_ref[t] = ys[:, t * LANES:(t + 1) * LANES]

    def gather(ref, start, count, stride):
        return jnp.concatenate([ref[t, pl.ds(start, count, stride=stride), :]
                                for t in range(tiles)], axis=1).astype(BF16)

    if dilation <= FREE_STRIDE:
        for r in range(dilation):
            out_ref[r, :, cols] = gather(rows_ref, r, PROJ_TM // dilation, dilation)
        return
    outer = dilation // FREE_STRIDE
    assert outer <= FREE_STRIDE
    per_inner = PROJ_TM // FREE_STRIDE
    for b in range(FREE_STRIDE):
        for t in range(tiles):
            half_ref[t, b * per_inner:(b + 1) * per_inner, :] = (
                rows_ref[t, pl.ds(b, per_inner, stride=FREE_STRIDE), :])
    for r in range(dilation):
        a, b = divmod(r, FREE_STRIDE)
        out_ref[r, :, cols] = gather(half_ref, b * per_inner + a, PROJ_TM // dilation, outer)


def _inproj_kernel(x_ref, mod_ref, g_ref, w_hbm, qkv0_ref, qkv1_ref, qkv2_ref, u_ref, gate_ref,
                   w_ref, w_stage, w_sem, rows_ref, half_ref, *, layer):
    @pl.when(_first_grid_step())
    def _():
        _load_weight(w_hbm, layer, w_ref, w_stage, w_sem)

    qkv_refs = (qkv0_ref, qkv1_ref, qkv2_ref)
    h = _norm_mod(x_ref[...], g_ref[...], mod_ref[3:4, :], mod_ref[4:5, :]).astype(BF16)
    for c0 in range(0, IN_COLS, PROJ_TN):
        y = jnp.dot(h, w_ref[:, c0:c0 + PROJ_TN], preferred_element_type=F32)
        for s0 in range(0, PROJ_TN, GROUP_WIDTH):
            seg = (c0 + s0) // GROUP_WIDTH
            ys = y[:, s0:s0 + GROUP_WIDTH]
            if seg < 3 * N_GROUPS:
                which, g = divmod(seg, N_GROUPS)
                if which == 0:
                    ys = ys * QSCALE
                _store_dilated(qkv_refs[g], which * GROUP_WIDTH, ys, ATTN_GROUPS[g][1],
                               rows_ref.at[seg % 2], half_ref.at[seg % 2])
            elif seg == 3 * N_GROUPS:
                u_ref[...] = ys
            else:
                g0 = (seg - 3 * N_GROUPS - 1) * GROUP_WIDTH
                gate_ref[:, g0:g0 + GROUP_WIDTH] = ys.astype(BF16)


def _inproj(x, mod_l, norm_g, w_in, layer):
    rows = lambda b, i: (b, i, 0)
    qkv_specs, qkv_shapes = [], []
    for _, dilation in ATTN_GROUPS:
        qkv_specs.append(pl.BlockSpec((None, dilation, PROJ_TM // dilation, QKV_WIDTH),
                                      lambda b, i: (b, 0, i, 0)))
        qkv_shapes.append(jax.ShapeDtypeStruct((BATCH, dilation, SEQ // dilation, QKV_WIDTH), BF16))
    tiles = GROUP_WIDTH // LANES
    return pl.pallas_call(
        functools.partial(_inproj_kernel, layer=layer),
        grid=(BATCH, SEQ // PROJ_TM),
        in_specs=[
            pl.BlockSpec((None, PROJ_TM, D_MODEL), rows),
            pl.BlockSpec((None, N_MOD, D_MODEL), lambda b, i: (b, 0, 0)),
            pl.BlockSpec((1, D_MODEL), lambda b, i: (0, 0)),
            pl.BlockSpec(memory_space=pl.ANY),
        ],
        out_specs=(*qkv_specs,
                   pl.BlockSpec((None, PROJ_TM, SSM_WIDTH), rows),
                   pl.BlockSpec((None, PROJ_TM, 2 * D_MODEL), rows)),
        out_shape=(*qkv_shapes,
                   jax.ShapeDtypeStruct((BATCH, SEQ, SSM_WIDTH), F32),
                   jax.ShapeDtypeStruct((BATCH, SEQ, 2 * D_MODEL), BF16)),
        scratch_shapes=(_weight_scratch(D_MODEL, IN_COLS, 2 * WEIGHT_CHUNKS)
                        + [pltpu.VMEM((2, tiles, PROJ_TM, LANES), F32),
                           pltpu.VMEM((2, tiles, PROJ_TM, LANES), F32)]),
        compiler_params=_cparams(("arbitrary", "arbitrary")),
        name="inproj",
    )(x, mod_l, norm_g.reshape(1, D_MODEL), w_in)


def _t5_bucket(dist):
    max_exact = REL_BUCKETS // 2
    d = np.maximum(dist, max_exact).astype(np.float32)
    large = max_exact + (np.log(d / max_exact) / np.log(REL_MAX_DIST / max_exact)
                         * (REL_BUCKETS - max_exact)).astype(np.int32)
    large = np.minimum(large, REL_BUCKETS - 1)
    return np.where(dist < max_exact, dist, large).astype(np.int32)


def _attn_bias_tables(rel_bias):
    qi = np.arange(BLOCK)[:, None]
    kj = np.arange(2 * BLOCK)[None, :]
    rel = BLOCK + qi - kj
    tabs = []
    for g, (window, dilation) in enumerate(ATTN_GROUPS):
        band = (rel >= 0) & (rel <= window // dilation)
        bucket = _t5_bucket(np.clip(rel, 0, None) * dilation)
        tbl = rel_bias[:, g * HEADS_PER_GROUP:(g + 1) * HEADS_PER_GROUP]
        onehot = jnp.asarray(bucket[None] == np.arange(REL_BUCKETS)[:, None, None], F32)
        bias = jnp.einsum('rqk,rh->hqk', onehot, tbl.astype(F32),
                          precision=lax.Precision.HIGHEST)
        general = jnp.where(band[None], bias * LOG2E, NEG)
        masked = jnp.full((HEADS_PER_GROUP, BLOCK, BLOCK), NEG, F32)
        first = jnp.concatenate([masked, general[:, :, BLOCK:]], axis=2)
        very_first = jnp.concatenate([general[:, :, BLOCK:], masked], axis=2)
        tabs.append(jnp.stack([very_first, first, general]))
    return jnp.stack(tabs).reshape(N_GROUPS, 3, HEAD_PAIRS, 2 * BLOCK, 2 * BLOCK)


def _attn_group(q_ref, k_ref, v_ref, bias_ref, o_ref, part_ref, *, dilation, slot):
    nb = SEQ // dilation // BLOCK
    lane = lax.broadcasted_iota(jnp.int32, (BLOCK, LANES), 1)
    head0 = lane < HEAD_DIM
    keep0 = jnp.where(head0, 1.0, 0.0).astype(BF16)
    keep1 = jnp.where(head0, 0.0, 1.0).astype(BF16)
    ones_cols = jnp.ones((2 * BLOCK, LANES), BF16)
    contract_last = (((1,), (1,)), ((), ()))

    def token_rows(n):
        start = n // nb + (n % nb) * (BLOCK * dilation)
        if dilation == 1:
            start = pl.multiple_of(start, BLOCK)
        return pl.ds(start, BLOCK, stride=dilation)

    def key_rows(n):
        return pl.ds(pl.multiple_of(jnp.maximum(n - 1, 0) * BLOCK, BLOCK), 2 * BLOCK)

    def logits(n):
        qb = q_ref[pl.ds(pl.multiple_of(n * BLOCK, BLOCK), BLOCK), :]
        qq = jnp.concatenate([qb * keep0, qb * keep1], axis=0)
        tab = jnp.where(n == 0, 0, jnp.where(n % nb == 0, 1, 2))
        l = lax.dot_general(qq, k_ref[key_rows(n), :], contract_last,
                            preferred_element_type=F32) + bias_ref[tab]
        return l, jnp.max(l, axis=1, keepdims=True)

    def weighted(n, l, m_rows):
        v_aug = jnp.concatenate([v_ref[key_rows(n), :], ones_cols], axis=1)
        p = jnp.exp2(l - m_rows).astype(BF16)
        r = jnp.dot(p, v_aug, preferred_element_type=F32)
        return (jnp.where(head0, r[:BLOCK, :LANES], r[BLOCK:, :LANES]),
                jnp.where(head0, r[:BLOCK, LANES:], r[BLOCK:, LANES:]))

    def body(step, carry):
        ns = [step * ATTN_UNROLL + u for u in range(ATTN_UNROLL)]
        scores = [logits(n) for n in ns]
        for n, (l, m_rows) in zip(ns, scores):
            acc, den = weighted(n, l, m_rows)
            m_blk = jnp.where(head0, m_rows[:BLOCK], m_rows[BLOCK:])
            rows = token_rows(n)
            if slot is not None:
                part_ref[slot, 0, rows, :] = acc / den
                part_ref[slot, 1, rows, :] = m_blk + jnp.log2(den)
            else:
                others = [(part_ref[g, 0, rows, :], part_ref[g, 1, rows, :])
                          for g in range(N_GROUPS - 1)]
                m_all = functools.reduce(jnp.maximum, [m_blk] + [lse for _, lse in others])
                w_own = jnp.exp2(m_blk - m_all)
                num = w_own * acc
                den_all = w_own * den
                for out_g, lse in others:
                    w = jnp.exp2(lse - m_all)
                    num = num + w * out_g
                    den_all = den_all + w
                o_ref[rows, :] = num / den_all
        return carry

    lax.fori_loop(0, N_QBLOCKS // ATTN_UNROLL, body, 0)


def _attn_kernel(*refs):
    qkv_refs, (bias_ref, o_ref, part_ref) = refs[:3 * N_GROUPS], refs[3 * N_GROUPS:]
    for i, gi in enumerate(ATTN_ORDER):
        q_ref, k_ref, v_ref = qkv_refs[3 * gi:3 * gi + 3]
        _attn_group(q_ref, k_ref, v_ref, bias_ref.at[gi], o_ref, part_ref,
                    dilation=ATTN_GROUPS[gi][1], slot=i if i < N_GROUPS - 1 else None)


def _attention(qkv_groups, bias_tabs):
    in_specs, args = [], []
    for qkv in qkv_groups:
        for which in range(3):
            in_specs.append(pl.BlockSpec(
                (None, SEQ, LANES), lambda b, hp, which=which: (b, 0, which * HEAD_PAIRS + hp)))
            args.append(qkv)
    in_specs.append(pl.BlockSpec((N_GROUPS, 3, None, 2 * BLOCK, 2 * BLOCK),
                                 lambda b, hp: (0, 0, hp, 0, 0)))
    return pl.pallas_call(
        _attn_kernel,
        grid=(BATCH, HEAD_PAIRS),
        in_specs=in_specs,
        out_specs=pl.BlockSpec((None, SEQ, LANES), lambda b, hp: (b, 0, hp)),
        out_shape=jax.ShapeDtypeStruct((BATCH, SEQ, GROUP_WIDTH), F32),
        scratch_shapes=[pltpu.VMEM((N_GROUPS - 1, 2, SEQ, LANES), F32)],
        compiler_params=_cparams(("arbitrary", "arbitrary")),
        name="dilated_attn",
    )(*args, bias_tabs)


def _ssm_param_kernel(lre_ref, lim_ref, ldt_ref, are_ref, aim_ref, fre_ref, fim_ref):
    lam_re = lre_ref[...]
    lam_im = lim_ref[...]
    dt = jnp.exp(ldt_ref[...])
    mag = jnp.exp(lam_re * dt)
    ang = lam_im * dt
    a_re = mag * jnp.cos(ang)
    a_im = mag * jnp.sin(ang)
    den = lam_re * lam_re + lam_im * lam_im
    are_ref[...] = a_re
    aim_ref[...] = a_im
    fre_ref[...] = ((a_re - 1) * lam_re + a_im * lam_im) / den
    fim_ref[...] = (a_im * lam_re - (a_re - 1) * lam_im) / den


def _ssm_params(lam_re, lam_im, log_dt):
    shp = jax.ShapeDtypeStruct((SSM_GROUPS, SSM_STATE), F32)
    return pl.pallas_call(
        _ssm_param_kernel, out_shape=(shp, shp, shp, shp), name="ssm_discretise",
    )(lam_re, lam_im, log_dt.reshape(SSM_GROUPS, 1))


def _ssm_kernel(u_ref, bre_ref, bim_ref, cre_ref, cim_ref, are_ref, aim_ref, d_ref,
                y_ref, sre_ref, sim_ref, xre_ref, xim_ref):
    c = pl.program_id(1)

    @pl.when(c == 0)
    def _():
        xre_ref[...] = jnp.zeros_like(xre_ref)
        xim_ref[...] = jnp.zeros_like(xim_ref)

    def plane_rows(t, b):
        half, pair = divmod(t, HALF_TILES)
        r0 = (half * BATCH + b) * SSM_PITCH
        return pair, slice(r0, r0 + SSM_TC)

    for b in range(BATCH):
        ub = u_ref[b].astype(BF16)
        bu_re = jnp.dot(ub, bre_ref[...], preferred_element_type=F32)
        bu_im = jnp.dot(ub, bim_ref[...], preferred_element_type=F32)
        for t in range(SLAB_TILES):
            pair, sl = plane_rows(t, b)
            sre_ref[pair, sl, :] = bu_re[:, t * LANES:(t + 1) * LANES]
            sim_ref[pair, sl, :] = bu_im[:, t * LANES:(t + 1) * LANES]

    def coeff(ref, pair):
        return jnp.concatenate(
            [jnp.broadcast_to(ref[half * HALF_TILES + pair:half * HALF_TILES + pair + 1, :],
                              (BATCH, LANES)) for half in range(2)], axis=0)

    a_re = [coeff(are_ref, p) for p in range(HALF_TILES)]
    a_im = [coeff(aim_ref, p) for p in range(HALF_TILES)]

    def step(i, carry):
        xs = list(carry)
        rows = pl.ds(i, SUBLANES, stride=SSM_PITCH)
        for p in range(HALF_TILES):
            xr, xi = xs[2 * p], xs[2 * p + 1]
            nr = a_re[p] * xr - a_im[p] * xi + sre_ref[p, rows, :]
            ni = a_re[p] * xi + a_im[p] * xr + sim_ref[p, rows, :]
            sre_ref[p, rows, :] = nr
            sim_ref[p, rows, :] = ni
            xs[2 * p], xs[2 * p + 1] = nr, ni
        return tuple(xs)

    init = []
    for p in range(HALF_TILES):
        init += [xre_ref[p], xim_ref[p]]
    fin = lax.fori_loop(0, SSM_TC, step, tuple(init), unroll=8)
    for p in range(HALF_TILES):
        xre_ref[p] = fin[2 * p]
        xim_ref[p] = fin[2 * p + 1]

    for b in range(BATCH):
        tiles = [plane_rows(t, b) for t in range(SLAB_TILES)]
        x_re = jnp.concatenate([sre_ref[pair, sl, :] for pair, sl in tiles], axis=1)
        x_im = jnp.concatenate([sim_ref[pair, sl, :] for pair, sl in tiles], axis=1)
        y = (_bdot(x_re, cre_ref[...]) + _bdot(x_im, cim_ref[...])
             + d_ref[...] * u_ref[b])
        y_ref[b] = jax.nn.gelu(y).astype(BF16)


def _ssm(u, bb_re, bb_im, cc_re, cc_im, a_re, a_im, d_skip):
    slab = lambda s, c: (s, 0, 0)
    return pl.pallas_call(
        _ssm_kernel,
        grid=(SSM_SLABS, SEQ // SSM_TC),
        in_specs=[
            pl.BlockSpec((BATCH, SSM_TC, SLAB_CH), lambda s, c: (0, c, s)),
            pl.BlockSpec((None, SLAB_CH, SLAB_STATES), slab),
            pl.BlockSpec((None, SLAB_CH, SLAB_STATES), slab),
            pl.BlockSpec((None, SLAB_STATES, SLAB_CH), slab),
            pl.BlockSpec((None, SLAB_STATES, SLAB_CH), slab),
            pl.BlockSpec((None, SLAB_TILES, LANES), slab),
            pl.BlockSpec((None, SLAB_TILES, LANES), slab),
            pl.BlockSpec((None, 1, SLAB_CH), slab),
        ],
        out_specs=pl.BlockSpec((BATCH, SSM_TC, SLAB_CH), lambda s, c: (0, c, s)),
        out_shape=jax.ShapeDtypeStruct((BATCH, SEQ, SSM_WIDTH), BF16),
        scratch_shapes=[
            pltpu.VMEM((HALF_TILES, SUBLANES * SSM_PITCH, LANES), F32),
            pltpu.VMEM((HALF_TILES, SUBLANES * SSM_PITCH, LANES), F32),
            pltpu.VMEM((HALF_TILES, SUBLANES, LANES), F32),
            pltpu.VMEM((HALF_TILES, SUBLANES, LANES), F32),
        ],
        compiler_params=_cparams(("arbitrary", "arbitrary")),
        name="s5_scan",
    )(u, bb_re, bb_im, cc_re, cc_im, a_re, a_im, d_skip)


def _ssm_matrices(f_re, f_im, b_re, b_im, c_re, c_im):
    bb_re = f_re[..., None] * b_re - f_im[..., None] * b_im
    bb_im = f_re[..., None] * b_im + f_im[..., None] * b_re
    gps = SSM_GROUPS // SSM_SLABS
    eye = jnp.eye(gps, dtype=F32)

    def in_map(bb):
        bb = bb.reshape(SSM_SLABS, gps, SSM_STATE, SSM_GROUP)
        return jnp.einsum('sgpi,gh->sgihp', bb, eye).reshape(SSM_SLABS, SLAB_CH, SLAB_STATES)

    def out_map(cc):
        cc = cc.reshape(SSM_SLABS, gps, SSM_GROUP, SSM_STATE)
        return jnp.einsum('sgcp,gh->sgphc', cc, eye).reshape(SSM_SLABS, SLAB_STATES, SLAB_CH)

    return (in_map(bb_re).astype(BF16), in_map(bb_im).astype(BF16),
            out_map(c_re).astype(BF16), out_map(-c_im).astype(BF16))


def _mix_kernel(x_ref, mod_ref, o_ref_in, ys_ref, ga_ref, gs_ref, wap_hbm, wglu_hbm, wout_hbm,
                out_ref, wap_ref, wap_stage, wap_sem, wglu_ref, wglu_stage, wglu_sem,
                wout_ref, wout_stage, wout_sem, *, layer):
    @pl.when(_first_grid_step())
    def _():
        _load_weight(wap_hbm, layer, wap_ref, wap_stage, wap_sem)
        _load_weight(wglu_hbm, layer, wglu_ref, wglu_stage, wglu_sem)
        _load_weight(wout_hbm, layer, wout_ref, wout_stage, wout_sem)

    y_attn = _bdot(o_ref_in[...], wap_ref[...])
    gl = jnp.dot(ys_ref[...], wglu_ref[...], preferred_element_type=F32)
    y_ssm = gl[:, :D_MODEL] * jax.nn.sigmoid(gl[:, D_MODEL:])
    mixed = (jax.nn.sigmoid(ga_ref[...].astype(F32)) * y_attn
             + jax.nn.sigmoid(gs_ref[...].astype(F32)) * y_ssm)
    out_ref[...] = x_ref[...] + mod_ref[5:6, :] * _bdot(mixed, wout_ref[...])


def _mix(x, mod_l, attn_o, y_ssm, gates, w_ap, w_glu, w_out, layer):
    rows = lambda b, i: (b, i, 0)
    return pl.pallas_call(
        functools.partial(_mix_kernel, layer=layer),
        grid=(BATCH, SEQ // MIX_TM),
        in_specs=[
            pl.BlockSpec((None, MIX_TM, D_MODEL), rows),
            pl.BlockSpec((None, N_MOD, D_MODEL), lambda b, i: (b, 0, 0)),
            pl.BlockSpec((None, MIX_TM, GROUP_WIDTH), rows),
            pl.BlockSpec((None, MIX_TM, SSM_WIDTH), rows),
            pl.BlockSpec((None, MIX_TM, D_MODEL), rows),
            pl.BlockSpec((None, MIX_TM, D_MODEL), lambda b, i: (b, i, 1)),
            pl.BlockSpec(memory_space=pl.ANY),
            pl.BlockSpec(memory_space=pl.ANY),
            pl.BlockSpec(memory_space=pl.ANY),
        ],
        out_specs=pl.BlockSpec((None, MIX_TM, D_MODEL), rows),
        out_shape=jax.ShapeDtypeStruct((BATCH, SEQ, D_MODEL), F32),
        scratch_shapes=(_weight_scratch(GROUP_WIDTH, D_MODEL, WEIGHT_CHUNKS)
                        + _weight_scratch(SSM_WIDTH, 2 * D_MODEL, WEIGHT_CHUNKS)
                        + _weight_scratch(D_MODEL, D_MODEL, WEIGHT_CHUNKS)),
        compiler_params=_cparams(("arbitrary", "arbitrary")),
        name="mix_out",
    )(x, mod_l, attn_o, y_ssm, gates, gates, w_ap, w_glu, w_out)


def kernel(x, c, w_ada, b_ada, norm_ffn1, w_ffn1_in, w_ffn1_out, norm_mix, w_in, rel_bias, lam_re, lam_im, log_dt, b_re, b_im, c_re, c_im, d_skip, w_glu, w_attn_proj, w_out, norm_ffn2, w_ffn2_in, w_ffn2_out, final_norm):
    c_pad = jnp.zeros((SUBLANES, D_MODEL), F32).at[:BATCH].set(c)
    mod = _ada(c_pad, w_ada, b_ada).reshape(DEPTH, SUBLANES, N_MOD, D_MODEL)
    bias_tabs = _attn_bias_tables(rel_bias)
    for l in range(DEPTH):
        mod_l = mod[l]
        x = _ffn(x, mod_l, norm_ffn1[l], w_ffn1_in, w_ffn1_out, l, 0)

        *qkv_groups, u, gates = _inproj(x, mod_l, norm_mix[l], w_in, l)
        attn_o = _attention([qkv.reshape(BATCH, SEQ, QKV_WIDTH) for qkv in qkv_groups], bias_tabs)
        a_re, a_im, f_re, f_im = _ssm_params(lam_re[l], lam_im[l], log_dt[l])
        bb_re, bb_im, cc_re, cc_im = _ssm_matrices(f_re, f_im, b_re[l], b_im[l], c_re[l], c_im[l])
        y_ssm = _ssm(u, bb_re, bb_im, cc_re, cc_im,
                     a_re.reshape(SSM_SLABS, SLAB_TILES, LANES),
                     a_im.reshape(SSM_SLABS, SLAB_TILES, LANES),
                     d_skip[l].reshape(SSM_SLABS, 1, SLAB_CH))
        x = _mix(x, mod_l, attn_o, y_ssm, gates, w_attn_proj, w_glu, w_out, l)

        x = _ffn(x, mod_l, norm_ffn2[l], w_ffn2_in, w_ffn2_out, l, 6,
                 final_norm=final_norm if l == DEPTH - 1 else None)
    return x
```

```python
import functools
import math

import jax
import jax.numpy as jnp
import numpy as np
from jax import lax
from jax.experimental import pallas as pl
from jax.experimental.pallas import tpu as pltpu

D_MODEL = 1024
BATCH = 4
SEQ = 4096
DEPTH = 2
HEAD_DIM = 64
HEADS_PER_GROUP = 8
ATTN_GROUPS = ((128, 1), (512, 4), (2048, 16))
N_GROUPS = len(ATTN_GROUPS)
GROUP_WIDTH = HEADS_PER_GROUP * HEAD_DIM
BLOCK = 128
REL_BUCKETS = 32
REL_MAX_DIST = 2048
NEG = -1e30
SSM_WIDTH = 512
SSM_GROUP = 16
SSM_GROUPS = 32
SSM_STATE = 64
D_FF = 2816
QKV_COLS = 3 * N_GROUPS * GROUP_WIDTH
IN_COLS = QKV_COLS + SSM_WIDTH + 2 * D_MODEL
N_MOD = 9
EPS = 1e-6

LANES = 128
SUBLANES = 8
MXU_DIM = 256
FREE_STRIDE = 4
VMEM_LIMIT = 56 * 1024 * 1024

ADA_TN = 1152
FFN_TM = 1024
FFN_TF = MXU_DIM
PROJ_TM = 512
QKV_WIDTH = 3 * GROUP_WIDTH
MIX_TM = 512
WEIGHT_CHUNKS = 8
HEAD_PAIRS = GROUP_WIDTH // LANES
ATTN_UNROLL = 16
ATTN_ORDER = (2, 1, 0)
LOG2E = math.log2(math.e)
QSCALE = HEAD_DIM ** -0.5 * LOG2E
N_QBLOCKS = SEQ // BLOCK
SSM_TC = 512
SSM_PITCH = SSM_TC + SUBLANES
SLAB_CH = MXU_DIM
SSM_SLABS = SSM_WIDTH // SLAB_CH
SLAB_STATES = (SSM_GROUPS // SSM_SLABS) * SSM_STATE
SLAB_TILES = SLAB_STATES // LANES
HALF_TILES = SLAB_TILES * BATCH // SUBLANES

BF16 = jnp.bfloat16
F32 = jnp.float32


def _cparams(sem):
    return pltpu.CompilerParams(dimension_semantics=sem, vmem_limit_bytes=VMEM_LIMIT)


def _first_grid_step():
    return (pl.program_id(0) == 0) & (pl.program_id(1) == 0)


def _weight_scratch(k, n, chunks):
    return [pltpu.VMEM((k, n), BF16), pltpu.VMEM((2, k // chunks, n), F32),
            pltpu.SemaphoreType.DMA((2,))]


def _load_weight(w_hbm, layer, w_ref, stage_ref, sem):
    rows = stage_ref.shape[1]
    n_chunks = w_ref.shape[0] // rows

    def chunk(i, slot):
        return pltpu.make_async_copy(w_hbm.at[layer, pl.ds(i * rows, rows), :],
                                     stage_ref.at[slot], sem.at[slot])

    chunk(0, 0).start()

    def body(i, carry):
        slot = i % 2

        @pl.when(i + 1 < n_chunks)
        def _():
            chunk(i + 1, 1 - slot).start()

        chunk(i, slot).wait()
        w_ref[pl.ds(pl.multiple_of(i * rows, rows), rows), :] = stage_ref[slot].astype(BF16)
        return carry

    lax.fori_loop(0, n_chunks, body, 0)


def _bdot(a, b):
    return jnp.dot(a.astype(BF16), b.astype(BF16), preferred_element_type=F32)


def _norm_mod(x, g, shift, scale):
    ms = jnp.mean(x * x, axis=-1, keepdims=True)
    y = x * lax.rsqrt(ms + EPS) * g
    return y * (1.0 + scale) + shift


def _ada_kernel(c_ref, w_ref, b_ref, o_ref):
    c = c_ref[...]
    ca = c * jax.nn.sigmoid(c)
    o_ref[...] = _bdot(ca, w_ref[...]) + b_ref[...]


def _ada(c_pad, w_ada, b_ada):
    n = N_MOD * D_MODEL
    return pl.pallas_call(
        _ada_kernel,
        grid=(DEPTH, n // ADA_TN),
        in_specs=[
            pl.BlockSpec((SUBLANES, D_MODEL), lambda l, j: (0, 0)),
            pl.BlockSpec((None, D_MODEL, ADA_TN), lambda l, j: (l, 0, j)),
            pl.BlockSpec((None, 1, ADA_TN), lambda l, j: (l, 0, j)),
        ],
        out_specs=pl.BlockSpec((None, SUBLANES, ADA_TN), lambda l, j: (l, 0, j)),
        out_shape=jax.ShapeDtypeStruct((DEPTH, SUBLANES, n), F32),
        compiler_params=_cparams(("arbitrary", "arbitrary")),
        name="ada_mod",
    )(c_pad, w_ada, b_ada.reshape(DEPTH, 1, n))


def _ffn_kernel(x_ref, mod_ref, g_ref, win_hbm, wout_hbm, *rest, layer, row0, final):
    if final:
        fn_ref, rest = rest[0], rest[1:]
    o_ref, act_ref, win_ref, wout_ref, a_stage, b_stage, o_stage, sem = rest
    n_chunks = D_FF // FFN_TF

    def chunk_copies(j, slot):
        lo, hi = j * FFN_TF, (j + 1) * FFN_TF
        return (pltpu.make_async_copy(win_hbm.at[layer, :, lo:hi], a_stage.at[slot], sem.at[0, slot]),
                pltpu.make_async_copy(win_hbm.at[layer, :, D_FF + lo:D_FF + hi], b_stage.at[slot],
                                      sem.at[1, slot]),
                pltpu.make_async_copy(wout_hbm.at[layer, lo:hi, :], o_stage.at[slot], sem.at[2, slot]))

    def tile(load_weights):
        x = x_ref[...]
        h = _norm_mod(x, g_ref[...], mod_ref[row0:row0 + 1, :], mod_ref[row0 + 1:row0 + 2, :])
        hb = h.astype(BF16)
        if load_weights:
            for copy in chunk_copies(0, 0):
                copy.start()
        for j in range(n_chunks):
            lo, hi = j * FFN_TF, (j + 1) * FFN_TF
            if load_weights:
                if j + 1 < n_chunks:
                    for copy in chunk_copies(j + 1, (j + 1) % 2):
                        copy.start()
                for copy in chunk_copies(j, j % 2):
                    copy.wait()
                win_ref[:, lo:hi] = a_stage[j % 2].astype(BF16)
                win_ref[:, D_FF + lo:D_FF + hi] = b_stage[j % 2].astype(BF16)
                wout_ref[lo:hi, :] = o_stage[j % 2].astype(BF16)
            a = jnp.dot(hb, win_ref[:, lo:hi], preferred_element_type=F32)
            b = jnp.dot(hb, win_ref[:, D_FF + lo:D_FF + hi], preferred_element_type=F32)
            act_ref[:, lo:hi] = (a * jax.nn.sigmoid(a) * b).astype(BF16)
        y = jnp.dot(act_ref[...], wout_ref[...], preferred_element_type=F32)
        out = x + (0.5 * mod_ref[row0 + 2:row0 + 3, :]) * y
        if final:
            ms = jnp.mean(out * out, axis=-1, keepdims=True)
            out = out * lax.rsqrt(ms + EPS) * fn_ref[...]
        o_ref[...] = out

    first = _first_grid_step()
    pl.when(first)(functools.partial(tile, True))
    pl.when(jnp.logical_not(first))(functools.partial(tile, False))


def _ffn(x, mod_l, norm_g, w_in, w_out, layer, row0, final_norm=None):
    final = final_norm is not None
    in_specs = [
        pl.BlockSpec((None, FFN_TM, D_MODEL), lambda b, i: (b, i, 0)),
        pl.BlockSpec((None, N_MOD, D_MODEL), lambda b, i: (b, 0, 0)),
        pl.BlockSpec((1, D_MODEL), lambda b, i: (0, 0)),
        pl.BlockSpec(memory_space=pl.ANY),
        pl.BlockSpec(memory_space=pl.ANY),
    ]
    args = [x, mod_l, norm_g.reshape(1, D_MODEL), w_in, w_out]
    if final:
        in_specs.append(pl.BlockSpec((1, D_MODEL), lambda b, i: (0, 0)))
        args.append(final_norm.reshape(1, D_MODEL))
    return pl.pallas_call(
        functools.partial(_ffn_kernel, layer=layer, row0=row0, final=final),
        grid=(BATCH, SEQ // FFN_TM),
        in_specs=in_specs,
        out_specs=pl.BlockSpec((None, FFN_TM, D_MODEL), lambda b, i: (b, i, 0)),
        out_shape=jax.ShapeDtypeStruct((BATCH, SEQ, D_MODEL), F32),
        scratch_shapes=[
            pltpu.VMEM((FFN_TM, D_FF), BF16),
            pltpu.VMEM((D_MODEL, 2 * D_FF), BF16),
            pltpu.VMEM((D_FF, D_MODEL), BF16),
            pltpu.VMEM((2, D_MODEL, FFN_TF), F32),
            pltpu.VMEM((2, D_MODEL, FFN_TF), F32),
            pltpu.VMEM((2, FFN_TF, D_MODEL), F32),
            pltpu.SemaphoreType.DMA((3, 2)),
        ],
        compiler_params=_cparams(("arbitrary", "arbitrary")),
        name="ffn_final" if final else "ffn",
    )(*args)


def _store_dilated(out_ref, col0, ys, dilation, rows_ref, half_ref):
    cols = slice(col0, col0 + GROUP_WIDTH)
    if dilation == 1:
        out_ref[0, :, cols] = ys.astype(BF16)
        return
    tiles = GROUP_WIDTH // LANES
    for t in range(tiles):
        rows_ref[t] = ys[:, t * LANES:(t + 1) * LANES]

    def gather(ref, start, count, stride):
        return jnp.concatenate([ref[t, pl.ds(start, count, stride=stride), :]
                                for t in range(tiles)], axis=1).astype(BF16)

    if dilation <= FREE_STRIDE:
        for r in range(dilation):
            out_ref[r, :, cols] = gather(rows_ref, r, PROJ_TM // dilation, dilation)
        return
    outer = dilation // FREE_STRIDE
    assert outer <= FREE_STRIDE
    per_inner = PROJ_TM // FREE_STRIDE
    for b in range(FREE_STRIDE):
        for t in range(tiles):
            half_ref[t, b * per_inner:(b + 1) * per_inner, :] = (
                rows_ref[t, pl.ds(b, per_inner, stride=FREE_STRIDE), :])
    for r in range(dilation):
        a, b = divmod(r, FREE_STRIDE)
        out_ref[r, :, cols] = gather(half_ref, b * per_inner + a, PROJ_TM // dilation, outer)


def _inproj_kernel(x_ref, mod_ref, g_ref, w_hbm, qkv0_ref, qkv1_ref, qkv2_ref, u_ref, gate_ref,
                   w_ref, w_stage, w_sem, rows_ref, half_ref, *, layer):
    qkv_refs = (qkv0_ref, qkv1_ref, qkv2_ref)
    n_segs = IN_COLS // GROUP_WIDTH

    def seg_copy(seg, slot):
        return pltpu.make_async_copy(
            w_hbm.at[layer, :, seg * GROUP_WIDTH:(seg + 1) * GROUP_WIDTH], w_stage.at[slot],
            w_sem.at[slot])

    def tile(load_weights):
        h = _norm_mod(x_ref[...], g_ref[...], mod_ref[3:4, :], mod_ref[4:5, :]).astype(BF16)
        if load_weights:
            seg_copy(0, 0).start()
        for seg in range(n_segs):
            cols = slice(seg * GROUP_WIDTH, (seg + 1) * GROUP_WIDTH)
            if load_weights:
                if seg + 1 < n_segs:
                    seg_copy(seg + 1, (seg + 1) % 2).start()
                seg_copy(seg, seg % 2).wait()
                w_ref[:, cols] = w_stage[seg % 2].astype(BF16)
            ys = jnp.dot(h, w_ref[:, cols], preferred_element_type=F32)
            if seg < 3 * N_GROUPS:
                which, g = divmod(seg, N_GROUPS)
                if which == 0:
                    ys = ys * QSCALE
                _store_dilated(qkv_refs[g], which * GROUP_WIDTH, ys, ATTN_GROUPS[g][1],
                               rows_ref.at[seg % 2], half_ref.at[seg % 2])
            elif seg == 3 * N_GROUPS:
                u_ref[...] = ys
            else:
                g0 = (seg - 3 * N_GROUPS - 1) * GROUP_WIDTH
                gate_ref[:, g0:g0 + GROUP_WIDTH] = ys.astype(BF16)

    first = _first_grid_step()
    pl.when(first)(functools.partial(tile, True))
    pl.when(jnp.logical_not(first))(functools.partial(tile, False))


def _inproj(x, mod_l, norm_g, w_in, layer):
    rows = lambda b, i: (b, i, 0)
    qkv_specs, qkv_shapes = [], []
    for _, dilation in ATTN_GROUPS:
        qkv_specs.append(pl.BlockSpec((None, dilation, PROJ_TM // dilation, QKV_WIDTH),
                                      lambda b, i: (b, 0, i, 0)))
        qkv_shapes.append(jax.ShapeDtypeStruct((BATCH, dilation, SEQ // dilation, QKV_WIDTH), BF16))
    tiles = GROUP_WIDTH // LANES
    return pl.pallas_call(
        functools.partial(_inproj_kernel, layer=layer),
        grid=(BATCH, SEQ // PROJ_TM),
        in_specs=[
            pl.BlockSpec((None, PROJ_TM, D_MODEL), rows),
            pl.BlockSpec((None, N_MOD, D_MODEL), lambda b, i: (b, 0, 0)),
            pl.BlockSpec((1, D_MODEL), lambda b, i: (0, 0)),
            pl.BlockSpec(memory_space=pl.ANY),
        ],
        out_specs=(*qkv_specs,
                   pl.BlockSpec((None, PROJ_TM, SSM_WIDTH), rows),
                   pl.BlockSpec((None, PROJ_TM, 2 * D_MODEL), rows)),
        out_shape=(*qkv_shapes,
                   jax.ShapeDtypeStruct((BATCH, SEQ, SSM_WIDTH), F32),
                   jax.ShapeDtypeStruct((BATCH, SEQ, 2 * D_MODEL), BF16)),
        scratch_shapes=[
            pltpu.VMEM((D_MODEL, IN_COLS), BF16),
            pltpu.VMEM((2, D_MODEL, GROUP_WIDTH), F32),
            pltpu.SemaphoreType.DMA((2,)),
            pltpu.VMEM((2, tiles, PROJ_TM, LANES), F32),
            pltpu.VMEM((2, tiles, PROJ_TM, LANES), F32),
        ],
        compiler_params=_cparams(("arbitrary", "arbitrary")),
        name="inproj",
    )(x, mod_l, norm_g.reshape(1, D_MODEL), w_in)


def _t5_bucket(dist):
    max_exact = REL_BUCKETS // 2
    d = np.maximum(dist, max_exact).astype(np.float32)
    large = max_exact + (np.log(d / max_exact) / np.log(REL_MAX_DIST / max_exact)
                         * (REL_BUCKETS - max_exact)).astype(np.int32)
    large = np.minimum(large, REL_BUCKETS - 1)
    return np.where(dist < max_exact, dist, large).astype(np.int32)


def _attn_bias_tables(rel_bias):
    qi = np.arange(BLOCK)[:, None]
    kj = np.arange(2 * BLOCK)[None, :]
    rel = BLOCK + qi - kj
    tabs = []
    for g, (window, dilation) in enumerate(ATTN_GROUPS):
        band = (rel >= 0) & (rel <= window // dilation)
        bucket = _t5_bucket(np.clip(rel, 0, None) * dilation)
        tbl = rel_bias[:, g * HEADS_PER_GROUP:(g + 1) * HEADS_PER_GROUP]
        onehot = jnp.asarray(bucket[None] == np.arange(REL_BUCKETS)[:, None, None], F32)
        bias = jnp.einsum('rqk,rh->hqk', onehot, tbl.astype(F32),
                          precision=lax.Precision.HIGHEST)
        general = jnp.where(band[None], bias * LOG2E, NEG)
        masked = jnp.full((HEADS_PER_GROUP, BLOCK, BLOCK), NEG, F32)
        first = jnp.concatenate([masked, general[:, :, BLOCK:]], axis=2)
        very_first = jnp.concatenate([general[:, :, BLOCK:], masked], axis=2)
        tabs.append(jnp.stack([very_first, first, general]))
    return jnp.stack(tabs).reshape(N_GROUPS, 3, HEAD_PAIRS, 2 * BLOCK, 2 * BLOCK)


def _attn_group(q_ref, k_ref, v_ref, bias_ref, o_ref, part_ref, *, dilation, slot):
    nb = SEQ // dilation // BLOCK
    lane = lax.broadcasted_iota(jnp.int32, (BLOCK, LANES), 1)
    head0 = lane < HEAD_DIM
    keep0 = jnp.where(head0, 1.0, 0.0).astype(BF16)
    keep1 = jnp.where(head0, 0.0, 1.0).astype(BF16)
    ones_cols = jnp.ones((2 * BLOCK, LANES), BF16)
    contract_last = (((1,), (1,)), ((), ()))

    def token_rows(n):
        start = n // nb + (n % nb) * (BLOCK * dilation)
        if dilation == 1:
            start = pl.multiple_of(start, BLOCK)
        return pl.ds(start, BLOCK, stride=dilation)

    def key_rows(n):
        return pl.ds(pl.multiple_of(jnp.maximum(n - 1, 0) * BLOCK, BLOCK), 2 * BLOCK)

    def logits(n):
        qb = q_ref[pl.ds(pl.multiple_of(n * BLOCK, BLOCK), BLOCK), :]
        qq = jnp.concatenate([qb * keep0, qb * keep1], axis=0)
        tab = jnp.where(n == 0, 0, jnp.where(n % nb == 0, 1, 2))
        l = lax.dot_general(qq, k_ref[key_rows(n), :], contract_last,
                            preferred_element_type=F32) + bias_ref[tab]
        return l, jnp.max(l, axis=1, keepdims=True)

    def weighted(n, l, m_rows):
        v_aug = jnp.concatenate([v_ref[key_rows(n), :], ones_cols], axis=1)
        p = jnp.exp2(l - m_rows).astype(BF16)
        r = jnp.dot(p, v_aug, preferred_element_type=F32)
        return (jnp.where(head0, r[:BLOCK, :LANES], r[BLOCK:, :LANES]),
                jnp.where(head0, r[:BLOCK, LANES:], r[BLOCK:, LANES:]))

    def body(step, carry):
        ns = [step * ATTN_UNROLL + u for u in range(ATTN_UNROLL)]
        scores = [logits(n) for n in ns]
        for n, (l, m_rows) in zip(ns, scores):
            acc, den = weighted(n, l, m_rows)
            m_blk = jnp.where(head0, m_rows[:BLOCK], m_rows[BLOCK:])
            rows = token_rows(n)
            if slot is not None:
                part_ref[slot, 0, rows, :] = acc / den
                part_ref[slot, 1, rows, :] = m_blk + jnp.log2(den)
            else:
                others = [(part_ref[g, 0, rows, :], part_ref[g, 1, rows, :])
                          for g in range(N_GROUPS - 1)]
                m_all = functools.reduce(jnp.maximum, [m_blk] + [lse for _, lse in others])
                w_own = jnp.exp2(m_blk - m_all)
                num = w_own * acc
                den_all = w_own * den
                for out_g, lse in others:
                    w = jnp.exp2(lse - m_all)
                    num = num + w * out_g
                    den_all = den_all + w
                o_ref[rows, :] = num / den_all
        return carry

    lax.fori_loop(0, N_QBLOCKS // ATTN_UNROLL, body, 0)


def _attn_kernel(*refs):
    qkv_refs, (bias_ref, o_ref, part_ref) = refs[:3 * N_GROUPS], refs[3 * N_GROUPS:]
    for i, gi in enumerate(ATTN_ORDER):
        q_ref, k_ref, v_ref = qkv_refs[3 * gi:3 * gi + 3]
        _attn_group(q_ref, k_ref, v_ref, bias_ref.at[gi], o_ref, part_ref,
                    dilation=ATTN_GROUPS[gi][1], slot=i if i < N_GROUPS - 1 else None)


def _attention(qkv_groups, bias_tabs):
    in_specs, args = [], []
    for qkv in qkv_groups:
        for which in range(3):
            in_specs.append(pl.BlockSpec(
                (None, SEQ, LANES), lambda b, hp, which=which: (b, 0, which * HEAD_PAIRS + hp)))
            args.append(qkv)
    in_specs.append(pl.BlockSpec((N_GROUPS, 3, None, 2 * BLOCK, 2 * BLOCK),
                                 lambda b, hp: (0, 0, hp, 0, 0)))
    return pl.pallas_call(
        _attn_kernel,
        grid=(BATCH, HEAD_PAIRS),
        in_specs=in_specs,
        out_specs=pl.BlockSpec((None, SEQ, LANES), lambda b, hp: (b, 0, hp)),
        out_shape=jax.ShapeDtypeStruct((BATCH, SEQ, GROUP_WIDTH), F32),
        scratch_shapes=[pltpu.VMEM((N_GROUPS - 1, 2, SEQ, LANES), F32)],
        compiler_params=_cparams(("arbitrary", "arbitrary")),
        name="dilated_attn",
    )(*args, bias_tabs)


def _ssm_param_kernel(lre_ref, lim_ref, ldt_ref, are_ref, aim_ref, fre_ref, fim_ref):
    lam_re = lre_ref[...]
    lam_im = lim_ref[...]
    dt = jnp.exp(ldt_ref[...])
    mag = jnp.exp(lam_re * dt)
    ang = lam_im * dt
    a_re = mag * jnp.cos(ang)
    a_im = mag * jnp.sin(ang)
    den = lam_re * lam_re + lam_im * lam_im
    are_ref[...] = a_re
    aim_ref[...] = a_im
    fre_ref[...] = ((a_re - 1) * lam_re + a_im * lam_im) / den
    fim_ref[...] = (a_im * lam_re - (a_re - 1) * lam_im) / den


def _ssm_params(lam_re, lam_im, log_dt):
    shp = jax.ShapeDtypeStruct((SSM_GROUPS, SSM_STATE), F32)
    return pl.pallas_call(
        _ssm_param_kernel, out_shape=(shp, shp, shp, shp), name="ssm_discretise",
    )(lam_re, lam_im, log_dt.reshape(SSM_GROUPS, 1))


def _ssm_kernel(u_ref, bre_ref, bim_ref, cre_ref, cim_ref, are_ref, aim_ref, d_ref,
                y_ref, sre_ref, sim_ref, xre_ref, xim_ref):
    c = pl.program_id(1)

    @pl.when(c == 0)
    def _():
        xre_ref[...] = jnp.zeros_like(xre_ref)
        xim_ref[...] = jnp.zeros_like(xim_ref)

    def plane_rows(t, b):
        half, pair = divmod(t, HALF_TILES)
        r0 = (half * BATCH + b) * SSM_PITCH
        return pair, slice(r0, r0 + SSM_TC)

    for b in range(BATCH):
        ub = u_ref[b].astype(BF16)
        bu_re = jnp.dot(ub, bre_ref[...], preferred_element_type=F32)
        bu_im = jnp.dot(ub, bim_ref[...], preferred_element_type=F32)
        for t in range(SLAB_TILES):
            pair, sl = plane_rows(t, b)
            sre_ref[pair, sl, :] = bu_re[:, t * LANES:(t + 1) * LANES]
            sim_ref[pair, sl, :] = bu_im[:, t * LANES:(t + 1) * LANES]

    def coeff(ref, pair):
        return jnp.concatenate(
            [jnp.broadcast_to(ref[half * HALF_TILES + pair:half * HALF_TILES + pair + 1, :],
                              (BATCH, LANES)) for half in range(2)], axis=0)

    a_re = [coeff(are_ref, p) for p in range(HALF_TILES)]
    a_im = [coeff(aim_ref, p) for p in range(HALF_TILES)]

    def step(i, carry):
        xs = list(carry)
        rows = pl.ds(i, SUBLANES, stride=SSM_PITCH)
        for p in range(HALF_TILES):
            xr, xi = xs[2 * p], xs[2 * p + 1]
            nr = a_re[p] * xr - a_im[p] * xi + sre_ref[p, rows, :]
            ni = a_re[p] * xi + a_im[p] * xr + sim_ref[p, rows, :]
            sre_ref[p, rows, :] = nr
            sim_ref[p, rows, :] = ni
            xs[2 * p], xs[2 * p + 1] = nr, ni
        return tuple(xs)

    init = []
    for p in range(HALF_TILES):
        init += [xre_ref[p], xim_ref[p]]
    fin = lax.fori_loop(0, SSM_TC, step, tuple(init), unroll=8)
    for p in range(HALF_TILES):
        xre_ref[p] = fin[2 * p]
        xim_ref[p] = fin[2 * p + 1]

    for b in range(BATCH):
        tiles = [plane_rows(t, b) for t in range(SLAB_TILES)]
        x_re = jnp.concatenate([sre_ref[pair, sl, :] for pair, sl in tiles], axis=1)
        x_im = jnp.concatenate([sim_ref[pair, sl, :] for pair, sl in tiles], axis=1)
        y = (_bdot(x_re, cre_ref[...]) + _bdot(x_im, cim_ref[...])
             + d_ref[...] * u_ref[b])
        y_ref[b] = jax.nn.gelu(y).astype(BF16)


def _ssm(u, bb_re, bb_im, cc_re, cc_im, a_re, a_im, d_skip):
    slab = lambda s, c: (s, 0, 0)
    return pl.pallas_call(
        _ssm_kernel,
        grid=(SSM_SLABS, SEQ // SSM_TC),
        in_specs=[
            pl.BlockSpec((BATCH, SSM_TC, SLAB_CH), lambda s, c: (0, c, s)),
            pl.BlockSpec((None, SLAB_CH, SLAB_STATES), slab),
            pl.BlockSpec((None, SLAB_CH, SLAB_STATES), slab),
            pl.BlockSpec((None, SLAB_STATES, SLAB_CH), slab),
            pl.BlockSpec((None, SLAB_STATES, SLAB_CH), slab),
            pl.BlockSpec((None, SLAB_TILES, LANES), slab),
            pl.BlockSpec((None, SLAB_TILES, LANES), slab),
            pl.BlockSpec((None, 1, SLAB_CH), slab),
        ],
        out_specs=pl.BlockSpec((BATCH, SSM_TC, SLAB_CH), lambda s, c: (0, c, s)),
        out_shape=jax.ShapeDtypeStruct((BATCH, SEQ, SSM_WIDTH), BF16),
        scratch_shapes=[
            pltpu.VMEM((HALF_TILES, SUBLANES * SSM_PITCH, LANES), F32),
            pltpu.VMEM((HALF_TILES, SUBLANES * SSM_PITCH, LANES), F32),
            pltpu.VMEM((HALF_TILES, SUBLANES, LANES), F32),
            pltpu.VMEM((HALF_TILES, SUBLANES, LANES), F32),
        ],
        compiler_params=_cparams(("arbitrary", "arbitrary")),
        name="s5_scan",
    )(u, bb_re, bb_im, cc_re, cc_im, a_re, a_im, d_skip)


def _ssm_matrices(f_re, f_im, b_re, b_im, c_re, c_im):
    bb_re = f_re[..., None] * b_re - f_im[..., None] * b_im
    bb_im = f_re[..., None] * b_im + f_im[..., None] * b_re
    gps = SSM_GROUPS // SSM_SLABS
    eye = jnp.eye(gps, dtype=F32)

    def in_map(bb):
        bb = bb.reshape(SSM_SLABS, gps, SSM_STATE, SSM_GROUP)
        return jnp.einsum('sgpi,gh->sgihp', bb, eye).reshape(SSM_SLABS, SLAB_CH, SLAB_STATES)

    def out_map(cc):
        cc = cc.reshape(SSM_SLABS, gps, SSM_GROUP, SSM_STATE)
        return jnp.einsum('sgcp,gh->sgphc', cc, eye).reshape(SSM_SLABS, SLAB_STATES, SLAB_CH)

    return (in_map(bb_re).astype(BF16), in_map(bb_im).astype(BF16),
            out_map(c_re).astype(BF16), out_map(-c_im).astype(BF16))


def _mix_kernel(x_ref, mod_ref, o_ref_in, ys_ref, ga_ref, gs_ref, wap_hbm, wglu_hbm, wout_hbm,
                out_ref, wap_ref, wap_stage, wap_sem, wglu_ref, wglu_stage, wglu_sem,
                wout_ref, wout_stage, wout_sem, *, layer):
    @pl.when(_first_grid_step())
    def _():
        _load_weight(wap_hbm, layer, wap_ref, wap_stage, wap_sem)
        _load_weight(wglu_hbm, layer, wglu_ref, wglu_stage, wglu_sem)
        _load_weight(wout_hbm, layer, wout_ref, wout_stage, wout_sem)

    y_attn = _bdot(o_ref_in[...], wap_ref[...])
    gl = jnp.dot(ys_ref[...], wglu_ref[...], preferred_element_type=F32)
    y_ssm = gl[:, :D_MODEL] * jax.nn.sigmoid(gl[:, D_MODEL:])
    mixed = (jax.nn.sigmoid(ga_ref[...].astype(F32)) * y_attn
             + jax.nn.sigmoid(gs_ref[...].astype(F32)) * y_ssm)
    out_ref[...] = x_ref[...] + mod_ref[5:6, :] * _bdot(mixed, wout_ref[...])


def _mix(x, mod_l, attn_o, y_ssm, gates, w_ap, w_glu, w_out, layer):
    rows = lambda b, i: (b, i, 0)
    return pl.pallas_call(
        functools.partial(_mix_kernel, layer=layer),
        grid=(BATCH, SEQ // MIX_TM),
        in_specs=[
            pl.BlockSpec((None, MIX_TM, D_MODEL), rows),
            pl.BlockSpec((None, N_MOD, D_MODEL), lambda b, i: (b, 0, 0)),
            pl.BlockSpec((None, MIX_TM, GROUP_WIDTH), rows),
            pl.BlockSpec((None, MIX_TM, SSM_WIDTH), rows),
            pl.BlockSpec((None, MIX_TM, D_MODEL), rows),
            pl.BlockSpec((None, MIX_TM, D_MODEL), lambda b, i: (b, i, 1)),
            pl.BlockSpec(memory_space=pl.ANY),
            pl.BlockSpec(memory_space=pl.ANY),
            pl.BlockSpec(memory_space=pl.ANY),
        ],
        out_specs=pl.BlockSpec((None, MIX_TM, D_MODEL), rows),
        out_shape=jax.ShapeDtypeStruct((BATCH, SEQ, D_MODEL), F32),
        scratch_shapes=(_weight_scratch(GROUP_WIDTH, D_MODEL, WEIGHT_CHUNKS)
                        + _weight_scratch(SSM_WIDTH, 2 * D_MODEL, WEIGHT_CHUNKS)
                        + _weight_scratch(D_MODEL, D_MODEL, WEIGHT_CHUNKS)),
        compiler_params=_cparams(("arbitrary", "arbitrary")),
        name="mix_out",
    )(x, mod_l, attn_o, y_ssm, gates, gates, w_ap, w_glu, w_out)


def kernel(x, c, w_ada, b_ada, norm_ffn1, w_ffn1_in, w_ffn1_out, norm_mix, w_in, rel_bias, lam_re, lam_im, log_dt, b_re, b_im, c_re, c_im, d_skip, w_glu, w_attn_proj, w_out, norm_ffn2, w_ffn2_in, w_ffn2_out, final_norm):
    c_pad = jnp.zeros((SUBLANES, D_MODEL), F32).at[:BATCH].set(c)
    mod = _ada(c_pad, w_ada, b_ada).reshape(DEPTH, SUBLANES, N_MOD, D_MODEL)
    bias_tabs = _attn_bias_tables(rel_bias)
    for l in range(DEPTH):
        mod_l = mod[l]
        x = _ffn(x, mod_l, norm_ffn1[l], w_ffn1_in, w_ffn1_out, l, 0)

        *qkv_groups, u, gates = _inproj(x, mod_l, norm_mix[l], w_in, l)
        attn_o = _attention([qkv.reshape(BATCH, SEQ, QKV_WIDTH) for qkv in qkv_groups], bias_tabs)
        a_re, a_im, f_re, f_im = _ssm_params(lam_re[l], lam_im[l], log_dt[l])
        bb_re, bb_im, cc_re, cc_im = _ssm_matrices(f_re, f_im, b_re[l], b_im[l], c_re[l], c_im[l])
        y_ssm = _ssm(u, bb_re, bb_im, cc_re, cc_im,
                     a_re.reshape(SSM_SLABS, SLAB_TILES, LANES),
                     a_im.reshape(SSM_SLABS, SLAB_TILES, LANES),
                     d_skip[l].reshape(SSM_SLABS, 1, SLAB_CH))
        x = _mix(x, mod_l, attn_o, y_ssm, gates, w_attn_proj, w_glu, w_out, l)

        x = _ffn(x, mod_l, norm_ffn2[l], w_ffn2_in, w_ffn2_out, l, 6,
                 final_norm=final_norm if l == DEPTH - 1 else None)
    return x
```

```python
import functools
import math

import jax
import jax.numpy as jnp
import numpy as np
from jax import lax
from jax.experimental import pallas as pl
from jax.experimental.pallas import tpu as pltpu

D_MODEL = 1024
BATCH = 4
SEQ = 4096
DEPTH = 2
HEAD_DIM = 64
HEADS_PER_GROUP = 8
ATTN_GROUPS = ((128, 1), (512, 4), (2048, 16))
N_GROUPS = len(ATTN_GROUPS)
GROUP_WIDTH = HEADS_PER_GROUP * HEAD_DIM
BLOCK = 128
REL_BUCKETS = 32
REL_MAX_DIST = 2048
NEG = -1e30
SSM_WIDTH = 512
SSM_GROUP = 16
SSM_GROUPS = 32
SSM_STATE = 64
D_FF = 2816
QKV_COLS = 3 * N_GROUPS * GROUP_WIDTH
IN_COLS = QKV_COLS + SSM_WIDTH + 2 * D_MODEL
N_MOD = 9
EPS = 1e-6

LANES = 128
SUBLANES = 8
MXU_DIM = 256
FREE_STRIDE = 4
VMEM_LIMIT = 56 * 1024 * 1024

ADA_TN = 1152
FFN_TM = 1024
FFN_TF = MXU_DIM
PROJ_TM = 512
QKV_WIDTH = 3 * GROUP_WIDTH
MIX_TM = 512
WEIGHT_CHUNKS = 8
HEAD_PAIRS = GROUP_WIDTH // LANES
ATTN_UNROLL = 16
ATTN_ORDER = (2, 1, 0)
LOG2E = math.log2(math.e)
QSCALE = HEAD_DIM ** -0.5 * LOG2E
N_QBLOCKS = SEQ // BLOCK
SSM_TC = 512
SSM_PITCH = SSM_TC + SUBLANES
SLAB_CH = MXU_DIM
SSM_SLABS = SSM_WIDTH // SLAB_CH
SLAB_STATES = (SSM_GROUPS // SSM_SLABS) * SSM_STATE
SLAB_TILES = SLAB_STATES // LANES
HALF_TILES = SLAB_TILES * BATCH // SUBLANES

BF16 = jnp.bfloat16
F32 = jnp.float32


def _cparams(sem):
    return pltpu.CompilerParams(dimension_semantics=sem, vmem_limit_bytes=VMEM_LIMIT)


def _first_grid_step():
    return (pl.program_id(0) == 0) & (pl.program_id(1) == 0)


def _weight_scratch(k, n, chunks):
    return [pltpu.VMEM((k, n), BF16), pltpu.VMEM((2, k // chunks, n), F32),
            pltpu.SemaphoreType.DMA((2,))]


def _load_weight(w_hbm, layer, w_ref, stage_ref, sem):
    rows = stage_ref.shape[1]
    n_chunks = w_ref.shape[0] // rows

    def chunk(i, slot):
        return pltpu.make_async_copy(w_hbm.at[layer, pl.ds(i * rows, rows), :],
                                     stage_ref.at[slot], sem.at[slot])

    chunk(0, 0).start()

    def body(i, carry):
        slot = i % 2

        @pl.when(i + 1 < n_chunks)
        def _():
            chunk(i + 1, 1 - slot).start()

        chunk(i, slot).wait()
        w_ref[pl.ds(pl.multiple_of(i * rows, rows), rows), :] = stage_ref[slot].astype(BF16)
        return carry

    lax.fori_loop(0, n_chunks, body, 0)


def _bdot(a, b):
    return jnp.dot(a.astype(BF16), b.astype(BF16), preferred_element_type=F32)


def _sigmoid(x):
    return 0.5 * jnp.tanh(0.5 * x) + 0.5


def _norm_mod(x, g, shift, scale):
    ms = jnp.mean(x * x, axis=-1, keepdims=True)
    y = x * lax.rsqrt(ms + EPS) * g
    return y * (1.0 + scale) + shift


def _ada_kernel(c_ref, w_ref, b_ref, o_ref):
    c = c_ref[...]
    ca = c * _sigmoid(c)
    o_ref[...] = _bdot(ca, w_ref[...]) + b_ref[...]


def _ada(c_pad, w_ada, b_ada):
    n = N_MOD * D_MODEL
    return pl.pallas_call(
        _ada_kernel,
        grid=(DEPTH, n // ADA_TN),
        in_specs=[
            pl.BlockSpec((SUBLANES, D_MODEL), lambda l, j: (0, 0)),
            pl.BlockSpec((None, D_MODEL, ADA_TN), lambda l, j: (l, 0, j)),
            pl.BlockSpec((None, 1, ADA_TN), lambda l, j: (l, 0, j)),
        ],
        out_specs=pl.BlockSpec((None, SUBLANES, ADA_TN), lambda l, j: (l, 0, j)),
        out_shape=jax.ShapeDtypeStruct((DEPTH, SUBLANES, n), F32),
        compiler_params=_cparams(("arbitrary", "arbitrary")),
        name="ada_mod",
    )(c_pad, w_ada, b_ada.reshape(DEPTH, 1, n))


def _ffn_kernel(x_ref, mod_ref, g_ref, win_hbm, wout_hbm, *rest, layer, row0, final):
    if final:
        fn_ref, rest = rest[0], rest[1:]
    o_ref, act_ref, win_ref, wout_ref, a_stage, b_stage, o_stage, sem = rest
    n_chunks = D_FF // FFN_TF

    def chunk_copies(j, slot):
        lo, hi = j * FFN_TF, (j + 1) * FFN_TF
        return (pltpu.make_async_copy(win_hbm.at[layer, :, lo:hi], a_stage.at[slot], sem.at[0, slot]),
                pltpu.make_async_copy(win_hbm.at[layer, :, D_FF + lo:D_FF + hi], b_stage.at[slot],
                                      sem.at[1, slot]),
                pltpu.make_async_copy(wout_hbm.at[layer, lo:hi, :], o_stage.at[slot], sem.at[2, slot]))

    def tile(load_weights):
        x = x_ref[...]
        h = _norm_mod(x, g_ref[...], mod_ref[row0:row0 + 1, :], mod_ref[row0 + 1:row0 + 2, :])
        hb = h.astype(BF16)
        if load_weights:
            for copy in chunk_copies(0, 0):
                copy.start()
        for j in range(n_chunks):
            lo, hi = j * FFN_TF, (j + 1) * FFN_TF
            if load_weights:
                if j + 1 < n_chunks:
                    for copy in chunk_copies(j + 1, (j + 1) % 2):
                        copy.start()
                for copy in chunk_copies(j, j % 2):
                    copy.wait()
                win_ref[:, lo:hi] = a_stage[j % 2].astype(BF16)
                win_ref[:, D_FF + lo:D_FF + hi] = b_stage[j % 2].astype(BF16)
                wout_ref[lo:hi, :] = o_stage[j % 2].astype(BF16)
            a = jnp.dot(hb, win_ref[:, lo:hi], preferred_element_type=F32)
            b = jnp.dot(hb, win_ref[:, D_FF + lo:D_FF + hi], preferred_element_type=F32)
            act_ref[:, lo:hi] = (a * _sigmoid(a) * b).astype(BF16)
        y = jnp.dot(act_ref[...], wout_ref[...], preferred_element_type=F32)
        out = x + (0.5 * mod_ref[row0 + 2:row0 + 3, :]) * y
        if final:
            ms = jnp.mean(out * out, axis=-1, keepdims=True)
            out = out * lax.rsqrt(ms + EPS) * fn_ref[...]
        o_ref[...] = out

    first = _first_grid_step()
    pl.when(first)(functools.partial(tile, True))
    pl.when(jnp.logical_not(first))(functools.partial(tile, False))


def _ffn(x, mod, norm_g, w_in, w_out, layer, row0, final_norm=None):
    final = final_norm is not None
    in_specs = [
        pl.BlockSpec((None, FFN_TM, D_MODEL), lambda b, i: (b, i, 0)),
        pl.BlockSpec((None, None, N_MOD, D_MODEL), lambda b, i: (layer, b, 0, 0)),
        pl.BlockSpec((None, 1, D_MODEL), lambda b, i: (layer, 0, 0)),
        pl.BlockSpec(memory_space=pl.ANY),
        pl.BlockSpec(memory_space=pl.ANY),
    ]
    args = [x, mod, norm_g.reshape(DEPTH, 1, D_MODEL), w_in, w_out]
    if final:
        in_specs.append(pl.BlockSpec((1, D_MODEL), lambda b, i: (0, 0)))
        args.append(final_norm.reshape(1, D_MODEL))
    return pl.pallas_call(
        functools.partial(_ffn_kernel, layer=layer, row0=row0, final=final),
        grid=(BATCH, SEQ // FFN_TM),
        in_specs=in_specs,
        out_specs=pl.BlockSpec((None, FFN_TM, D_MODEL), lambda b, i: (b, i, 0)),
        out_shape=jax.ShapeDtypeStruct((BATCH, SEQ, D_MODEL), F32),
        scratch_shapes=[
            pltpu.VMEM((FFN_TM, D_FF), BF16),
            pltpu.VMEM((D_MODEL, 2 * D_FF), BF16),
            pltpu.VMEM((D_FF, D_MODEL), BF16),
            pltpu.VMEM((2, D_MODEL, FFN_TF), F32),
            pltpu.VMEM((2, D_MODEL, FFN_TF), F32),
            pltpu.VMEM((2, FFN_TF, D_MODEL), F32),
            pltpu.SemaphoreType.DMA((3, 2)),
        ],
        compiler_params=_cparams(("arbitrary", "arbitrary")),
        name="ffn_final" if final else "ffn",
    )(*args)


def _store_dilated(out_ref, col0, ys, dilation, rows_ref, half_ref):
    cols = slice(col0, col0 + GROUP_WIDTH)
    if dilation == 1:
        out_ref[0, :, cols] = ys.astype(BF16)
        return
    tiles = GROUP_WIDTH // LANES
    for t in range(tiles):
        rows_ref[t] = ys[:, t * LANES:(t + 1) * LANES]

    def gather(ref, start, count, stride):
        return jnp.concatenate([ref[t, pl.ds(start, count, stride=stride), :]
                                for t in range(tiles)], axis=1).astype(BF16)

    if dilation <= FREE_STRIDE:
        for r in range(dilation):
            out_ref[r, :, cols] = gather(rows_ref, r, PROJ_TM // dilation, dilation)
        return
    outer = dilation // FREE_STRIDE
    assert outer <= FREE_STRIDE
    per_inner = PROJ_TM // FREE_STRIDE
    for b in range(FREE_STRIDE):
        for t in range(tiles):
            half_ref[t, b * per_inner:(b + 1) * per_inner, :] = (
                rows_ref[t, pl.ds(b, per_inner, stride=FREE_STRIDE), :])
    for r in range(dilation):
        a, b = divmod(r, FREE_STRIDE)
        out_ref[r, :, cols] = gather(half_ref, b * per_inner + a, PROJ_TM // dilation, outer)


def _inproj_kernel(x_ref, mod_ref, g_ref, w_hbm, qkv0_ref, qkv1_ref, qkv2_ref, u_ref, gate_ref,
                   w_ref, w_stage, w_sem, rows_ref, half_ref, *, layer):
    qkv_refs = (qkv0_ref, qkv1_ref, qkv2_ref)
    n_segs = IN_COLS // GROUP_WIDTH

    def seg_copy(seg, slot):
        return pltpu.make_async_copy(
            w_hbm.at[layer, :, seg * GROUP_WIDTH:(seg + 1) * GROUP_WIDTH], w_stage.at[slot],
            w_sem.at[slot])

    def tile(load_weights):
        h = _norm_mod(x_ref[...], g_ref[...], mod_ref[3:4, :], mod_ref[4:5, :]).astype(BF16)
        if load_weights:
            seg_copy(0, 0).start()
        for seg in range(n_segs):
            cols = slice(seg * GROUP_WIDTH, (seg + 1) * GROUP_WIDTH)
            if load_weights:
                if seg + 1 < n_segs:
                    seg_copy(seg + 1, (seg + 1) % 2).start()
                seg_copy(seg, seg % 2).wait()
                w_ref[:, cols] = w_stage[seg % 2].astype(BF16)
            ys = jnp.dot(h, w_ref[:, cols], preferred_element_type=F32)
            if seg < 3 * N_GROUPS:
                which, g = divmod(seg, N_GROUPS)
                if which == 0:
                    ys = ys * QSCALE
                _store_dilated(qkv_refs[g], which * GROUP_WIDTH, ys, ATTN_GROUPS[g][1],
                               rows_ref.at[seg % 2], half_ref.at[seg % 2])
            elif seg == 3 * N_GROUPS:
                u_ref[...] = ys
            else:
                g0 = (seg - 3 * N_GROUPS - 1) * GROUP_WIDTH
                gate_ref[:, g0:g0 + GROUP_WIDTH] = ys.astype(BF16)

    first = _first_grid_step()
    pl.when(first)(functools.partial(tile, True))
    pl.when(jnp.logical_not(first))(functools.partial(tile, False))


def _inproj(x, mod, norm_g, w_in, layer):
    rows = lambda b, i: (b, i, 0)
    qkv_specs, qkv_shapes = [], []
    for _, dilation in ATTN_GROUPS:
        qkv_specs.append(pl.BlockSpec((None, dilation, PROJ_TM // dilation, QKV_WIDTH),
                                      lambda b, i: (b, 0, i, 0)))
        qkv_shapes.append(jax.ShapeDtypeStruct((BATCH, dilation, SEQ // dilation, QKV_WIDTH), BF16))
    tiles = GROUP_WIDTH // LANES
    return pl.pallas_call(
        functools.partial(_inproj_kernel, layer=layer),
        grid=(BATCH, SEQ // PROJ_TM),
        in_specs=[
            pl.BlockSpec((None, PROJ_TM, D_MODEL), rows),
            pl.BlockSpec((None, None, N_MOD, D_MODEL), lambda b, i: (layer, b, 0, 0)),
            pl.BlockSpec((None, 1, D_MODEL), lambda b, i: (layer, 0, 0)),
            pl.BlockSpec(memory_space=pl.ANY),
        ],
        out_specs=(*qkv_specs,
                   pl.BlockSpec((None, PROJ_TM, SSM_WIDTH), rows),
                   pl.BlockSpec((None, PROJ_TM, 2 * D_MODEL), rows)),
        out_shape=(*qkv_shapes,
                   jax.ShapeDtypeStruct((BATCH, SEQ, SSM_WIDTH), F32),
                   jax.ShapeDtypeStruct((BATCH, SEQ, 2 * D_MODEL), BF16)),
        scratch_shapes=[
            pltpu.VMEM((D_MODEL, IN_COLS), BF16),
            pltpu.VMEM((2, D_MODEL, GROUP_WIDTH), F32),
            pltpu.SemaphoreType.DMA((2,)),
            pltpu.VMEM((2, tiles, PROJ_TM, LANES), F32),
            pltpu.VMEM((2, tiles, PROJ_TM, LANES), F32),
        ],
        compiler_params=_cparams(("arbitrary", "arbitrary")),
        name="inproj",
    )(x, mod, norm_g.reshape(DEPTH, 1, D_MODEL), w_in)


def _t5_bucket(dist):
    max_exact = REL_BUCKETS // 2
    d = np.maximum(dist, max_exact).astype(np.float32)
    large = max_exact + (np.log(d / max_exact) / np.log(REL_MAX_DIST / max_exact)
                         * (REL_BUCKETS - max_exact)).astype(np.int32)
    large = np.minimum(large, REL_BUCKETS - 1)
    return np.where(dist < max_exact, dist, large).astype(np.int32)


def _attn_bias_tables(rel_bias):
    qi = np.arange(BLOCK)[:, None]
    kj = np.arange(2 * BLOCK)[None, :]
    rel = BLOCK + qi - kj
    tabs = []
    for g, (window, dilation) in enumerate(ATTN_GROUPS):
        band = (rel >= 0) & (rel <= window // dilation)
        bucket = _t5_bucket(np.clip(rel, 0, None) * dilation)
        tbl = rel_bias[:, g * HEADS_PER_GROUP:(g + 1) * HEADS_PER_GROUP]
        onehot = jnp.asarray(bucket[None] == np.arange(REL_BUCKETS)[:, None, None], F32)
        bias = jnp.einsum('rqk,rh->hqk', onehot, tbl.astype(F32),
                          precision=lax.Precision.HIGHEST)
        general = jnp.where(band[None], bias * LOG2E, NEG)
        masked = jnp.full((HEADS_PER_GROUP, BLOCK, BLOCK), NEG, F32)
        first = jnp.concatenate([masked, general[:, :, BLOCK:]], axis=2)
        very_first = jnp.concatenate([general[:, :, BLOCK:], masked], axis=2)
        tabs.append(jnp.stack([very_first, first, general]))
    return jnp.stack(tabs).reshape(N_GROUPS, 3, HEAD_PAIRS, 2 * BLOCK, 2 * BLOCK)


def _attn_group(q_ref, k_ref, v_ref, bias_ref, o_ref, part_ref, *, dilation, slot):
    nb = SEQ // dilation // BLOCK
    lane = lax.broadcasted_iota(jnp.int32, (BLOCK, LANES), 1)
    head0 = lane < HEAD_DIM
    keep0 = jnp.where(head0, 1.0, 0.0).astype(BF16)
    keep1 = jnp.where(head0, 0.0, 1.0).astype(BF16)
    ones_cols = jnp.ones((2 * BLOCK, LANES), BF16)
    contract_last = (((1,), (1,)), ((), ()))

    def token_rows(n):
        start = n // nb + (n % nb) * (BLOCK * dilation)
        if dilation == 1:
            start = pl.multiple_of(start, BLOCK)
        return pl.ds(start, BLOCK, stride=dilation)

    def key_rows(n):
        return pl.ds(pl.multiple_of(jnp.maximum(n - 1, 0) * BLOCK, BLOCK), 2 * BLOCK)

    def logits(n):
        qb = q_ref[pl.ds(pl.multiple_of(n * BLOCK, BLOCK), BLOCK), :]
        qq = jnp.concatenate([qb * keep0, qb * keep1], axis=0)
        tab = jnp.where(n == 0, 0, jnp.where(n % nb == 0, 1, 2))
        l = lax.dot_general(qq, k_ref[key_rows(n), :], contract_last,
                            preferred_element_type=F32) + bias_ref[tab]
        return l, jnp.max(l, axis=1, keepdims=True)

    def weighted(n, l, m_rows):
        v_aug = jnp.concatenate([v_ref[key_rows(n), :], ones_cols], axis=1)
        p = jnp.exp2(l - m_rows).astype(BF16)
        r = jnp.dot(p, v_aug, preferred_element_type=F32)
        return (jnp.where(head0, r[:BLOCK, :LANES], r[BLOCK:, :LANES]),
                jnp.where(head0, r[:BLOCK, LANES:], r[BLOCK:, LANES:]))

    def body(step, carry):
        ns = [step * ATTN_UNROLL + u for u in range(ATTN_UNROLL)]
        scores = [logits(n) for n in ns]
        for n, (l, m_rows) in zip(ns, scores):
            acc, den = weighted(n, l, m_rows)
            m_blk = jnp.where(head0, m_rows[:BLOCK], m_rows[BLOCK:])
            rows = token_rows(n)
            if slot is not None:
                part_ref[slot, 0, rows, :] = acc / den
                part_ref[slot, 1, rows, :] = m_blk + jnp.log2(den)
            else:
                others = [(part_ref[g, 0, rows, :], part_ref[g, 1, rows, :])
                          for g in range(N_GROUPS - 1)]
                m_all = functools.reduce(jnp.maximum, [m_blk] + [lse for _, lse in others])
                w_own = jnp.exp2(m_blk - m_all)
                num = w_own * acc
                den_all = w_own * den
                for out_g, lse in others:
                    w = jnp.exp2(lse - m_all)
                    num = num + w * out_g
                    den_all = den_all + w
                o_ref[rows, :] = num / den_all
        return carry

    lax.fori_loop(0, N_QBLOCKS // ATTN_UNROLL, body, 0)


def _attn_kernel(*refs):
    qkv_refs, (bias_ref, o_ref, part_ref) = refs[:3 * N_GROUPS], refs[3 * N_GROUPS:]
    for i, gi in enumerate(ATTN_ORDER):
        q_ref, k_ref, v_ref = qkv_refs[3 * gi:3 * gi + 3]
        _attn_group(q_ref, k_ref, v_ref, bias_ref.at[gi], o_ref, part_ref,
                    dilation=ATTN_GROUPS[gi][1], slot=i if i < N_GROUPS - 1 else None)


def _attention(qkv_groups, bias_tabs):
    in_specs, args = [], []
    for qkv in qkv_groups:
        for which in range(3):
            in_specs.append(pl.BlockSpec(
                (None, SEQ, LANES), lambda b, hp, which=which: (b, 0, which * HEAD_PAIRS + hp)))
            args.append(qkv)
    in_specs.append(pl.BlockSpec((N_GROUPS, 3, None, 2 * BLOCK, 2 * BLOCK),
                                 lambda b, hp: (0, 0, hp, 0, 0)))
    return pl.pallas_call(
        _attn_kernel,
        grid=(BATCH, HEAD_PAIRS),
        in_specs=in_specs,
        out_specs=pl.BlockSpec((None, SEQ, LANES), lambda b, hp: (b, 0, hp)),
        out_shape=jax.ShapeDtypeStruct((BATCH, SEQ, GROUP_WIDTH), F32),
        scratch_shapes=[pltpu.VMEM((N_GROUPS - 1, 2, SEQ, LANES), F32)],
        compiler_params=_cparams(("arbitrary", "arbitrary")),
        name="dilated_attn",
    )(*args, bias_tabs)


def _ssm_param_kernel(lre_ref, lim_ref, ldt_ref, are_ref, aim_ref, fre_ref, fim_ref):
    lam_re = lre_ref[...]
    lam_im = lim_ref[...]
    dt = jnp.exp(ldt_ref[...])
    mag = jnp.exp(lam_re * dt)
    ang = lam_im * dt
    a_re = mag * jnp.cos(ang)
    a_im = mag * jnp.sin(ang)
    den = lam_re * lam_re + lam_im * lam_im
    are_ref[...] = a_re
    aim_ref[...] = a_im
    fre_ref[...] = ((a_re - 1) * lam_re + a_im * lam_im) / den
    fim_ref[...] = (a_im * lam_re - (a_re - 1) * lam_im) / den


def _ssm_params(lam_re, lam_im, log_dt):
    n = DEPTH * SSM_GROUPS
    shp = jax.ShapeDtypeStruct((n, SSM_STATE), F32)
    return pl.pallas_call(
        _ssm_param_kernel, out_shape=(shp, shp, shp, shp), name="ssm_discretise",
    )(lam_re.reshape(n, SSM_STATE), lam_im.reshape(n, SSM_STATE), log_dt.reshape(n, 1))


def _ssm_kernel(u_ref, bre_ref, bim_ref, cre_ref, cim_ref, are_ref, aim_ref, d_ref,
                y_ref, sre_ref, sim_ref, xre_ref, xim_ref):
    c = pl.program_id(1)

    @pl.when(c == 0)
    def _():
        xre_ref[...] = jnp.zeros_like(xre_ref)
        xim_ref[...] = jnp.zeros_like(xim_ref)

    def plane_rows(t, b):
        half, pair = divmod(t, HALF_TILES)
        r0 = (half * BATCH + b) * SSM_PITCH
        return pair, slice(r0, r0 + SSM_TC)

    for b in range(BATCH):
        ub = u_ref[b].astype(BF16)
        bu_re = jnp.dot(ub, bre_ref[...], preferred_element_type=F32)
        bu_im = jnp.dot(ub, bim_ref[...], preferred_element_type=F32)
        for t in range(SLAB_TILES):
            pair, sl = plane_rows(t, b)
            sre_ref[pair, sl, :] = bu_re[:, t * LANES:(t + 1) * LANES]
            sim_ref[pair, sl, :] = bu_im[:, t * LANES:(t + 1) * LANES]

    def coeff(ref, pair):
        return jnp.concatenate(
            [jnp.broadcast_to(ref[half * HALF_TILES + pair:half * HALF_TILES + pair + 1, :],
                              (BATCH, LANES)) for half in range(2)], axis=0)

    a_re = [coeff(are_ref, p) for p in range(HALF_TILES)]
    a_im = [coeff(aim_ref, p) for p in range(HALF_TILES)]

    def step(i, carry):
        xs = list(carry)
        rows = pl.ds(i, SUBLANES, stride=SSM_PITCH)
        for p in range(HALF_TILES):
            xr, xi = xs[2 * p], xs[2 * p + 1]
            nr = a_re[p] * xr - a_im[p] * xi + sre_ref[p, rows, :]
            ni = a_re[p] * xi + a_im[p] * xr + sim_ref[p, rows, :]
            sre_ref[p, rows, :] = nr
            sim_ref[p, rows, :] = ni
            xs[2 * p], xs[2 * p + 1] = nr, ni
        return tuple(xs)

    init = []
    for p in range(HALF_TILES):
        init += [xre_ref[p], xim_ref[p]]
    fin = lax.fori_loop(0, SSM_TC, step, tuple(init), unroll=8)
    for p in range(HALF_TILES):
        xre_ref[p] = fin[2 * p]
        xim_ref[p] = fin[2 * p + 1]

    for b in range(BATCH):
        tiles = [plane_rows(t, b) for t in range(SLAB_TILES)]
        x_re = jnp.concatenate([sre_ref[pair, sl, :] for pair, sl in tiles], axis=1)
        x_im = jnp.concatenate([sim_ref[pair, sl, :] for pair, sl in tiles], axis=1)
        y = (_bdot(x_re, cre_ref[...]) + _bdot(x_im, cim_ref[...])
             + d_ref[...] * u_ref[b])
        y_ref[b] = jax.nn.gelu(y).astype(BF16)


def _ssm(u, bb_re, bb_im, cc_re, cc_im, a_re, a_im, d_skip, layer):
    slab = lambda s, c: (layer * SSM_SLABS + s, 0, 0)
    return pl.pallas_call(
        _ssm_kernel,
        grid=(SSM_SLABS, SEQ // SSM_TC),
        in_specs=[
            pl.BlockSpec((BATCH, SSM_TC, SLAB_CH), lambda s, c: (0, c, s)),
            pl.BlockSpec((None, SLAB_CH, SLAB_STATES), slab),
            pl.BlockSpec((None, SLAB_CH, SLAB_STATES), slab),
            pl.BlockSpec((None, SLAB_STATES, SLAB_CH), slab),
            pl.BlockSpec((None, SLAB_STATES, SLAB_CH), slab),
            pl.BlockSpec((None, SLAB_TILES, LANES), slab),
            pl.BlockSpec((None, SLAB_TILES, LANES), slab),
            pl.BlockSpec((None, 1, SLAB_CH), slab),
        ],
        out_specs=pl.BlockSpec((BATCH, SSM_TC, SLAB_CH), lambda s, c: (0, c, s)),
        out_shape=jax.ShapeDtypeStruct((BATCH, SEQ, SSM_WIDTH), BF16),
        scratch_shapes=[
            pltpu.VMEM((HALF_TILES, SUBLANES * SSM_PITCH, LANES), F32),
            pltpu.VMEM((HALF_TILES, SUBLANES * SSM_PITCH, LANES), F32),
            pltpu.VMEM((HALF_TILES, SUBLANES, LANES), F32),
            pltpu.VMEM((HALF_TILES, SUBLANES, LANES), F32),
        ],
        compiler_params=_cparams(("arbitrary", "arbitrary")),
        name="s5_scan",
    )(u, bb_re, bb_im, cc_re, cc_im, a_re, a_im, d_skip)


def _ssm_matrices(f_re, f_im, b_re, b_im, c_re, c_im):
    n = DEPTH * SSM_GROUPS
    f_re, f_im = f_re[..., None], f_im[..., None]
    b_re = b_re.reshape(n, SSM_STATE, SSM_GROUP)
    b_im = b_im.reshape(n, SSM_STATE, SSM_GROUP)
    bb_re = f_re * b_re - f_im * b_im
    bb_im = f_re * b_im + f_im * b_re
    gps = SSM_GROUPS // SSM_SLABS
    slabs = DEPTH * SSM_SLABS
    eye = jnp.eye(gps, dtype=F32)

    def in_map(bb):
        bb = bb.reshape(slabs, gps, SSM_STATE, SSM_GROUP)
        return jnp.einsum('sgpi,gh->sgihp', bb, eye).reshape(slabs, SLAB_CH, SLAB_STATES)

    def out_map(cc):
        cc = cc.reshape(slabs, gps, SSM_GROUP, SSM_STATE)
        return jnp.einsum('sgcp,gh->sgphc', cc, eye).reshape(slabs, SLAB_STATES, SLAB_CH)

    return (in_map(bb_re).astype(BF16), in_map(bb_im).astype(BF16),
            out_map(c_re).astype(BF16), out_map(-c_im).astype(BF16))


def _mix_kernel(x_ref, mod_ref, o_ref_in, ys_ref, ga_ref, gs_ref, wap_hbm, wglu_hbm, wout_hbm,
                out_ref, wap_ref, wap_stage, wap_sem, wglu_ref, wglu_stage, wglu_sem,
                wout_ref, wout_stage, wout_sem, *, layer):
    @pl.when(_first_grid_step())
    def _():
        _load_weight(wap_hbm, layer, wap_ref, wap_stage, wap_sem)
        _load_weight(wglu_hbm, layer, wglu_ref, wglu_stage, wglu_sem)
        _load_weight(wout_hbm, layer, wout_ref, wout_stage, wout_sem)

    y_attn = _bdot(o_ref_in[...], wap_ref[...])
    gl = jnp.dot(ys_ref[...], wglu_ref[...], preferred_element_type=F32)
    y_ssm = gl[:, :D_MODEL] * _sigmoid(gl[:, D_MODEL:])
    mixed = (_sigmoid(ga_ref[...].astype(F32)) * y_attn
             + _sigmoid(gs_ref[...].astype(F32)) * y_ssm)
    out_ref[...] = x_ref[...] + mod_ref[5:6, :] * _bdot(mixed, wout_ref[...])


def _mix(x, mod, attn_o, y_ssm, gates, w_ap, w_glu, w_out, layer):
    rows = lambda b, i: (b, i, 0)
    return pl.pallas_call(
        functools.partial(_mix_kernel, layer=layer),
        grid=(BATCH, SEQ // MIX_TM),
        in_specs=[
            pl.BlockSpec((None, MIX_TM, D_MODEL), rows),
            pl.BlockSpec((None, None, N_MOD, D_MODEL), lambda b, i: (layer, b, 0, 0)),
            pl.BlockSpec((None, MIX_TM, GROUP_WIDTH), rows),
            pl.BlockSpec((None, MIX_TM, SSM_WIDTH), rows),
            pl.BlockSpec((None, MIX_TM, D_MODEL), rows),
            pl.BlockSpec((None, MIX_TM, D_MODEL), lambda b, i: (b, i, 1)),
            pl.BlockSpec(memory_space=pl.ANY),
            pl.BlockSpec(memory_space=pl.ANY),
            pl.BlockSpec(memory_space=pl.ANY),
        ],
        out_specs=pl.BlockSpec((None, MIX_TM, D_MODEL), rows),
        out_shape=jax.ShapeDtypeStruct((BATCH, SEQ, D_MODEL), F32),
        scratch_shapes=(_weight_scratch(GROUP_WIDTH, D_MODEL, WEIGHT_CHUNKS)
                        + _weight_scratch(SSM_WIDTH, 2 * D_MODEL, WEIGHT_CHUNKS)
                        + _weight_scratch(D_MODEL, D_MODEL, WEIGHT_CHUNKS)),
        compiler_params=_cparams(("arbitrary", "arbitrary")),
        name="mix_out",
    )(x, mod, attn_o, y_ssm, gates, gates, w_ap, w_glu, w_out)


def kernel(x, c, w_ada, b_ada, norm_ffn1, w_ffn1_in, w_ffn1_out, norm_mix, w_in, rel_bias, lam_re, lam_im, log_dt, b_re, b_im, c_re, c_im, d_skip, w_glu, w_attn_proj, w_out, norm_ffn2, w_ffn2_in, w_ffn2_out, final_norm):
    c_pad = jnp.zeros((SUBLANES, D_MODEL), F32).at[:BATCH].set(c)
    mod = _ada(c_pad, w_ada, b_ada).reshape(DEPTH, SUBLANES, N_MOD, D_MODEL)
    bias_tabs = _attn_bias_tables(rel_bias)
    a_re, a_im, f_re, f_im = _ssm_params(lam_re, lam_im, log_dt)
    bb_re, bb_im, cc_re, cc_im = _ssm_matrices(f_re, f_im, b_re, b_im, c_re, c_im)
    a_re = a_re.reshape(DEPTH * SSM_SLABS, SLAB_TILES, LANES)
    a_im = a_im.reshape(DEPTH * SSM_SLABS, SLAB_TILES, LANES)
    d_skip = d_skip.reshape(DEPTH * SSM_SLABS, 1, SLAB_CH)
    for l in range(DEPTH):
        x = _ffn(x, mod, norm_ffn1, w_ffn1_in, w_ffn1_out, l, 0)

        *qkv_groups, u, gates = _inproj(x, mod, norm_mix, w_in, l)
        attn_o = _attention([qkv.reshape(BATCH, SEQ, QKV_WIDTH) for qkv in qkv_groups], bias_tabs)
        y_ssm = _ssm(u, bb_re, bb_im, cc_re, cc_im, a_re, a_im, d_skip, l)
        x = _mix(x, mod, attn_o, y_ssm, gates, w_attn_proj, w_glu, w_out, l)

        x = _ffn(x, mod, norm_ffn2, w_ffn2_in, w_ffn2_out, l, 6,
                 final_norm=final_norm if l == DEPTH - 1 else None)
    return x
```

```python
import functools
import math

import jax
import jax.numpy as jnp
import numpy as np
from jax import lax
from jax.experimental import pallas as pl
from jax.experimental.pallas import tpu as pltpu

D_MODEL = 1024
BATCH = 4
SEQ = 4096
DEPTH = 2
HEAD_DIM = 64
HEADS_PER_GROUP = 8
ATTN_GROUPS = ((128, 1), (512, 4), (2048, 16))
N_GROUPS = len(ATTN_GROUPS)
GROUP_WIDTH = HEADS_PER_GROUP * HEAD_DIM
BLOCK = 128
REL_BUCKETS = 32
REL_MAX_DIST = 2048
NEG = -1e30
SSM_WIDTH = 512
SSM_GROUP = 16
SSM_GROUPS = 32
SSM_STATE = 64
D_FF = 2816
QKV_COLS = 3 * N_GROUPS * GROUP_WIDTH
IN_COLS = QKV_COLS + SSM_WIDTH + 2 * D_MODEL
N_MOD = 9
EPS = 1e-6

LANES = 128
SUBLANES = 8
MXU_DIM = 256
FREE_STRIDE = 4
VMEM_LIMIT = 56 * 1024 * 1024

ADA_TN = 1152
FFN_TM = 1024
FFN_TF = MXU_DIM
PROJ_TM = 512
QKV_WIDTH = 3 * GROUP_WIDTH
MIX_TM = 512
WEIGHT_CHUNKS = 8
HEAD_PAIRS = GROUP_WIDTH // LANES
ATTN_UNROLL = 16
ATTN_ORDER = (2, 1, 0)
LOG2E = math.log2(math.e)
QSCALE = HEAD_DIM ** -0.5 * LOG2E
N_QBLOCKS = SEQ // BLOCK
SSM_TC = 512
SSM_PITCH = SSM_TC + SUBLANES
SLAB_CH = MXU_DIM
SSM_SLABS = SSM_WIDTH // SLAB_CH
SLAB_STATES = (SSM_GROUPS // SSM_SLABS) * SSM_STATE
SLAB_TILES = SLAB_STATES // LANES
HALF_TILES = SLAB_TILES * BATCH // SUBLANES

BF16 = jnp.bfloat16
F32 = jnp.float32


def _cparams(sem):
    return pltpu.CompilerParams(dimension_semantics=sem, vmem_limit_bytes=VMEM_LIMIT)


def _first_grid_step():
    return (pl.program_id(0) == 0) & (pl.program_id(1) == 0)


def _weight_scratch(k, n, chunks):
    return [pltpu.VMEM((k, n), BF16), pltpu.VMEM((2, k // chunks, n), F32),
            pltpu.SemaphoreType.DMA((2,))]


def _load_weight(w_hbm, layer, w_ref, stage_ref, sem):
    rows = stage_ref.shape[1]
    n_chunks = w_ref.shape[0] // rows

    def chunk(i, slot):
        return pltpu.make_async_copy(w_hbm.at[layer, pl.ds(i * rows, rows), :],
                                     stage_ref.at[slot], sem.at[slot])

    chunk(0, 0).start()

    def body(i, carry):
        slot = i % 2

        @pl.when(i + 1 < n_chunks)
        def _():
            chunk(i + 1, 1 - slot).start()

        chunk(i, slot).wait()
        w_ref[pl.ds(pl.multiple_of(i * rows, rows), rows), :] = stage_ref[slot].astype(BF16)
        return carry

    lax.fori_loop(0, n_chunks, body, 0)


def _bdot(a, b):
    return jnp.dot(a.astype(BF16), b.astype(BF16), preferred_element_type=F32)


def _sigmoid(x):
    return 0.5 * jnp.tanh(0.5 * x) + 0.5


def _norm_mod(x, g, shift, scale):
    ms = jnp.mean(x * x, axis=-1, keepdims=True)
    y = x * lax.rsqrt(ms + EPS) * g
    return y * (1.0 + scale) + shift


def _ada_kernel(c_ref, w_ref, b_ref, o_ref):
    c = c_ref[...]
    ca = c * _sigmoid(c)
    o_ref[...] = _bdot(ca, w_ref[...]) + b_ref[...]


def _ada(c_pad, w_ada, b_ada):
    n = N_MOD * D_MODEL
    return pl.pallas_call(
        _ada_kernel,
        grid=(DEPTH, n // ADA_TN),
        in_specs=[
            pl.BlockSpec((SUBLANES, D_MODEL), lambda l, j: (0, 0)),
            pl.BlockSpec((None, D_MODEL, ADA_TN), lambda l, j: (l, 0, j)),
            pl.BlockSpec((None, 1, ADA_TN), lambda l, j: (l, 0, j)),
        ],
        out_specs=pl.BlockSpec((None, SUBLANES, ADA_TN), lambda l, j: (l, 0, j)),
        out_shape=jax.ShapeDtypeStruct((DEPTH, SUBLANES, n), F32),
        compiler_params=_cparams(("arbitrary", "arbitrary")),
        name="ada_mod",
    )(c_pad, w_ada, b_ada.reshape(DEPTH, 1, n))


def _ffn_kernel(x_ref, mod_ref, g_ref, win_hbm, wout_hbm, *rest, layer, row0, final):
    if final:
        fn_ref, rest = rest[0], rest[1:]
    o_ref, act_ref, win_ref, wout_ref, a_stage, b_stage, o_stage, sem = rest
    n_chunks = D_FF // FFN_TF

    def chunk_copies(j, slot):
        lo, hi = j * FFN_TF, (j + 1) * FFN_TF
        return (pltpu.make_async_copy(win_hbm.at[layer, :, lo:hi], a_stage.at[slot], sem.at[0, slot]),
                pltpu.make_async_copy(win_hbm.at[layer, :, D_FF + lo:D_FF + hi], b_stage.at[slot],
                                      sem.at[1, slot]),
                pltpu.make_async_copy(wout_hbm.at[layer, lo:hi, :], o_stage.at[slot], sem.at[2, slot]))

    def tile(load_weights):
        x = x_ref[...]
        h = _norm_mod(x, g_ref[...], mod_ref[row0:row0 + 1, :], mod_ref[row0 + 1:row0 + 2, :])
        hb = h.astype(BF16)
        if load_weights:
            for copy in chunk_copies(0, 0):
                copy.start()
        for j in range(n_chunks):
            lo, hi = j * FFN_TF, (j + 1) * FFN_TF
            if load_weights:
                if j + 1 < n_chunks:
                    for copy in chunk_copies(j + 1, (j + 1) % 2):
                        copy.start()
                for copy in chunk_copies(j, j % 2):
                    copy.wait()
                win_ref[:, lo:hi] = a_stage[j % 2].astype(BF16)
                win_ref[:, D_FF + lo:D_FF + hi] = b_stage[j % 2].astype(BF16)
                wout_ref[lo:hi, :] = o_stage[j % 2].astype(BF16)
            a = jnp.dot(hb, win_ref[:, lo:hi], preferred_element_type=F32)
            b = jnp.dot(hb, win_ref[:, D_FF + lo:D_FF + hi], preferred_element_type=F32)
            act_ref[:, lo:hi] = (a * _sigmoid(a) * b).astype(BF16)
        y = jnp.dot(act_ref[...], wout_ref[...], preferred_element_type=F32)
        out = x + (0.5 * mod_ref[row0 + 2:row0 + 3, :]) * y
        if final:
            ms = jnp.mean(out * out, axis=-1, keepdims=True)
            out = out * lax.rsqrt(ms + EPS) * fn_ref[...]
        o_ref[...] = out

    first = _first_grid_step()
    pl.when(first)(functools.partial(tile, True))
    pl.when(jnp.logical_not(first))(functools.partial(tile, False))


def _ffn(x, mod, norm_g, w_in, w_out, layer, row0, final_norm=None):
    final = final_norm is not None
    in_specs = [
        pl.BlockSpec((None, FFN_TM, D_MODEL), lambda b, i: (b, i, 0)),
        pl.BlockSpec((None, None, N_MOD, D_MODEL), lambda b, i: (layer, b, 0, 0)),
        pl.BlockSpec((None, 1, D_MODEL), lambda b, i: (layer, 0, 0)),
        pl.BlockSpec(memory_space=pl.ANY),
        pl.BlockSpec(memory_space=pl.ANY),
    ]
    args = [x, mod, norm_g.reshape(DEPTH, 1, D_MODEL), w_in, w_out]
    if final:
        in_specs.append(pl.BlockSpec((1, D_MODEL), lambda b, i: (0, 0)))
        args.append(final_norm.reshape(1, D_MODEL))
    return pl.pallas_call(
        functools.partial(_ffn_kernel, layer=layer, row0=row0, final=final),
        grid=(BATCH, SEQ // FFN_TM),
        in_specs=in_specs,
        out_specs=pl.BlockSpec((None, FFN_TM, D_MODEL), lambda b, i: (b, i, 0)),
        out_shape=jax.ShapeDtypeStruct((BATCH, SEQ, D_MODEL), F32),
        scratch_shapes=[
            pltpu.VMEM((FFN_TM, D_FF), BF16),
            pltpu.VMEM((D_MODEL, 2 * D_FF), BF16),
            pltpu.VMEM((D_FF, D_MODEL), BF16),
            pltpu.VMEM((2, D_MODEL, FFN_TF), F32),
            pltpu.VMEM((2, D_MODEL, FFN_TF), F32),
            pltpu.VMEM((2, FFN_TF, D_MODEL), F32),
            pltpu.SemaphoreType.DMA((3, 2)),
        ],
        compiler_params=_cparams(("arbitrary", "arbitrary")),
        name="ffn_final" if final else "ffn",
    )(*args)


def _rows_by_residue(hf, rows_ref, half_ref):
    tiles = D_MODEL // LANES
    out = {1: hf.astype(BF16)}
    if all(d == 1 for _, d in ATTN_GROUPS):
        return out
    for t in range(tiles):
        rows_ref[t] = hf[:, t * LANES:(t + 1) * LANES]

    def gather(ref, pieces):
        return jnp.concatenate(
            [jnp.concatenate([ref[t, pl.ds(start, count, stride=stride), :]
                              for start, count, stride in pieces], axis=0)
             for t in range(tiles)], axis=1)

    per_inner = PROJ_TM // FREE_STRIDE
    inner = gather(rows_ref, [(b, per_inner, FREE_STRIDE) for b in range(FREE_STRIDE)])
    for _, dilation in ATTN_GROUPS:
        if dilation == 1 or dilation in out:
            continue
        if dilation == FREE_STRIDE:
            out[dilation] = inner.astype(BF16)
            continue
        outer = dilation // FREE_STRIDE
        assert dilation % FREE_STRIDE == 0 and outer <= FREE_STRIDE
        for t in range(tiles):
            half_ref[t] = inner[:, t * LANES:(t + 1) * LANES]
        pieces = [(b * per_inner + a, PROJ_TM // dilation, outer)
                  for a, b in (divmod(r, FREE_STRIDE) for r in range(dilation))]
        out[dilation] = gather(half_ref, pieces).astype(BF16)
    return out


def _inproj_kernel(x_ref, mod_ref, g_ref, w_hbm, qkv0_ref, qkv1_ref, qkv2_ref, u_ref, gate_ref,
                   w_ref, w_stage, w_sem, rows_ref, half_ref, *, layer):
    qkv_refs = (qkv0_ref, qkv1_ref, qkv2_ref)
    n_segs = IN_COLS // GROUP_WIDTH
    order = sorted(range(n_segs),
                   key=lambda seg: ATTN_GROUPS[seg % N_GROUPS][1] if seg < 3 * N_GROUPS else 1)

    def seg_copy(k, slot):
        seg = order[k]
        return pltpu.make_async_copy(
            w_hbm.at[layer, :, seg * GROUP_WIDTH:(seg + 1) * GROUP_WIDTH], w_stage.at[slot],
            w_sem.at[slot])

    def tile(load_weights):
        hf = _norm_mod(x_ref[...], g_ref[...], mod_ref[3:4, :], mod_ref[4:5, :])
        h = _rows_by_residue(hf, rows_ref, half_ref)
        if load_weights:
            seg_copy(0, 0).start()
        for k, seg in enumerate(order):
            cols = slice(seg * GROUP_WIDTH, (seg + 1) * GROUP_WIDTH)
            if load_weights:
                if k + 1 < n_segs:
                    seg_copy(k + 1, (k + 1) % 2).start()
                seg_copy(k, k % 2).wait()
                w_ref[:, cols] = w_stage[k % 2].astype(BF16)
            if seg < 3 * N_GROUPS:
                which, g = divmod(seg, N_GROUPS)
                dilation = ATTN_GROUPS[g][1]
                ys = jnp.dot(h[dilation], w_ref[:, cols], preferred_element_type=F32)
                if which == 0:
                    ys = ys * QSCALE
                n = PROJ_TM // dilation
                for r in range(dilation):
                    qkv_refs[g][r, :, which * GROUP_WIDTH:(which + 1) * GROUP_WIDTH] = (
                        ys[r * n:(r + 1) * n, :].astype(BF16))
            else:
                ys = jnp.dot(h[1], w_ref[:, cols], preferred_element_type=F32)
                if seg == 3 * N_GROUPS:
                    u_ref[...] = ys
                else:
                    g0 = (seg - 3 * N_GROUPS - 1) * GROUP_WIDTH
                    gate_ref[:, g0:g0 + GROUP_WIDTH] = ys.astype(BF16)

    first = _first_grid_step()
    pl.when(first)(functools.partial(tile, True))
    pl.when(jnp.logical_not(first))(functools.partial(tile, False))


def _inproj(x, mod, norm_g, w_in, layer):
    rows = lambda b, i: (b, i, 0)
    qkv_specs, qkv_shapes = [], []
    for _, dilation in ATTN_GROUPS:
        qkv_specs.append(pl.BlockSpec((None, dilation, PROJ_TM // dilation, QKV_WIDTH),
                                      lambda b, i: (b, 0, i, 0)))
        qkv_shapes.append(jax.ShapeDtypeStruct((BATCH, dilation, SEQ // dilation, QKV_WIDTH), BF16))
    tiles = D_MODEL // LANES
    return pl.pallas_call(
        functools.partial(_inproj_kernel, layer=layer),
        grid=(BATCH, SEQ // PROJ_TM),
        in_specs=[
            pl.BlockSpec((None, PROJ_TM, D_MODEL), rows),
            pl.BlockSpec((None, None, N_MOD, D_MODEL), lambda b, i: (layer, b, 0, 0)),
            pl.BlockSpec((None, 1, D_MODEL), lambda b, i: (layer, 0, 0)),
            pl.BlockSpec(memory_space=pl.ANY),
        ],
        out_specs=(*qkv_specs,
                   pl.BlockSpec((None, PROJ_TM, SSM_WIDTH), rows),
                   pl.BlockSpec((None, PROJ_TM, 2 * D_MODEL), rows)),
        out_shape=(*qkv_shapes,
                   jax.ShapeDtypeStruct((BATCH, SEQ, SSM_WIDTH), F32),
                   jax.ShapeDtypeStruct((BATCH, SEQ, 2 * D_MODEL), BF16)),
        scratch_shapes=[
            pltpu.VMEM((D_MODEL, IN_COLS), BF16),
            pltpu.VMEM((2, D_MODEL, GROUP_WIDTH), F32),
            pltpu.SemaphoreType.DMA((2,)),
            pltpu.VMEM((tiles, PROJ_TM, LANES), F32),
            pltpu.VMEM((tiles, PROJ_TM, LANES), F32),
        ],
        compiler_params=_cparams(("arbitrary", "arbitrary")),
        name="inproj",
    )(x, mod, norm_g.reshape(DEPTH, 1, D_MODEL), w_in)


def _t5_bucket(dist):
    max_exact = REL_BUCKETS // 2
    d = np.maximum(dist, max_exact).astype(np.float32)
    large = max_exact + (np.log(d / max_exact) / np.log(REL_MAX_DIST / max_exact)
                         * (REL_BUCKETS - max_exact)).astype(np.int32)
    large = np.minimum(large, REL_BUCKETS - 1)
    return np.where(dist < max_exact, dist, large).astype(np.int32)


def _attn_bias_tables(rel_bias):
    qi = np.arange(BLOCK)[:, None]
    kj = np.arange(2 * BLOCK)[None, :]
    rel = BLOCK + qi - kj
    tabs = []
    for g, (window, dilation) in enumerate(ATTN_GROUPS):
        band = (rel >= 0) & (rel <= window // dilation)
        bucket = _t5_bucket(np.clip(rel, 0, None) * dilation)
        tbl = rel_bias[:, g * HEADS_PER_GROUP:(g + 1) * HEADS_PER_GROUP]
        onehot = jnp.asarray(bucket[None] == np.arange(REL_BUCKETS)[:, None, None], F32)
        bias = jnp.einsum('rqk,rh->hqk', onehot, tbl.astype(F32),
                          precision=lax.Precision.HIGHEST)
        general = jnp.where(band[None], bias * LOG2E, NEG)
        masked = jnp.full((HEADS_PER_GROUP, BLOCK, BLOCK), NEG, F32)
        first = jnp.concatenate([masked, general[:, :, BLOCK:]], axis=2)
        very_first = jnp.concatenate([general[:, :, BLOCK:], masked], axis=2)
        tabs.append(jnp.stack([very_first, first, general]))
    return jnp.stack(tabs).reshape(N_GROUPS, 3, HEAD_PAIRS, 2 * BLOCK, 2 * BLOCK)


def _attn_group(q_ref, k_ref, v_ref, bias_ref, o_ref, part_ref, *, dilation, slot):
    nb = SEQ // dilation // BLOCK
    lane = lax.broadcasted_iota(jnp.int32, (BLOCK, LANES), 1)
    head0 = lane < HEAD_DIM
    keep0 = jnp.where(head0, 1.0, 0.0).astype(BF16)
    keep1 = jnp.where(head0, 0.0, 1.0).astype(BF16)
    ones_cols = jnp.ones((2 * BLOCK, LANES), BF16)
    contract_last = (((1,), (1,)), ((), ()))

    def token_rows(n):
        start = n // nb + (n % nb) * (BLOCK * dilation)
        if dilation == 1:
            start = pl.multiple_of(start, BLOCK)
        return pl.ds(start, BLOCK, stride=dilation)

    def key_rows(n):
        return pl.ds(pl.multiple_of(jnp.maximum(n - 1, 0) * BLOCK, BLOCK), 2 * BLOCK)

    def logits(n):
        qb = q_ref[pl.ds(pl.multiple_of(n * BLOCK, BLOCK), BLOCK), :]
        qq = jnp.concatenate([qb * keep0, qb * keep1], axis=0)
        tab = jnp.where(n == 0, 0, jnp.where(n % nb == 0, 1, 2))
        l = lax.dot_general(qq, k_ref[key_rows(n), :], contract_last,
                            preferred_element_type=F32) + bias_ref[tab]
        return l, jnp.max(l, axis=1, keepdims=True)

    def weighted(n, l, m_rows):
        v_aug = jnp.concatenate([v_ref[key_rows(n), :], ones_cols], axis=1)
        p = jnp.exp2(l - m_rows).astype(BF16)
        r = jnp.dot(p, v_aug, preferred_element_type=F32)
        return (jnp.where(head0, r[:BLOCK, :LANES], r[BLOCK:, :LANES]),
                jnp.where(head0, r[:BLOCK, LANES:], r[BLOCK:, LANES:]))

    def body(step, carry):
        ns = [step * ATTN_UNROLL + u for u in range(ATTN_UNROLL)]
        scores = [logits(n) for n in ns]
        for n, (l, m_rows) in zip(ns, scores):
            acc, den = weighted(n, l, m_rows)
            m_blk = jnp.where(head0, m_rows[:BLOCK], m_rows[BLOCK:])
            rows = token_rows(n)
            if slot is not None:
                part_ref[slot, 0, rows, :] = acc / den
                part_ref[slot, 1, rows, :] = m_blk + jnp.log2(den)
            else:
                others = [(part_ref[g, 0, rows, :], part_ref[g, 1, rows, :])
                          for g in range(N_GROUPS - 1)]
                m_all = functools.reduce(jnp.maximum, [m_blk] + [lse for _, lse in others])
                w_own = jnp.exp2(m_blk - m_all)
                num = w_own * acc
                den_all = w_own * den
                for out_g, lse in others:
                    w = jnp.exp2(lse - m_all)
                    num = num + w * out_g
                    den_all = den_all + w
                o_ref[rows, :] = num / den_all
        return carry

    lax.fori_loop(0, N_QBLOCKS // ATTN_UNROLL, body, 0)


def _attn_kernel(*refs):
    qkv_refs, (bias_ref, o_ref, part_ref) = refs[:3 * N_GROUPS], refs[3 * N_GROUPS:]
    for i, gi in enumerate(ATTN_ORDER):
        q_ref, k_ref, v_ref = qkv_refs[3 * gi:3 * gi + 3]
        _attn_group(q_ref, k_ref, v_ref, bias_ref.at[gi], o_ref, part_ref,
                    dilation=ATTN_GROUPS[gi][1], slot=i if i < N_GROUPS - 1 else None)


def _attention(qkv_groups, bias_tabs):
    in_specs, args = [], []
    for qkv in qkv_groups:
        for which in range(3):
            in_specs.append(pl.BlockSpec(
                (None, SEQ, LANES), lambda b, hp, which=which: (b, 0, which * HEAD_PAIRS + hp)))
            args.append(qkv)
    in_specs.append(pl.BlockSpec((N_GROUPS, 3, None, 2 * BLOCK, 2 * BLOCK),
                                 lambda b, hp: (0, 0, hp, 0, 0)))
    return pl.pallas_call(
        _attn_kernel,
        grid=(BATCH, HEAD_PAIRS),
        in_specs=in_specs,
        out_specs=pl.BlockSpec((None, SEQ, LANES), lambda b, hp: (b, 0, hp)),
        out_shape=jax.ShapeDtypeStruct((BATCH, SEQ, GROUP_WIDTH), F32),
        scratch_shapes=[pltpu.VMEM((N_GROUPS - 1, 2, SEQ, LANES), F32)],
        compiler_params=_cparams(("arbitrary", "arbitrary")),
        name="dilated_attn",
    )(*args, bias_tabs)


def _ssm_param_kernel(lre_ref, lim_ref, ldt_ref, are_ref, aim_ref, fre_ref, fim_ref):
    lam_re = lre_ref[...]
    lam_im = lim_ref[...]
    dt = jnp.exp(ldt_ref[...])
    mag = jnp.exp(lam_re * dt)
    ang = lam_im * dt
    a_re = mag * jnp.cos(ang)
    a_im = mag * jnp.sin(ang)
    den = lam_re * lam_re + lam_im * lam_im
    are_ref[...] = a_re
    aim_ref[...] = a_im
    fre_ref[...] = ((a_re - 1) * lam_re + a_im * lam_im) / den
    fim_ref[...] = (a_im * lam_re - (a_re - 1) * lam_im) / den


def _ssm_params(lam_re, lam_im, log_dt):
    n = DEPTH * SSM_GROUPS
    shp = jax.ShapeDtypeStruct((n, SSM_STATE), F32)
    return pl.pallas_call(
        _ssm_param_kernel, out_shape=(shp, shp, shp, shp), name="ssm_discretise",
    )(lam_re.reshape(n, SSM_STATE), lam_im.reshape(n, SSM_STATE), log_dt.reshape(n, 1))


def _ssm_kernel(u_ref, bre_ref, bim_ref, cre_ref, cim_ref, are_ref, aim_ref, d_ref,
                y_ref, sre_ref, sim_ref, xre_ref, xim_ref):
    c = pl.program_id(1)

    @pl.when(c == 0)
    def _():
        xre_ref[...] = jnp.zeros_like(xre_ref)
        xim_ref[...] = jnp.zeros_like(xim_ref)

    def plane_rows(t, b):
        half, pair = divmod(t, HALF_TILES)
        r0 = (half * BATCH + b) * SSM_PITCH
        return pair, slice(r0, r0 + SSM_TC)

    for b in range(BATCH):
        ub = u_ref[b].astype(BF16)
        bu_re = jnp.dot(ub, bre_ref[...], preferred_element_type=F32)
        bu_im = jnp.dot(ub, bim_ref[...], preferred_element_type=F32)
        for t in range(SLAB_TILES):
            pair, sl = plane_rows(t, b)
            sre_ref[pair, sl, :] = bu_re[:, t * LANES:(t + 1) * LANES]
            sim_ref[pair, sl, :] = bu_im[:, t * LANES:(t + 1) * LANES]

    def coeff(ref, pair):
        return jnp.concatenate(
            [jnp.broadcast_to(ref[half * HALF_TILES + pair:half * HALF_TILES + pair + 1, :],
                              (BATCH, LANES)) for half in range(2)], axis=0)

    a_re = [coeff(are_ref, p) for p in range(HALF_TILES)]
    a_im = [coeff(aim_ref, p) for p in range(HALF_TILES)]

    def step(i, carry):
        xs = list(carry)
        rows = pl.ds(i, SUBLANES, stride=SSM_PITCH)
        for p in range(HALF_TILES):
            xr, xi = xs[2 * p], xs[2 * p + 1]
            nr = a_re[p] * xr - a_im[p] * xi + sre_ref[p, rows, :]
            ni = a_re[p] * xi + a_im[p] * xr + sim_ref[p, rows, :]
            sre_ref[p, rows, :] = nr
            sim_ref[p, rows, :] = ni
            xs[2 * p], xs[2 * p + 1] = nr, ni
        return tuple(xs)

    init = []
    for p in range(HALF_TILES):
        init += [xre_ref[p], xim_ref[p]]
    fin = lax.fori_loop(0, SSM_TC, step, tuple(init), unroll=8)
    for p in range(HALF_TILES):
        xre_ref[p] = fin[2 * p]
        xim_ref[p] = fin[2 * p + 1]

    for b in range(BATCH):
        tiles = [plane_rows(t, b) for t in range(SLAB_TILES)]
        x_re = jnp.concatenate([sre_ref[pair, sl, :] for pair, sl in tiles], axis=1)
        x_im = jnp.concatenate([sim_ref[pair, sl, :] for pair, sl in tiles], axis=1)
        y = (_bdot(x_re, cre_ref[...]) + _bdot(x_im, cim_ref[...])
             + d_ref[...] * u_ref[b])
        y_ref[b] = jax.nn.gelu(y).astype(BF16)


def _ssm(u, bb_re, bb_im, cc_re, cc_im, a_re, a_im, d_skip, layer):
    slab = lambda s, c: (layer * SSM_SLABS + s, 0, 0)
    return pl.pallas_call(
        _ssm_kernel,
        grid=(SSM_SLABS, SEQ // SSM_TC),
        in_specs=[
            pl.BlockSpec((BATCH, SSM_TC, SLAB_CH), lambda s, c: (0, c, s)),
            pl.BlockSpec((None, SLAB_CH, SLAB_STATES), slab),
            pl.BlockSpec((None, SLAB_CH, SLAB_STATES), slab),
            pl.BlockSpec((None, SLAB_STATES, SLAB_CH), slab),
            pl.BlockSpec((None, SLAB_STATES, SLAB_CH), slab),
            pl.BlockSpec((None, SLAB_TILES, LANES), slab),
            pl.BlockSpec((None, SLAB_TILES, LANES), slab),
            pl.BlockSpec((None, 1, SLAB_CH), slab),
        ],
        out_specs=pl.BlockSpec((BATCH, SSM_TC, SLAB_CH), lambda s, c: (0, c, s)),
        out_shape=jax.ShapeDtypeStruct((BATCH, SEQ, SSM_WIDTH), BF16),
        scratch_shapes=[
            pltpu.VMEM((HALF_TILES, SUBLANES * SSM_PITCH, LANES), F32),
            pltpu.VMEM((HALF_TILES, SUBLANES * SSM_PITCH, LANES), F32),
            pltpu.VMEM((HALF_TILES, SUBLANES, LANES), F32),
            pltpu.VMEM((HALF_TILES, SUBLANES, LANES), F32),
        ],
        compiler_params=_cparams(("arbitrary", "arbitrary")),
        name="s5_scan",
    )(u, bb_re, bb_im, cc_re, cc_im, a_re, a_im, d_skip)


def _ssm_matrices(f_re, f_im, b_re, b_im, c_re, c_im):
    n = DEPTH * SSM_GROUPS
    f_re, f_im = f_re[..., None], f_im[..., None]
    b_re = b_re.reshape(n, SSM_STATE, SSM_GROUP)
    b_im = b_im.reshape(n, SSM_STATE, SSM_GROUP)
    bb_re = f_re * b_re - f_im * b_im
    bb_im = f_re * b_im + f_im * b_re
    gps = SSM_GROUPS // SSM_SLABS
    slabs = DEPTH * SSM_SLABS
    eye = jnp.eye(gps, dtype=F32)

    def in_map(bb):
        bb = bb.reshape(slabs, gps, SSM_STATE, SSM_GROUP)
        return jnp.einsum('sgpi,gh->sgihp', bb, eye).reshape(slabs, SLAB_CH, SLAB_STATES)

    def out_map(cc):
        cc = cc.reshape(slabs, gps, SSM_GROUP, SSM_STATE)
        return jnp.einsum('sgcp,gh->sgphc', cc, eye).reshape(slabs, SLAB_STATES, SLAB_CH)

    return (in_map(bb_re).astype(BF16), in_map(bb_im).astype(BF16),
            out_map(c_re).astype(BF16), out_map(-c_im).astype(BF16))


def _mix_kernel(x_ref, mod_ref, o_ref_in, ys_ref, ga_ref, gs_ref, wap_hbm, wglu_hbm, wout_hbm,
                out_ref, wap_ref, wap_stage, wap_sem, wglu_ref, wglu_stage, wglu_sem,
                wout_ref, wout_stage, wout_sem, *, layer):
    @pl.when(_first_grid_step())
    def _():
        _load_weight(wap_hbm, layer, wap_ref, wap_stage, wap_sem)
        _load_weight(wglu_hbm, layer, wglu_ref, wglu_stage, wglu_sem)
        _load_weight(wout_hbm, layer, wout_ref, wout_stage, wout_sem)

    y_attn = _bdot(o_ref_in[...], wap_ref[...])
    gl = jnp.dot(ys_ref[...], wglu_ref[...], preferred_element_type=F32)
    y_ssm = gl[:, :D_MODEL] * _sigmoid(gl[:, D_MODEL:])
    mixed = (_sigmoid(ga_ref[...].astype(F32)) * y_attn
             + _sigmoid(gs_ref[...].astype(F32)) * y_ssm)
    out_ref[...] = x_ref[...] + mod_ref[5:6, :] * _bdot(mixed, wout_ref[...])


def _mix(x, mod, attn_o, y_ssm, gates, w_ap, w_glu, w_out, layer):
    rows = lambda b, i: (b, i, 0)
    return pl.pallas_call(
        functools.partial(_mix_kernel, layer=layer),
        grid=(BATCH, SEQ // MIX_TM),
        in_specs=[
            pl.BlockSpec((None, MIX_TM, D_MODEL), rows),
            pl.BlockSpec((None, None, N_MOD, D_MODEL), lambda b, i: (layer, b, 0, 0)),
            pl.BlockSpec((None, MIX_TM, GROUP_WIDTH), rows),
            pl.BlockSpec((None, MIX_TM, SSM_WIDTH), rows),
            pl.BlockSpec((None, MIX_TM, D_MODEL), rows),
            pl.BlockSpec((None, MIX_TM, D_MODEL), lambda b, i: (b, i, 1)),
            pl.BlockSpec(memory_space=pl.ANY),
            pl.BlockSpec(memory_space=pl.ANY),
            pl.BlockSpec(memory_space=pl.ANY),
        ],
        out_specs=pl.BlockSpec((None, MIX_TM, D_MODEL), rows),
        out_shape=jax.ShapeDtypeStruct((BATCH, SEQ, D_MODEL), F32),
        scratch_shapes=(_weight_scratch(GROUP_WIDTH, D_MODEL, WEIGHT_CHUNKS)
                        + _weight_scratch(SSM_WIDTH, 2 * D_MODEL, WEIGHT_CHUNKS)
                        + _weight_scratch(D_MODEL, D_MODEL, WEIGHT_CHUNKS)),
        compiler_params=_cparams(("arbitrary", "arbitrary")),
        name="mix_out",
    )(x, mod, attn_o, y_ssm, gates, gates, w_ap, w_glu, w_out)


def kernel(x, c, w_ada, b_ada, norm_ffn1, w_ffn1_in, w_ffn1_out, norm_mix, w_in, rel_bias, lam_re, lam_im, log_dt, b_re, b_im, c_re, c_im, d_skip, w_glu, w_attn_proj, w_out, norm_ffn2, w_ffn2_in, w_ffn2_out, final_norm):
    c_pad = jnp.zeros((SUBLANES, D_MODEL), F32).at[:BATCH].set(c)
    mod = _ada(c_pad, w_ada, b_ada).reshape(DEPTH, SUBLANES, N_MOD, D_MODEL)
    bias_tabs = _attn_bias_tables(rel_bias)
    a_re, a_im, f_re, f_im = _ssm_params(lam_re, lam_im, log_dt)
    bb_re, bb_im, cc_re, cc_im = _ssm_matrices(f_re, f_im, b_re, b_im, c_re, c_im)
    a_re = a_re.reshape(DEPTH * SSM_SLABS, SLAB_TILES, LANES)
    a_im = a_im.reshape(DEPTH * SSM_SLABS, SLAB_TILES, LANES)
    d_skip = d_skip.reshape(DEPTH * SSM_SLABS, 1, SLAB_CH)
    for l in range(DEPTH):
        x = _ffn(x, mod, norm_ffn1, w_ffn1_in, w_ffn1_out, l, 0)

        *qkv_groups, u, gates = _inproj(x, mod, norm_mix, w_in, l)
        attn_o = _attention([qkv.reshape(BATCH, SEQ, QKV_WIDTH) for qkv in qkv_groups], bias_tabs)
        y_ssm = _ssm(u, bb_re, bb_im, cc_re, cc_im, a_re, a_im, d_skip, l)
        x = _mix(x, mod, attn_o, y_ssm, gates, w_attn_proj, w_glu, w_out, l)

        x = _ffn(x, mod, norm_ffn2, w_ffn2_in, w_ffn2_out, l, 6,
                 final_norm=final_norm if l == DEPTH - 1 else None)
    return x
```

```python
import functools
import math

import jax
import jax.numpy as jnp
import numpy as np
from jax import lax
from jax.experimental import pallas as pl
from jax.experimental.pallas import tpu as pltpu

D_MODEL = 1024
BATCH = 4
SEQ = 4096
DEPTH = 2
HEAD_DIM = 64
HEADS_PER_GROUP = 8
ATTN_GROUPS = ((128, 1), (512, 4), (2048, 16))
N_GROUPS = len(ATTN_GROUPS)
GROUP_WIDTH = HEADS_PER_GROUP * HEAD_DIM
BLOCK = 128
REL_BUCKETS = 32
REL_MAX_DIST = 2048
NEG = -1e30
SSM_WIDTH = 512
SSM_GROUP = 16
SSM_GROUPS = 32
SSM_STATE = 64
D_FF = 2816
QKV_COLS = 3 * N_GROUPS * GROUP_WIDTH
IN_COLS = QKV_COLS + SSM_WIDTH + 2 * D_MODEL
N_MOD = 9
EPS = 1e-6

LANES = 128
SUBLANES = 8
MXU_DIM = 256
FREE_STRIDE = 4
VMEM_LIMIT = 56 * 1024 * 1024

ADA_TN = 1152
FFN_TM = 1024
FFN_TF = MXU_DIM
FFN_RB = 256
PROJ_TM = 512
QKV_TILES = 3 * GROUP_WIDTH // LANES
MIX_TM = 512
WEIGHT_CHUNKS = 8
HEAD_PAIRS = GROUP_WIDTH // LANES
ATTN_UNROLL = 16
ATTN_ORDER = (2, 1, 0)
LOG2E = math.log2(math.e)
QSCALE = HEAD_DIM ** -0.5 * LOG2E
N_QBLOCKS = SEQ // BLOCK
SSM_TC = 512
SSM_PITCH = SSM_TC + SUBLANES
SLAB_CH = MXU_DIM
SSM_SLABS = SSM_WIDTH // SLAB_CH
SLAB_STATES = (SSM_GROUPS // SSM_SLABS) * SSM_STATE
SLAB_TILES = SLAB_STATES // LANES
HALF_TILES = SLAB_TILES * BATCH // SUBLANES

BF16 = jnp.bfloat16
F32 = jnp.float32


def _cparams(sem):
    return pltpu.CompilerParams(dimension_semantics=sem, vmem_limit_bytes=VMEM_LIMIT)


def _first_grid_step():
    return (pl.program_id(0) == 0) & (pl.program_id(1) == 0)


def _weight_scratch(k, n, chunks):
    return [pltpu.VMEM((k, n), BF16), pltpu.VMEM((2, k // chunks, n), F32),
            pltpu.SemaphoreType.DMA((2,))]


def _load_weight(w_hbm, layer, w_ref, stage_ref, sem):
    rows = stage_ref.shape[1]
    n_chunks = w_ref.shape[0] // rows

    def chunk(i, slot):
        return pltpu.make_async_copy(w_hbm.at[layer, pl.ds(i * rows, rows), :],
                                     stage_ref.at[slot], sem.at[slot])

    chunk(0, 0).start()

    def body(i, carry):
        slot = i % 2

        @pl.when(i + 1 < n_chunks)
        def _():
            chunk(i + 1, 1 - slot).start()

        chunk(i, slot).wait()
        w_ref[pl.ds(pl.multiple_of(i * rows, rows), rows), :] = stage_ref[slot].astype(BF16)
        return carry

    lax.fori_loop(0, n_chunks, body, 0)


def _bdot(a, b):
    return jnp.dot(a.astype(BF16), b.astype(BF16), preferred_element_type=F32)


def _sigmoid(x):
    return 0.5 * jnp.tanh(0.5 * x) + 0.5


def _norm_mod(x, g, shift, scale):
    ms = jnp.mean(x * x, axis=-1, keepdims=True)
    y = x * lax.rsqrt(ms + EPS) * g
    return y * (1.0 + scale) + shift


def _ada_kernel(c_ref, w_ref, b_ref, o_ref):
    c = c_ref[...]
    ca = c * _sigmoid(c)
    o_ref[...] = _bdot(ca, w_ref[...]) + b_ref[...]


def _ada(c_pad, w_ada, b_ada):
    n = N_MOD * D_MODEL
    return pl.pallas_call(
        _ada_kernel,
        grid=(DEPTH, n // ADA_TN),
        in_specs=[
            pl.BlockSpec((SUBLANES, D_MODEL), lambda l, j: (0, 0)),
            pl.BlockSpec((None, D_MODEL, ADA_TN), lambda l, j: (l, 0, j)),
            pl.BlockSpec((None, 1, ADA_TN), lambda l, j: (l, 0, j)),
        ],
        out_specs=pl.BlockSpec((None, SUBLANES, ADA_TN), lambda l, j: (l, 0, j)),
        out_shape=jax.ShapeDtypeStruct((DEPTH, SUBLANES, n), F32),
        compiler_params=_cparams(("arbitrary", "arbitrary")),
        name="ada_mod",
    )(c_pad, w_ada, b_ada.reshape(DEPTH, 1, n))


def _ffn_kernel(x_ref, mod_ref, g_ref, win_hbm, wout_hbm, *rest, layer, row0, final):
    if final:
        fn_ref, rest = rest[0], rest[1:]
    o_ref, act_ref, win_ref, wout_ref, a_stage, b_stage, o_stage, sem = rest
    n_chunks = D_FF // FFN_TF

    def chunk_copies(j, slot):
        lo, hi = j * FFN_TF, (j + 1) * FFN_TF
        return (pltpu.make_async_copy(win_hbm.at[layer, :, lo:hi], a_stage.at[slot], sem.at[0, slot]),
                pltpu.make_async_copy(win_hbm.at[layer, :, D_FF + lo:D_FF + hi], b_stage.at[slot],
                                      sem.at[1, slot]),
                pltpu.make_async_copy(wout_hbm.at[layer, lo:hi, :], o_stage.at[slot], sem.at[2, slot]))

    blocks = [slice(r0, r0 + FFN_RB) for r0 in range(0, FFN_TM, FFN_RB)]

    def tile(load_weights):
        hbs = [_norm_mod(x_ref[rows, :], g_ref[...], mod_ref[row0:row0 + 1, :],
                         mod_ref[row0 + 1:row0 + 2, :]).astype(BF16) for rows in blocks]
        if load_weights:
            for copy in chunk_copies(0, 0):
                copy.start()
        for j in range(n_chunks):
            lo, hi = j * FFN_TF, (j + 1) * FFN_TF
            if load_weights:
                if j + 1 < n_chunks:
                    for copy in chunk_copies(j + 1, (j + 1) % 2):
                        copy.start()
                for copy in chunk_copies(j, j % 2):
                    copy.wait()
                win_ref[:, lo:hi] = a_stage[j % 2].astype(BF16)
                win_ref[:, D_FF + lo:D_FF + hi] = b_stage[j % 2].astype(BF16)
                wout_ref[lo:hi, :] = o_stage[j % 2].astype(BF16)
            if j == 0:
                pieces = list(zip(blocks, hbs))
            else:
                if j == 1:
                    hb = jnp.concatenate(hbs, axis=0)
                pieces = [(slice(None), hb)]
            for rows, lhs in pieces:
                a = jnp.dot(lhs, win_ref[:, lo:hi], preferred_element_type=F32)
                b = jnp.dot(lhs, win_ref[:, D_FF + lo:D_FF + hi], preferred_element_type=F32)
                act_ref[rows, lo:hi] = (a * _sigmoid(a) * b).astype(BF16)
        for rows in blocks:
            y = jnp.dot(act_ref[rows, :], wout_ref[...], preferred_element_type=F32)
            out = x_ref[rows, :] + (0.5 * mod_ref[row0 + 2:row0 + 3, :]) * y
            if final:
                ms = jnp.mean(out * out, axis=-1, keepdims=True)
                out = out * lax.rsqrt(ms + EPS) * fn_ref[...]
            o_ref[rows, :] = out

    first = _first_grid_step()
    pl.when(first)(functools.partial(tile, True))
    pl.when(jnp.logical_not(first))(functools.partial(tile, False))


def _ffn(x, mod, norm_g, w_in, w_out, layer, row0, final_norm=None):
    final = final_norm is not None
    in_specs = [
        pl.BlockSpec((None, FFN_TM, D_MODEL), lambda b, i: (b, i, 0)),
        pl.BlockSpec((None, None, N_MOD, D_MODEL), lambda b, i: (layer, b, 0, 0)),
        pl.BlockSpec((None, 1, D_MODEL), lambda b, i: (layer, 0, 0)),
        pl.BlockSpec(memory_space=pl.ANY),
        pl.BlockSpec(memory_space=pl.ANY),
    ]
    args = [x, mod, norm_g.reshape(DEPTH, 1, D_MODEL), w_in, w_out]
    if final:
        in_specs.append(pl.BlockSpec((1, D_MODEL), lambda b, i: (0, 0)))
        args.append(final_norm.reshape(1, D_MODEL))
    return pl.pallas_call(
        functools.partial(_ffn_kernel, layer=layer, row0=row0, final=final),
        grid=(BATCH, SEQ // FFN_TM),
        in_specs=in_specs,
        out_specs=pl.BlockSpec((None, FFN_TM, D_MODEL), lambda b, i: (b, i, 0)),
        out_shape=jax.ShapeDtypeStruct((BATCH, SEQ, D_MODEL), F32),
        scratch_shapes=[
            pltpu.VMEM((FFN_TM, D_FF), BF16),
            pltpu.VMEM((D_MODEL, 2 * D_FF), BF16),
            pltpu.VMEM((D_FF, D_MODEL), BF16),
            pltpu.VMEM((2, D_MODEL, FFN_TF), F32),
            pltpu.VMEM((2, D_MODEL, FFN_TF), F32),
            pltpu.VMEM((2, FFN_TF, D_MODEL), F32),
            pltpu.SemaphoreType.DMA((3, 2)),
        ],
        compiler_params=_cparams(("arbitrary", "arbitrary")),
        name="ffn_final" if final else "ffn",
    )(*args)


def _rows_by_residue(hf, rows_ref, half_ref):
    tiles = D_MODEL // LANES
    out = {1: hf.astype(BF16)}
    if all(d == 1 for _, d in ATTN_GROUPS):
        return out
    for t in range(tiles):
        rows_ref[t] = hf[:, t * LANES:(t + 1) * LANES]

    def gather(ref, pieces):
        return jnp.concatenate(
            [jnp.concatenate([ref[t, pl.ds(start, count, stride=stride), :]
                              for start, count, stride in pieces], axis=0)
             for t in range(tiles)], axis=1)

    per_inner = PROJ_TM // FREE_STRIDE
    inner = gather(rows_ref, [(b, per_inner, FREE_STRIDE) for b in range(FREE_STRIDE)])
    for _, dilation in ATTN_GROUPS:
        if dilation == 1 or dilation in out:
            continue
        if dilation == FREE_STRIDE:
            out[dilation] = inner.astype(BF16)
            continue
        outer = dilation // FREE_STRIDE
        assert dilation % FREE_STRIDE == 0 and outer <= FREE_STRIDE
        for t in range(tiles):
            half_ref[t] = inner[:, t * LANES:(t + 1) * LANES]
        pieces = [(b * per_inner + a, PROJ_TM // dilation, outer)
                  for a, b in (divmod(r, FREE_STRIDE) for r in range(dilation))]
        out[dilation] = gather(half_ref, pieces).astype(BF16)
    return out


def _inproj_kernel(x_ref, mod_ref, g_ref, w_hbm, qkv0_ref, qkv1_ref, qkv2_ref, u_ref, gate_ref,
                   w_ref, w_stage, w_sem, rows_ref, half_ref, *, layer):
    qkv_refs = (qkv0_ref, qkv1_ref, qkv2_ref)
    n_segs = IN_COLS // GROUP_WIDTH
    order = sorted(range(n_segs),
                   key=lambda seg: ATTN_GROUPS[seg % N_GROUPS][1] if seg < 3 * N_GROUPS else 1)

    def seg_copy(k, slot):
        seg = order[k]
        return pltpu.make_async_copy(
            w_hbm.at[layer, :, seg * GROUP_WIDTH:(seg + 1) * GROUP_WIDTH], w_stage.at[slot],
            w_sem.at[slot])

    def tile(load_weights):
        hf = _norm_mod(x_ref[...], g_ref[...], mod_ref[3:4, :], mod_ref[4:5, :])
        h = _rows_by_residue(hf, rows_ref, half_ref)
        if load_weights:
            seg_copy(0, 0).start()
        for k, seg in enumerate(order):
            cols = slice(seg * GROUP_WIDTH, (seg + 1) * GROUP_WIDTH)
            if load_weights:
                if k + 1 < n_segs:
                    seg_copy(k + 1, (k + 1) % 2).start()
                seg_copy(k, k % 2).wait()
                w_ref[:, cols] = w_stage[k % 2].astype(BF16)
            if seg < 3 * N_GROUPS:
                which, g = divmod(seg, N_GROUPS)
                dilation = ATTN_GROUPS[g][1]
                ys = jnp.dot(h[dilation], w_ref[:, cols], preferred_element_type=F32)
                if which == 0:
                    ys = ys * QSCALE
                n = PROJ_TM // dilation
                yb = ys.astype(BF16)
                for hp in range(HEAD_PAIRS):
                    for r in range(dilation):
                        qkv_refs[g][which * HEAD_PAIRS + hp, r, :, :] = (
                            yb[r * n:(r + 1) * n, hp * LANES:(hp + 1) * LANES])
            else:
                ys = jnp.dot(h[1], w_ref[:, cols], preferred_element_type=F32)
                if seg == 3 * N_GROUPS:
                    u_ref[...] = ys
                else:
                    g0 = (seg - 3 * N_GROUPS - 1) * GROUP_WIDTH
                    gate_ref[:, g0:g0 + GROUP_WIDTH] = ys.astype(BF16)

    first = _first_grid_step()
    pl.when(first)(functools.partial(tile, True))
    pl.when(jnp.logical_not(first))(functools.partial(tile, False))


def _inproj(x, mod, norm_g, w_in, layer):
    rows = lambda b, i: (b, i, 0)
    qkv_specs, qkv_shapes = [], []
    for _, dilation in ATTN_GROUPS:
        qkv_specs.append(pl.BlockSpec((None, QKV_TILES, dilation, PROJ_TM // dilation, LANES),
                                      lambda b, i: (b, 0, 0, i, 0)))
        qkv_shapes.append(jax.ShapeDtypeStruct(
            (BATCH, QKV_TILES, dilation, SEQ // dilation, LANES), BF16))
    tiles = D_MODEL // LANES
    return pl.pallas_call(
        functools.partial(_inproj_kernel, layer=layer),
        grid=(BATCH, SEQ // PROJ_TM),
        in_specs=[
            pl.BlockSpec((None, PROJ_TM, D_MODEL), rows),
            pl.BlockSpec((None, None, N_MOD, D_MODEL), lambda b, i: (layer, b, 0, 0)),
            pl.BlockSpec((None, 1, D_MODEL), lambda b, i: (layer, 0, 0)),
            pl.BlockSpec(memory_space=pl.ANY),
        ],
        out_specs=(*qkv_specs,
                   pl.BlockSpec((None, PROJ_TM, SSM_WIDTH), rows),
                   pl.BlockSpec((None, PROJ_TM, 2 * D_MODEL), rows)),
        out_shape=(*qkv_shapes,
                   jax.ShapeDtypeStruct((BATCH, SEQ, SSM_WIDTH), F32),
                   jax.ShapeDtypeStruct((BATCH, SEQ, 2 * D_MODEL), BF16)),
        scratch_shapes=[
            pltpu.VMEM((D_MODEL, IN_COLS), BF16),
            pltpu.VMEM((2, D_MODEL, GROUP_WIDTH), F32),
            pltpu.SemaphoreType.DMA((2,)),
            pltpu.VMEM((tiles, PROJ_TM, LANES), F32),
            pltpu.VMEM((tiles, PROJ_TM, LANES), F32),
        ],
        compiler_params=_cparams(("arbitrary", "arbitrary")),
        name="inproj",
    )(x, mod, norm_g.reshape(DEPTH, 1, D_MODEL), w_in)


def _t5_bucket(dist):
    max_exact = REL_BUCKETS // 2
    d = np.maximum(dist, max_exact).astype(np.float32)
    large = max_exact + (np.log(d / max_exact) / np.log(REL_MAX_DIST / max_exact)
                         * (REL_BUCKETS - max_exact)).astype(np.int32)
    large = np.minimum(large, REL_BUCKETS - 1)
    return np.where(dist < max_exact, dist, large).astype(np.int32)


def _attn_bias_tables(rel_bias):
    qi = np.arange(BLOCK)[:, None]
    kj = np.arange(2 * BLOCK)[None, :]
    rel = BLOCK + qi - kj
    tabs = []
    for g, (window, dilation) in enumerate(ATTN_GROUPS):
        band = (rel >= 0) & (rel <= window // dilation)
        bucket = _t5_bucket(np.clip(rel, 0, None) * dilation)
        tbl = rel_bias[:, g * HEADS_PER_GROUP:(g + 1) * HEADS_PER_GROUP]
        onehot = jnp.asarray(bucket[None] == np.arange(REL_BUCKETS)[:, None, None], F32)
        bias = jnp.einsum('rqk,rh->hqk', onehot, tbl.astype(F32),
                          precision=lax.Precision.HIGHEST)
        general = jnp.where(band[None], bias * LOG2E, NEG)
        masked = jnp.full((HEADS_PER_GROUP, BLOCK, BLOCK), NEG, F32)
        first = jnp.concatenate([masked, general[:, :, BLOCK:]], axis=2)
        very_first = jnp.concatenate([general[:, :, BLOCK:], masked], axis=2)
        tabs.append(jnp.stack([very_first, first, general]))
    return jnp.stack(tabs).reshape(N_GROUPS, 3, HEAD_PAIRS, 2 * BLOCK, 2 * BLOCK)


def _attn_group(q_ref, k_ref, v_ref, bias_ref, o_ref, part_ref, *, dilation, slot):
    nb = SEQ // dilation // BLOCK
    lane = lax.broadcasted_iota(jnp.int32, (BLOCK, LANES), 1)
    head0 = lane < HEAD_DIM
    keep0 = jnp.where(head0, 1.0, 0.0).astype(BF16)
    keep1 = jnp.where(head0, 0.0, 1.0).astype(BF16)
    ones_cols = jnp.ones((2 * BLOCK, LANES), BF16)
    contract_last = (((1,), (1,)), ((), ()))

    def token_rows(n):
        start = n // nb + (n % nb) * (BLOCK * dilation)
        if dilation == 1:
            start = pl.multiple_of(start, BLOCK)
        return pl.ds(start, BLOCK, stride=dilation)

    def key_rows(n):
        return pl.ds(pl.multiple_of(jnp.maximum(n - 1, 0) * BLOCK, BLOCK), 2 * BLOCK)

    def logits(n):
        qb = q_ref[pl.ds(pl.multiple_of(n * BLOCK, BLOCK), BLOCK), :]
        qq = jnp.concatenate([qb * keep0, qb * keep1], axis=0)
        tab = jnp.where(n == 0, 0, jnp.where(n % nb == 0, 1, 2))
        l = lax.dot_general(qq, k_ref[key_rows(n), :], contract_last,
                            preferred_element_type=F32) + bias_ref[tab]
        return l, jnp.max(l, axis=1, keepdims=True)

    def weighted(n, l, m_rows):
        v_aug = jnp.concatenate([v_ref[key_rows(n), :], ones_cols], axis=1)
        p = jnp.exp2(l - m_rows).astype(BF16)
        r = jnp.dot(p, v_aug, preferred_element_type=F32)
        return (jnp.where(head0, r[:BLOCK, :LANES], r[BLOCK:, :LANES]),
                jnp.where(head0, r[:BLOCK, LANES:], r[BLOCK:, LANES:]))

    def body(step, carry):
        ns = [step * ATTN_UNROLL + u for u in range(ATTN_UNROLL)]
        scores = [logits(n) for n in ns]
        for n, (l, m_rows) in zip(ns, scores):
            acc, den = weighted(n, l, m_rows)
            m_blk = jnp.where(head0, m_rows[:BLOCK], m_rows[BLOCK:])
            rows = token_rows(n)
            if slot is not None:
                part_ref[slot, 0, rows, :] = acc / den
                part_ref[slot, 1, rows, :] = m_blk + jnp.log2(den)
            else:
                others = [(part_ref[g, 0, rows, :], part_ref[g, 1, rows, :])
                          for g in range(N_GROUPS - 1)]
                m_all = functools.reduce(jnp.maximum, [m_blk] + [lse for _, lse in others])
                w_own = jnp.exp2(m_blk - m_all)
                num = w_own * acc
                den_all = w_own * den
                for out_g, lse in others:
                    w = jnp.exp2(lse - m_all)
                    num = num + w * out_g
                    den_all = den_all + w
                o_ref[rows, :] = num / den_all
        return carry

    lax.fori_loop(0, N_QBLOCKS // ATTN_UNROLL, body, 0)


def _attn_kernel(*refs):
    qkv_refs, (bias_ref, o_ref, part_ref) = refs[:3 * N_GROUPS], refs[3 * N_GROUPS:]
    for i, gi in enumerate(ATTN_ORDER):
        q_ref, k_ref, v_ref = qkv_refs[3 * gi:3 * gi + 3]
        _attn_group(q_ref, k_ref, v_ref, bias_ref.at[gi], o_ref, part_ref,
                    dilation=ATTN_GROUPS[gi][1], slot=i if i < N_GROUPS - 1 else None)


def _attention(qkv_groups, bias_tabs):
    in_specs, args = [], []
    for qkv in qkv_groups:
        for which in range(3):
            in_specs.append(pl.BlockSpec(
                (None, None, SEQ, LANES),
                lambda b, hp, which=which: (b, which * HEAD_PAIRS + hp, 0, 0)))
            args.append(qkv)
    in_specs.append(pl.BlockSpec((N_GROUPS, 3, None, 2 * BLOCK, 2 * BLOCK),
                                 lambda b, hp: (0, 0, hp, 0, 0)))
    return pl.pallas_call(
        _attn_kernel,
        grid=(BATCH, HEAD_PAIRS),
        in_specs=in_specs,
        out_specs=pl.BlockSpec((None, SEQ, LANES), lambda b, hp: (b, 0, hp)),
        out_shape=jax.ShapeDtypeStruct((BATCH, SEQ, GROUP_WIDTH), F32),
        scratch_shapes=[pltpu.VMEM((N_GROUPS - 1, 2, SEQ, LANES), F32)],
        compiler_params=_cparams(("arbitrary", "arbitrary")),
        name="dilated_attn",
    )(*args, bias_tabs)


def _ssm_param_kernel(lre_ref, lim_ref, ldt_ref, are_ref, aim_ref, fre_ref, fim_ref):
    lam_re = lre_ref[...]
    lam_im = lim_ref[...]
    dt = jnp.exp(ldt_ref[...])
    mag = jnp.exp(lam_re * dt)
    ang = lam_im * dt
    a_re = mag * jnp.cos(ang)
    a_im = mag * jnp.sin(ang)
    den = lam_re * lam_re + lam_im * lam_im
    are_ref[...] = a_re
    aim_ref[...] = a_im
    fre_ref[...] = ((a_re - 1) * lam_re + a_im * lam_im) / den
    fim_ref[...] = (a_im * lam_re - (a_re - 1) * lam_im) / den


def _ssm_params(lam_re, lam_im, log_dt):
    n = DEPTH * SSM_GROUPS
    shp = jax.ShapeDtypeStruct((n, SSM_STATE), F32)
    return pl.pallas_call(
        _ssm_param_kernel, out_shape=(shp, shp, shp, shp), name="ssm_discretise",
    )(lam_re.reshape(n, SSM_STATE), lam_im.reshape(n, SSM_STATE), log_dt.reshape(n, 1))


def _ssm_kernel(u_ref, bre_ref, bim_ref, cre_ref, cim_ref, are_ref, aim_ref, d_ref,
                y_ref, sre_ref, sim_ref, xre_ref, xim_ref):
    c = pl.program_id(1)

    @pl.when(c == 0)
    def _():
        xre_ref[...] = jnp.zeros_like(xre_ref)
        xim_ref[...] = jnp.zeros_like(xim_ref)

    def plane_rows(t, b):
        half, pair = divmod(t, HALF_TILES)
        r0 = (half * BATCH + b) * SSM_PITCH
        return pair, slice(r0, r0 + SSM_TC)

    for b in range(BATCH):
        ub = u_ref[b].astype(BF16)
        bu_re = jnp.dot(ub, bre_ref[...], preferred_element_type=F32)
        bu_im = jnp.dot(ub, bim_ref[...], preferred_element_type=F32)
        for t in range(SLAB_TILES):
            pair, sl = plane_rows(t, b)
            sre_ref[pair, sl, :] = bu_re[:, t * LANES:(t + 1) * LANES]
            sim_ref[pair, sl, :] = bu_im[:, t * LANES:(t + 1) * LANES]

    def coeff(ref, pair):
        return jnp.concatenate(
            [jnp.broadcast_to(ref[half * HALF_TILES + pair:half * HALF_TILES + pair + 1, :],
                              (BATCH, LANES)) for half in range(2)], axis=0)

    a_re = [coeff(are_ref, p) for p in range(HALF_TILES)]
    a_im = [coeff(aim_ref, p) for p in range(HALF_TILES)]

    def step(i, carry):
        xs = list(carry)
        rows = pl.ds(i, SUBLANES, stride=SSM_PITCH)
        for p in range(HALF_TILES):
            xr, xi = xs[2 * p], xs[2 * p + 1]
            nr = a_re[p] * xr - a_im[p] * xi + sre_ref[p, rows, :]
            ni = a_re[p] * xi + a_im[p] * xr + sim_ref[p, rows, :]
            sre_ref[p, rows, :] = nr
            sim_ref[p, rows, :] = ni
            xs[2 * p], xs[2 * p + 1] = nr, ni
        return tuple(xs)

    init = []
    for p in range(HALF_TILES):
        init += [xre_ref[p], xim_ref[p]]
    fin = lax.fori_loop(0, SSM_TC, step, tuple(init), unroll=8)
    for p in range(HALF_TILES):
        xre_ref[p] = fin[2 * p]
        xim_ref[p] = fin[2 * p + 1]

    for b in range(BATCH):
        tiles = [plane_rows(t, b) for t in range(SLAB_TILES)]
        x_re = jnp.concatenate([sre_ref[pair, sl, :] for pair, sl in tiles], axis=1)
        x_im = jnp.concatenate([sim_ref[pair, sl, :] for pair, sl in tiles], axis=1)
        y = (_bdot(x_re, cre_ref[...]) + _bdot(x_im, cim_ref[...])
             + d_ref[...] * u_ref[b])
        y_ref[b] = jax.nn.gelu(y).astype(BF16)


def _ssm(u, bb_re, bb_im, cc_re, cc_im, a_re, a_im, d_skip, layer):
    slab = lambda s, c: (layer * SSM_SLABS + s, 0, 0)
    return pl.pallas_call(
        _ssm_kernel,
        grid=(SSM_SLABS, SEQ // SSM_TC),
        in_specs=[
            pl.BlockSpec((BATCH, SSM_TC, SLAB_CH), lambda s, c: (0, c, s)),
            pl.BlockSpec((None, SLAB_CH, SLAB_STATES), slab),
            pl.BlockSpec((None, SLAB_CH, SLAB_STATES), slab),
            pl.BlockSpec((None, SLAB_STATES, SLAB_CH), slab),
            pl.BlockSpec((None, SLAB_STATES, SLAB_CH), slab),
            pl.BlockSpec((None, SLAB_TILES, LANES), slab),
            pl.BlockSpec((None, SLAB_TILES, LANES), slab),
            pl.BlockSpec((None, 1, SLAB_CH), slab),
        ],
        out_specs=pl.BlockSpec((BATCH, SSM_TC, SLAB_CH), lambda s, c: (0, c, s)),
        out_shape=jax.ShapeDtypeStruct((BATCH, SEQ, SSM_WIDTH), BF16),
        scratch_shapes=[
            pltpu.VMEM((HALF_TILES, SUBLANES * SSM_PITCH, LANES), F32),
            pltpu.VMEM((HALF_TILES, SUBLANES * SSM_PITCH, LANES), F32),
            pltpu.VMEM((HALF_TILES, SUBLANES, LANES), F32),
            pltpu.VMEM((HALF_TILES, SUBLANES, LANES), F32),
        ],
        compiler_params=_cparams(("arbitrary", "arbitrary")),
        name="s5_scan",
    )(u, bb_re, bb_im, cc_re, cc_im, a_re, a_im, d_skip)


def _ssm_matrices(f_re, f_im, b_re, b_im, c_re, c_im):
    n = DEPTH * SSM_GROUPS
    f_re, f_im = f_re[..., None], f_im[..., None]
    b_re = b_re.reshape(n, SSM_STATE, SSM_GROUP)
    b_im = b_im.reshape(n, SSM_STATE, SSM_GROUP)
    bb_re = f_re * b_re - f_im * b_im
    bb_im = f_re * b_im + f_im * b_re
    gps = SSM_GROUPS // SSM_SLABS
    slabs = DEPTH * SSM_SLABS
    eye = jnp.eye(gps, dtype=F32)

    def in_map(bb):
        bb = bb.reshape(slabs, gps, SSM_STATE, SSM_GROUP)
        return jnp.einsum('sgpi,gh->sgihp', bb, eye).reshape(slabs, SLAB_CH, SLAB_STATES)

    def out_map(cc):
        cc = cc.reshape(slabs, gps, SSM_GROUP, SSM_STATE)
        return jnp.einsum('sgcp,gh->sgphc', cc, eye).reshape(slabs, SLAB_STATES, SLAB_CH)

    return (in_map(bb_re).astype(BF16), in_map(bb_im).astype(BF16),
            out_map(c_re).astype(BF16), out_map(-c_im).astype(BF16))


def _mix_kernel(x_ref, mod_ref, o_ref_in, ys_ref, ga_ref, gs_ref, wap_hbm, wglu_hbm, wout_hbm,
                out_ref, wap_ref, wap_stage, wap_sem, wglu_ref, wglu_stage, wglu_sem,
                wout_ref, wout_stage, wout_sem, *, layer):
    @pl.when(_first_grid_step())
    def _():
        _load_weight(wap_hbm, layer, wap_ref, wap_stage, wap_sem)
        _load_weight(wglu_hbm, layer, wglu_ref, wglu_stage, wglu_sem)
        _load_weight(wout_hbm, layer, wout_ref, wout_stage, wout_sem)

    y_attn = _bdot(o_ref_in[...], wap_ref[...])
    gl = jnp.dot(ys_ref[...], wglu_ref[...], preferred_element_type=F32)
    y_ssm = gl[:, :D_MODEL] * _sigmoid(gl[:, D_MODEL:])
    mixed = (_sigmoid(ga_ref[...].astype(F32)) * y_attn
             + _sigmoid(gs_ref[...].astype(F32)) * y_ssm)
    out_ref[...] = x_ref[...] + mod_ref[5:6, :] * _bdot(mixed, wout_ref[...])


def _mix(x, mod, attn_o, y_ssm, gates, w_ap, w_glu, w_out, layer):
    rows = lambda b, i: (b, i, 0)
    return pl.pallas_call(
        functools.partial(_mix_kernel, layer=layer),
        grid=(BATCH, SEQ // MIX_TM),
        in_specs=[
            pl.BlockSpec((None, MIX_TM, D_MODEL), rows),
            pl.BlockSpec((None, None, N_MOD, D_MODEL), lambda b, i: (layer, b, 0, 0)),
            pl.BlockSpec((None, MIX_TM, GROUP_WIDTH), rows),
            pl.BlockSpec((None, MIX_TM, SSM_WIDTH), rows),
            pl.BlockSpec((None, MIX_TM, D_MODEL), rows),
            pl.BlockSpec((None, MIX_TM, D_MODEL), lambda b, i: (b, i, 1)),
            pl.BlockSpec(memory_space=pl.ANY),
            pl.BlockSpec(memory_space=pl.ANY),
            pl.BlockSpec(memory_space=pl.ANY),
        ],
        out_specs=pl.BlockSpec((None, MIX_TM, D_MODEL), rows),
        out_shape=jax.ShapeDtypeStruct((BATCH, SEQ, D_MODEL), F32),
        scratch_shapes=(_weight_scratch(GROUP_WIDTH, D_MODEL, WEIGHT_CHUNKS)
                        + _weight_scratch(SSM_WIDTH, 2 * D_MODEL, WEIGHT_CHUNKS)
                        + _weight_scratch(D_MODEL, D_MODEL, WEIGHT_CHUNKS)),
        compiler_params=_cparams(("arbitrary", "arbitrary")),
        name="mix_out",
    )(x, mod, attn_o, y_ssm, gates, gates, w_ap, w_glu, w_out)


def kernel(x, c, w_ada, b_ada, norm_ffn1, w_ffn1_in, w_ffn1_out, norm_mix, w_in, rel_bias, lam_re, lam_im, log_dt, b_re, b_im, c_re, c_im, d_skip, w_glu, w_attn_proj, w_out, norm_ffn2, w_ffn2_in, w_ffn2_out, final_norm):
    c_pad = jnp.zeros((SUBLANES, D_MODEL), F32).at[:BATCH].set(c)
    mod = _ada(c_pad, w_ada, b_ada).reshape(DEPTH, SUBLANES, N_MOD, D_MODEL)
    bias_tabs = _attn_bias_tables(rel_bias)
    a_re, a_im, f_re, f_im = _ssm_params(lam_re, lam_im, log_dt)
    bb_re, bb_im, cc_re, cc_im = _ssm_matrices(f_re, f_im, b_re, b_im, c_re, c_im)
    a_re = a_re.reshape(DEPTH * SSM_SLABS, SLAB_TILES, LANES)
    a_im = a_im.reshape(DEPTH * SSM_SLABS, SLAB_TILES, LANES)
    d_skip = d_skip.reshape(DEPTH * SSM_SLABS, 1, SLAB_CH)
    for l in range(DEPTH):
        x = _ffn(x, mod, norm_ffn1, w_ffn1_in, w_ffn1_out, l, 0)

        *qkv_groups, u, gates = _inproj(x, mod, norm_mix, w_in, l)
        attn_o = _attention([qkv.reshape(BATCH, QKV_TILES, SEQ, LANES) for qkv in qkv_groups],
                            bias_tabs)
        y_ssm = _ssm(u, bb_re, bb_im, cc_re, cc_im, a_re, a_im, d_skip, l)
        x = _mix(x, mod, attn_o, y_ssm, gates, w_attn_proj, w_glu, w_out, l)

        x = _ffn(x, mod, norm_ffn2, w_ffn2_in, w_ffn2_out, l, 6,
                 final_norm=final_norm if l == DEPTH - 1 else None)
    return x
```

```python
import functools
import math

import jax
import jax.numpy as jnp
import numpy as np
from jax import lax
from jax.experimental import pallas as pl
from jax.experimental.pallas import tpu as pltpu

D_MODEL = 1024
BATCH = 4
SEQ = 4096
DEPTH = 2
HEAD_DIM = 64
HEADS_PER_GROUP = 8
ATTN_GROUPS = ((128, 1), (512, 4), (2048, 16))
N_GROUPS = len(ATTN_GROUPS)
GROUP_WIDTH = HEADS_PER_GROUP * HEAD_DIM
BLOCK = 128
REL_BUCKETS = 32
REL_MAX_DIST = 2048
NEG = -1e30
SSM_WIDTH = 512
SSM_GROUP = 16
SSM_GROUPS = 32
SSM_STATE = 64
D_FF = 2816
QKV_COLS = 3 * N_GROUPS * GROUP_WIDTH
IN_COLS = QKV_COLS + SSM_WIDTH + 2 * D_MODEL
N_MOD = 9
EPS = 1e-6

LANES = 128
SUBLANES = 8
MXU_DIM = 256
FREE_STRIDE = 4
VMEM_LIMIT = 56 * 1024 * 1024

ADA_TN = 1152
FFN_TM = 1024
FFN_TF = MXU_DIM
PROJ_TM = 512
QKV_WIDTH = 3 * GROUP_WIDTH
MIX_TM = 512
WEIGHT_CHUNKS = 8
HEAD_PAIRS = GROUP_WIDTH // LANES
ATTN_UNROLL = 16
ATTN_ORDER = (2, 1, 0)
LOG2E = math.log2(math.e)
QSCALE = HEAD_DIM ** -0.5 * LOG2E
N_QBLOCKS = SEQ // BLOCK
SSM_TC = 512
SSM_PITCH = SSM_TC + SUBLANES
SLAB_CH = MXU_DIM
SSM_SLABS = SSM_WIDTH // SLAB_CH
SLAB_STATES = (SSM_GROUPS // SSM_SLABS) * SSM_STATE
SLAB_TILES = SLAB_STATES // LANES
HALF_TILES = SLAB_TILES * BATCH // SUBLANES

BF16 = jnp.bfloat16
F32 = jnp.float32


def _cparams(sem):
    return pltpu.CompilerParams(dimension_semantics=sem, vmem_limit_bytes=VMEM_LIMIT)


def _first_grid_step():
    return (pl.program_id(0) == 0) & (pl.program_id(1) == 0)


def _weight_scratch(k, n, chunks):
    return [pltpu.VMEM((k, n), BF16), pltpu.VMEM((2, k // chunks, n), F32),
            pltpu.SemaphoreType.DMA((2,))]


def _load_weight(w_hbm, layer, w_ref, stage_ref, sem):
    rows = stage_ref.shape[1]
    n_chunks = w_ref.shape[0] // rows

    def chunk(i, slot):
        return pltpu.make_async_copy(w_hbm.at[layer, pl.ds(i * rows, rows), :],
                                     stage_ref.at[slot], sem.at[slot])

    chunk(0, 0).start()

    def body(i, carry):
        slot = i % 2

        @pl.when(i + 1 < n_chunks)
        def _():
            chunk(i + 1, 1 - slot).start()

        chunk(i, slot).wait()
        w_ref[pl.ds(pl.multiple_of(i * rows, rows), rows), :] = stage_ref[slot].astype(BF16)
        return carry

    lax.fori_loop(0, n_chunks, body, 0)


def _bdot(a, b):
    return jnp.dot(a.astype(BF16), b.astype(BF16), preferred_element_type=F32)


def _sigmoid(x):
    return 0.5 * jnp.tanh(0.5 * x) + 0.5


def _norm_mod(x, g, shift, scale):
    ms = jnp.mean(x * x, axis=-1, keepdims=True)
    y = x * lax.rsqrt(ms + EPS) * g
    return y * (1.0 + scale) + shift


def _ada_kernel(c_ref, w_ref, b_ref, o_ref):
    c = c_ref[...]
    ca = c * _sigmoid(c)
    o_ref[...] = _bdot(ca, w_ref[...]) + b_ref[...]


def _ada(c_pad, w_ada, b_ada):
    n = N_MOD * D_MODEL
    return pl.pallas_call(
        _ada_kernel,
        grid=(DEPTH, n // ADA_TN),
        in_specs=[
            pl.BlockSpec((SUBLANES, D_MODEL), lambda l, j: (0, 0)),
            pl.BlockSpec((None, D_MODEL, ADA_TN), lambda l, j: (l, 0, j)),
            pl.BlockSpec((None, 1, ADA_TN), lambda l, j: (l, 0, j)),
        ],
        out_specs=pl.BlockSpec((None, SUBLANES, ADA_TN), lambda l, j: (l, 0, j)),
        out_shape=jax.ShapeDtypeStruct((DEPTH, SUBLANES, n), F32),
        compiler_params=_cparams(("arbitrary", "arbitrary")),
        name="ada_mod",
    )(c_pad, w_ada, b_ada.reshape(DEPTH, 1, n))


def _ffn_kernel(x_ref, mod_ref, g_ref, win_hbm, wout_hbm, *rest, layer, row0, final):
    if final:
        fn_ref, rest = rest[0], rest[1:]
    o_ref, act_ref, win_ref, wout_ref, a_stage, b_stage, o_stage, sem = rest
    n_chunks = D_FF // FFN_TF

    def chunk_copies(j, slot):
        lo, hi = j * FFN_TF, (j + 1) * FFN_TF
        return (pltpu.make_async_copy(win_hbm.at[layer, :, lo:hi], a_stage.at[slot], sem.at[0, slot]),
                pltpu.make_async_copy(win_hbm.at[layer, :, D_FF + lo:D_FF + hi], b_stage.at[slot],
                                      sem.at[1, slot]),
                pltpu.make_async_copy(wout_hbm.at[layer, lo:hi, :], o_stage.at[slot], sem.at[2, slot]))

    def tile(load_weights):
        x = x_ref[...]
        h = _norm_mod(x, g_ref[...], mod_ref[row0:row0 + 1, :], mod_ref[row0 + 1:row0 + 2, :])
        hb = h.astype(BF16)
        if load_weights:
            for copy in chunk_copies(0, 0):
                copy.start()
        for j in range(n_chunks):
            lo, hi = j * FFN_TF, (j + 1) * FFN_TF
            if load_weights:
                if j + 1 < n_chunks:
                    for copy in chunk_copies(j + 1, (j + 1) % 2):
                        copy.start()
                for copy in chunk_copies(j, j % 2):
                    copy.wait()
                win_ref[:, lo:hi] = a_stage[j % 2].astype(BF16)
                win_ref[:, D_FF + lo:D_FF + hi] = b_stage[j % 2].astype(BF16)
                wout_ref[lo:hi, :] = o_stage[j % 2].astype(BF16)
            a = jnp.dot(hb, win_ref[:, lo:hi], preferred_element_type=F32)
            b = jnp.dot(hb, win_ref[:, D_FF + lo:D_FF + hi], preferred_element_type=F32)
            act_ref[:, lo:hi] = (a * _sigmoid(a) * b).astype(BF16)
        y = jnp.dot(act_ref[...], wout_ref[...], preferred_element_type=F32)
        out = x + (0.5 * mod_ref[row0 + 2:row0 + 3, :]) * y
        if final:
            ms = jnp.mean(out * out, axis=-1, keepdims=True)
            out = out * lax.rsqrt(ms + EPS) * fn_ref[...]
        o_ref[...] = out

    first = _first_grid_step()
    pl.when(first)(functools.partial(tile, True))
    pl.when(jnp.logical_not(first))(functools.partial(tile, False))


def _ffn(x, mod, norm_g, w_in, w_out, layer, row0, final_norm=None):
    final = final_norm is not None
    in_specs = [
        pl.BlockSpec((None, FFN_TM, D_MODEL), lambda b, i: (b, i, 0)),
        pl.BlockSpec((None, None, N_MOD, D_MODEL), lambda b, i: (layer, b, 0, 0)),
        pl.BlockSpec((None, 1, D_MODEL), lambda b, i: (layer, 0, 0)),
        pl.BlockSpec(memory_space=pl.ANY),
        pl.BlockSpec(memory_space=pl.ANY),
    ]
    args = [x, mod, norm_g.reshape(DEPTH, 1, D_MODEL), w_in, w_out]
    if final:
        in_specs.append(pl.BlockSpec((1, D_MODEL), lambda b, i: (0, 0)))
        args.append(final_norm.reshape(1, D_MODEL))
    return pl.pallas_call(
        functools.partial(_ffn_kernel, layer=layer, row0=row0, final=final),
        grid=(BATCH, SEQ // FFN_TM),
        in_specs=in_specs,
        out_specs=pl.BlockSpec((None, FFN_TM, D_MODEL), lambda b, i: (b, i, 0)),
        out_shape=jax.ShapeDtypeStruct((BATCH, SEQ, D_MODEL), F32),
        scratch_shapes=[
            pltpu.VMEM((FFN_TM, D_FF), BF16),
            pltpu.VMEM((D_MODEL, 2 * D_FF), BF16),
            pltpu.VMEM((D_FF, D_MODEL), BF16),
            pltpu.VMEM((2, D_MODEL, FFN_TF), F32),
            pltpu.VMEM((2, D_MODEL, FFN_TF), F32),
            pltpu.VMEM((2, FFN_TF, D_MODEL), F32),
            pltpu.SemaphoreType.DMA((3, 2)),
        ],
        compiler_params=_cparams(("arbitrary", "arbitrary")),
        name="ffn_final" if final else "ffn",
    )(*args)


def _rows_by_residue(hf, rows_ref, half_ref):
    tiles = D_MODEL // LANES
    out = {1: hf.astype(BF16)}
    if all(d == 1 for _, d in ATTN_GROUPS):
        return out
    for t in range(tiles):
        rows_ref[t] = hf[:, t * LANES:(t + 1) * LANES]

    def gather(ref, pieces):
        return jnp.concatenate(
            [jnp.concatenate([ref[t, pl.ds(start, count, stride=stride), :]
                              for start, count, stride in pieces], axis=0)
             for t in range(tiles)], axis=1)

    per_inner = PROJ_TM // FREE_STRIDE
    inner = gather(rows_ref, [(b, per_inner, FREE_STRIDE) for b in range(FREE_STRIDE)])
    for _, dilation in ATTN_GROUPS:
        if dilation == 1 or dilation in out:
            continue
        if dilation == FREE_STRIDE:
            out[dilation] = inner.astype(BF16)
            continue
        outer = dilation // FREE_STRIDE
        assert dilation % FREE_STRIDE == 0 and outer <= FREE_STRIDE
        for t in range(tiles):
            half_ref[t] = inner[:, t * LANES:(t + 1) * LANES]
        pieces = [(b * per_inner + a, PROJ_TM // dilation, outer)
                  for a, b in (divmod(r, FREE_STRIDE) for r in range(dilation))]
        out[dilation] = gather(half_ref, pieces).astype(BF16)
    return out


def _inproj_kernel(x_ref, mod_ref, g_ref, w_hbm, qkv0_ref, qkv1_ref, qkv2_ref, u_ref, gate_ref,
                   w_ref, w_stage, w_sem, rows_ref, half_ref, *, layer):
    qkv_refs = (qkv0_ref, qkv1_ref, qkv2_ref)
    n_segs = IN_COLS // GROUP_WIDTH
    order = sorted(range(n_segs),
                   key=lambda seg: ATTN_GROUPS[seg % N_GROUPS][1] if seg < 3 * N_GROUPS else 1)

    def seg_copy(k, slot):
        seg = order[k]
        return pltpu.make_async_copy(
            w_hbm.at[layer, :, seg * GROUP_WIDTH:(seg + 1) * GROUP_WIDTH], w_stage.at[slot],
            w_sem.at[slot])

    def tile(load_weights):
        hf = _norm_mod(x_ref[...], g_ref[...], mod_ref[3:4, :], mod_ref[4:5, :])
        h = _rows_by_residue(hf, rows_ref, half_ref)
        if load_weights:
            seg_copy(0, 0).start()
        for k, seg in enumerate(order):
            cols = slice(seg * GROUP_WIDTH, (seg + 1) * GROUP_WIDTH)
            if load_weights:
                if k + 1 < n_segs:
                    seg_copy(k + 1, (k + 1) % 2).start()
                seg_copy(k, k % 2).wait()
                w_ref[:, cols] = w_stage[k % 2].astype(BF16)
            if seg < 3 * N_GROUPS:
                which, g = divmod(seg, N_GROUPS)
                dilation = ATTN_GROUPS[g][1]
                ys = jnp.dot(h[dilation], w_ref[:, cols], preferred_element_type=F32)
                if which == 0:
                    ys = ys * QSCALE
                n = PROJ_TM // dilation
                for r in range(dilation):
                    qkv_refs[g][r, :, which * GROUP_WIDTH:(which + 1) * GROUP_WIDTH] = (
                        ys[r * n:(r + 1) * n, :].astype(BF16))
            else:
                ys = jnp.dot(h[1], w_ref[:, cols], preferred_element_type=F32)
                if seg == 3 * N_GROUPS:
                    u_ref[...] = ys
                else:
                    g0 = (seg - 3 * N_GROUPS - 1) * GROUP_WIDTH
                    gate_ref[:, g0:g0 + GROUP_WIDTH] = ys.astype(BF16)

    first = _first_grid_step()
    pl.when(first)(functools.partial(tile, True))
    pl.when(jnp.logical_not(first))(functools.partial(tile, False))


def _inproj(x, mod, norm_g, w_in, layer):
    rows = lambda b, i: (b, i, 0)
    qkv_specs, qkv_shapes = [], []
    for _, dilation in ATTN_GROUPS:
        qkv_specs.append(pl.BlockSpec((None, dilation, PROJ_TM // dilation, QKV_WIDTH),
                                      lambda b, i: (b, 0, i, 0)))
        qkv_shapes.append(jax.ShapeDtypeStruct((BATCH, dilation, SEQ // dilation, QKV_WIDTH), BF16))
    tiles = D_MODEL // LANES
    return pl.pallas_call(
        functools.partial(_inproj_kernel, layer=layer),
        grid=(BATCH, SEQ // PROJ_TM),
        in_specs=[
            pl.BlockSpec((None, PROJ_TM, D_MODEL), rows),
            pl.BlockSpec((None, None, N_MOD, D_MODEL), lambda b, i: (layer, b, 0, 0)),
            pl.BlockSpec((None, 1, D_MODEL), lambda b, i: (layer, 0, 0)),
            pl.BlockSpec(memory_space=pl.ANY),
        ],
        out_specs=(*qkv_specs,
                   pl.BlockSpec((None, PROJ_TM, SSM_WIDTH), rows),
                   pl.BlockSpec((None, PROJ_TM, 2 * D_MODEL), rows)),
        out_shape=(*qkv_shapes,
                   jax.ShapeDtypeStruct((BATCH, SEQ, SSM_WIDTH), F32),
                   jax.ShapeDtypeStruct((BATCH, SEQ, 2 * D_MODEL), BF16)),
        scratch_shapes=[
            pltpu.VMEM((D_MODEL, IN_COLS), BF16),
            pltpu.VMEM((2, D_MODEL, GROUP_WIDTH), F32),
            pltpu.SemaphoreType.DMA((2,)),
            pltpu.VMEM((tiles, PROJ_TM, LANES), F32),
            pltpu.VMEM((tiles, PROJ_TM, LANES), F32),
        ],
        compiler_params=_cparams(("arbitrary", "arbitrary")),
        name="inproj",
    )(x, mod, norm_g.reshape(DEPTH, 1, D_MODEL), w_in)


def _t5_bucket(dist):
    max_exact = REL_BUCKETS // 2
    d = np.maximum(dist, max_exact).astype(np.float32)
    large = max_exact + (np.log(d / max_exact) / np.log(REL_MAX_DIST / max_exact)
                         * (REL_BUCKETS - max_exact)).astype(np.int32)
    large = np.minimum(large, REL_BUCKETS - 1)
    return np.where(dist < max_exact, dist, large).astype(np.int32)


def _attn_bias_tables(rel_bias):
    qi = np.arange(BLOCK)[:, None]
    kj = np.arange(2 * BLOCK)[None, :]
    rel = BLOCK + qi - kj
    tabs = []
    for g, (window, dilation) in enumerate(ATTN_GROUPS):
        band = (rel >= 0) & (rel <= window // dilation)
        bucket = _t5_bucket(np.clip(rel, 0, None) * dilation)
        tbl = rel_bias[:, g * HEADS_PER_GROUP:(g + 1) * HEADS_PER_GROUP]
        onehot = jnp.asarray(bucket[None] == np.arange(REL_BUCKETS)[:, None, None], F32)
        bias = jnp.einsum('rqk,rh->hqk', onehot, tbl.astype(F32),
                          precision=lax.Precision.HIGHEST)
        general = jnp.where(band[None], bias * LOG2E, NEG)
        masked = jnp.full((HEADS_PER_GROUP, BLOCK, BLOCK), NEG, F32)
        first = jnp.concatenate([masked, general[:, :, BLOCK:]], axis=2)
        very_first = jnp.concatenate([general[:, :, BLOCK:], masked], axis=2)
        tabs.append(jnp.stack([very_first, first, general]))
    return jnp.stack(tabs).reshape(N_GROUPS, 3, HEAD_PAIRS, 2 * BLOCK, 2 * BLOCK)


def _attn_group(q_ref, k_ref, v_ref, bias_ref, o_ref, part_ref, *, dilation, slot):
    nb = SEQ // dilation // BLOCK
    lane = lax.broadcasted_iota(jnp.int32, (BLOCK, LANES), 1)
    head0 = lane < HEAD_DIM
    keep0 = jnp.where(head0, 1.0, 0.0).astype(BF16)
    keep1 = jnp.where(head0, 0.0, 1.0).astype(BF16)
    ones_cols = jnp.ones((2 * BLOCK, LANES), BF16)
    contract_last = (((1,), (1,)), ((), ()))

    def token_rows(n):
        start = n // nb + (n % nb) * (BLOCK * dilation)
        if dilation == 1:
            start = pl.multiple_of(start, BLOCK)
        return pl.ds(start, BLOCK, stride=dilation)

    def key_rows(n):
        return pl.ds(pl.multiple_of(jnp.maximum(n - 1, 0) * BLOCK, BLOCK), 2 * BLOCK)

    def logits(n):
        qb = q_ref[pl.ds(pl.multiple_of(n * BLOCK, BLOCK), BLOCK), :]
        qq = jnp.concatenate([qb * keep0, qb * keep1], axis=0)
        tab = jnp.where(n == 0, 0, jnp.where(n % nb == 0, 1, 2))
        l = lax.dot_general(qq, k_ref[key_rows(n), :], contract_last,
                            preferred_element_type=F32) + bias_ref[tab]
        return l, jnp.max(l, axis=1, keepdims=True)

    def weighted(n, l, m_rows):
        v_aug = jnp.concatenate([v_ref[key_rows(n), :], ones_cols], axis=1)
        p = jnp.exp2(l - m_rows).astype(BF16)
        r = jnp.dot(p, v_aug, preferred_element_type=F32)
        return (jnp.where(head0, r[:BLOCK, :LANES], r[BLOCK:, :LANES]),
                jnp.where(head0, r[:BLOCK, LANES:], r[BLOCK:, LANES:]))

    def body(step, carry):
        ns = [step * ATTN_UNROLL + u for u in range(ATTN_UNROLL)]
        scores = [logits(n) for n in ns]
        for n, (l, m_rows) in zip(ns, scores):
            acc, den = weighted(n, l, m_rows)
            m_blk = jnp.where(head0, m_rows[:BLOCK], m_rows[BLOCK:])
            rows = token_rows(n)
            if slot is not None:
                part_ref[slot, 0, rows, :] = acc / den
                part_ref[slot, 1, rows, :] = m_blk + jnp.log2(den)
            else:
                others = [(part_ref[g, 0, rows, :], part_ref[g, 1, rows, :])
                          for g in range(N_GROUPS - 1)]
                m_all = functools.reduce(jnp.maximum, [m_blk] + [lse for _, lse in others])
                w_own = jnp.exp2(m_blk - m_all)
                num = w_own * acc
                den_all = w_own * den
                for out_g, lse in others:
                    w = jnp.exp2(lse - m_all)
                    num = num + w * out_g
                    den_all = den_all + w
                o_ref[rows, :] = num / den_all
        return carry

    lax.fori_loop(0, N_QBLOCKS // ATTN_UNROLL, body, 0)


def _attn_kernel(*refs):
    qkv_refs, (bias_ref, o_ref, part_ref) = refs[:3 * N_GROUPS], refs[3 * N_GROUPS:]
    for i, gi in enumerate(ATTN_ORDER):
        q_ref, k_ref, v_ref = qkv_refs[3 * gi:3 * gi + 3]
        _attn_group(q_ref, k_ref, v_ref, bias_ref.at[gi], o_ref, part_ref,
                    dilation=ATTN_GROUPS[gi][1], slot=i if i < N_GROUPS - 1 else None)


def _attention(qkv_groups, bias_tabs):
    in_specs, args = [], []
    for qkv in qkv_groups:
        for which in range(3):
            in_specs.append(pl.BlockSpec(
                (None, SEQ, LANES), lambda b, hp, which=which: (b, 0, which * HEAD_PAIRS + hp)))
            args.append(qkv)
    in_specs.append(pl.BlockSpec((N_GROUPS, 3, None, 2 * BLOCK, 2 * BLOCK),
                                 lambda b, hp: (0, 0, hp, 0, 0)))
    return pl.pallas_call(
        _attn_kernel,
        grid=(BATCH, HEAD_PAIRS),
        in_specs=in_specs,
        out_specs=pl.BlockSpec((None, SEQ, LANES), lambda b, hp: (b, 0, hp)),
        out_shape=jax.ShapeDtypeStruct((BATCH, SEQ, GROUP_WIDTH), F32),
        scratch_shapes=[pltpu.VMEM((N_GROUPS - 1, 2, SEQ, LANES), F32)],
        compiler_params=_cparams(("arbitrary", "arbitrary")),
        name="dilated_attn",
    )(*args, bias_tabs)


def _ssm_param_kernel(lre_ref, lim_ref, ldt_ref, are_ref, aim_ref, fre_ref, fim_ref):
    lam_re = lre_ref[...]
    lam_im = lim_ref[...]
    dt = jnp.exp(ldt_ref[...])
    mag = jnp.exp(lam_re * dt)
    ang = lam_im * dt
    a_re = mag * jnp.cos(ang)
    a_im = mag * jnp.sin(ang)
    den = lam_re * lam_re + lam_im * lam_im
    are_ref[...] = a_re
    aim_ref[...] = a_im
    fre_ref[...] = ((a_re - 1) * lam_re + a_im * lam_im) / den
    fim_ref[...] = (a_im * lam_re - (a_re - 1) * lam_im) / den


def _ssm_params(lam_re, lam_im, log_dt):
    n = DEPTH * SSM_GROUPS
    shp = jax.ShapeDtypeStruct((n, SSM_STATE), F32)
    return pl.pallas_call(
        _ssm_param_kernel, out_shape=(shp, shp, shp, shp), name="ssm_discretise",
    )(lam_re.reshape(n, SSM_STATE), lam_im.reshape(n, SSM_STATE), log_dt.reshape(n, 1))


def _ssm_kernel(u_ref, bre_ref, bim_ref, cre_ref, cim_ref, are_ref, aim_ref, d_ref,
                y_ref, sre_ref, sim_ref, xre_ref, xim_ref):
    c = pl.program_id(1)

    @pl.when(c == 0)
    def _():
        xre_ref[...] = jnp.zeros_like(xre_ref)
        xim_ref[...] = jnp.zeros_like(xim_ref)

    def plane_rows(t, b):
        half, pair = divmod(t, HALF_TILES)
        r0 = (half * BATCH + b) * SSM_PITCH
        return pair, slice(r0, r0 + SSM_TC)

    for b in range(BATCH):
        ub = u_ref[b].astype(BF16)
        bu_re = jnp.dot(ub, bre_ref[...], preferred_element_type=F32)
        bu_im = jnp.dot(ub, bim_ref[...], preferred_element_type=F32)
        for t in range(SLAB_TILES):
            pair, sl = plane_rows(t, b)
            sre_ref[pair, sl, :] = bu_re[:, t * LANES:(t + 1) * LANES]
            sim_ref[pair, sl, :] = bu_im[:, t * LANES:(t + 1) * LANES]

    def coeff(ref, pair):
        return jnp.concatenate(
            [jnp.broadcast_to(ref[half * HALF_TILES + pair:half * HALF_TILES + pair + 1, :],
                              (BATCH, LANES)) for half in range(2)], axis=0)

    a_re = [coeff(are_ref, p) for p in range(HALF_TILES)]
    a_im = [coeff(aim_ref, p) for p in range(HALF_TILES)]

    def step(i, carry):
        xs = list(carry)
        rows = pl.ds(i, SUBLANES, stride=SSM_PITCH)
        for p in range(HALF_TILES):
            xr, xi = xs[2 * p], xs[2 * p + 1]
            nr = a_re[p] * xr - a_im[p] * xi + sre_ref[p, rows, :]
            ni = a_re[p] * xi + a_im[p] * xr + sim_ref[p, rows, :]
            sre_ref[p, rows, :] = nr
            sim_ref[p, rows, :] = ni
            xs[2 * p], xs[2 * p + 1] = nr, ni
        return tuple(xs)

    init = []
    for p in range(HALF_TILES):
        init += [xre_ref[p], xim_ref[p]]
    fin = lax.fori_loop(0, SSM_TC, step, tuple(init), unroll=8)
    for p in range(HALF_TILES):
        xre_ref[p] = fin[2 * p]
        xim_ref[p] = fin[2 * p + 1]

    for b in range(BATCH):
        tiles = [plane_rows(t, b) for t in range(SLAB_TILES)]
        x_re = jnp.concatenate([sre_ref[pair, sl, :] for pair, sl in tiles], axis=1)
        x_im = jnp.concatenate([sim_ref[pair, sl, :] for pair, sl in tiles], axis=1)
        y = (_bdot(x_re, cre_ref[...]) + _bdot(x_im, cim_ref[...])
             + d_ref[...] * u_ref[b])
        y_ref[b] = jax.nn.gelu(y).astype(BF16)


def _ssm(u, bb_re, bb_im, cc_re, cc_im, a_re, a_im, d_skip, layer):
    slab = lambda s, c: (layer * SSM_SLABS + s, 0, 0)
    return pl.pallas_call(
        _ssm_kernel,
        grid=(SSM_SLABS, SEQ // SSM_TC),
        in_specs=[
            pl.BlockSpec((BATCH, SSM_TC, SLAB_CH), lambda s, c: (0, c, s)),
            pl.BlockSpec((None, SLAB_CH, SLAB_STATES), slab),
            pl.BlockSpec((None, SLAB_CH, SLAB_STATES), slab),
            pl.BlockSpec((None, SLAB_STATES, SLAB_CH), slab),
            pl.BlockSpec((None, SLAB_STATES, SLAB_CH), slab),
            pl.BlockSpec((None, SLAB_TILES, LANES), slab),
            pl.BlockSpec((None, SLAB_TILES, LANES), slab),
            pl.BlockSpec((None, 1, SLAB_CH), slab),
        ],
        out_specs=pl.BlockSpec((BATCH, SSM_TC, SLAB_CH), lambda s, c: (0, c, s)),
        out_shape=jax.ShapeDtypeStruct((BATCH, SEQ, SSM_WIDTH), BF16),
        scratch_shapes=[
            pltpu.VMEM((HALF_TILES, SUBLANES * SSM_PITCH, LANES), F32),
            pltpu.VMEM((HALF_TILES, SUBLANES * SSM_PITCH, LANES), F32),
            pltpu.VMEM((HALF_TILES, SUBLANES, LANES), F32),
            pltpu.VMEM((HALF_TILES, SUBLANES, LANES), F32),
        ],
        compiler_params=_cparams(("arbitrary", "arbitrary")),
        name="s5_scan",
    )(u, bb_re, bb_im, cc_re, cc_im, a_re, a_im, d_skip)


def _ssm_matrices(f_re, f_im, b_re, b_im, c_re, c_im):
    n = DEPTH * SSM_GROUPS
    f_re, f_im = f_re[..., None], f_im[..., None]
    b_re = b_re.reshape(n, SSM_STATE, SSM_GROUP)
    b_im = b_im.reshape(n, SSM_STATE, SSM_GROUP)
    bb_re = f_re * b_re - f_im * b_im
    bb_im = f_re * b_im + f_im * b_re
    gps = SSM_GROUPS // SSM_SLABS
    slabs = DEPTH * SSM_SLABS

    def block_diag(blocks):
        _, _, r, c = blocks.shape
        tiled = jnp.tile(blocks.reshape(slabs, gps * r, c), (1, 1, gps))
        on_diag = (np.arange(gps * r)[:, None] // r) == (np.arange(gps * c)[None, :] // c)
        return jnp.where(jnp.asarray(on_diag), tiled, 0.0)

    def in_map(bb):
        bb = bb.reshape(slabs, gps, SSM_STATE, SSM_GROUP)
        return block_diag(jnp.swapaxes(bb, 2, 3))

    def out_map(cc):
        cc = cc.reshape(slabs, gps, SSM_GROUP, SSM_STATE)
        return block_diag(jnp.swapaxes(cc, 2, 3))

    return (in_map(bb_re).astype(BF16), in_map(bb_im).astype(BF16),
            out_map(c_re).astype(BF16), out_map(-c_im).astype(BF16))


def _mix_kernel(x_ref, mod_ref, o_ref_in, ys_ref, ga_ref, gs_ref, wap_hbm, wglu_hbm, wout_hbm,
                out_ref, wap_ref, wap_stage, wap_sem, wglu_ref, wglu_stage, wglu_sem,
                wout_ref, wout_stage, wout_sem, *, layer):
    @pl.when(_first_grid_step())
    def _():
        _load_weight(wap_hbm, layer, wap_ref, wap_stage, wap_sem)
        _load_weight(wglu_hbm, layer, wglu_ref, wglu_stage, wglu_sem)
        _load_weight(wout_hbm, layer, wout_ref, wout_stage, wout_sem)

    y_attn = _bdot(o_ref_in[...], wap_ref[...])
    gl = jnp.dot(ys_ref[...], wglu_ref[...], preferred_element_type=F32)
    y_ssm = gl[:, :D_MODEL] * _sigmoid(gl[:, D_MODEL:])
    mixed = (_sigmoid(ga_ref[...].astype(F32)) * y_attn
             + _sigmoid(gs_ref[...].astype(F32)) * y_ssm)
    out_ref[...] = x_ref[...] + mod_ref[5:6, :] * _bdot(mixed, wout_ref[...])


def _mix(x, mod, attn_o, y_ssm, gates, w_ap, w_glu, w_out, layer):
    rows = lambda b, i: (b, i, 0)
    return pl.pallas_call(
        functools.partial(_mix_kernel, layer=layer),
        grid=(BATCH, SEQ // MIX_TM),
        in_specs=[
            pl.BlockSpec((None, MIX_TM, D_MODEL), rows),
            pl.BlockSpec((None, None, N_MOD, D_MODEL), lambda b, i: (layer, b, 0, 0)),
            pl.BlockSpec((None, MIX_TM, GROUP_WIDTH), rows),
            pl.BlockSpec((None, MIX_TM, SSM_WIDTH), rows),
            pl.BlockSpec((None, MIX_TM, D_MODEL), rows),
            pl.BlockSpec((None, MIX_TM, D_MODEL), lambda b, i: (b, i, 1)),
            pl.BlockSpec(memory_space=pl.ANY),
            pl.BlockSpec(memory_space=pl.ANY),
            pl.BlockSpec(memory_space=pl.ANY),
        ],
        out_specs=pl.BlockSpec((None, MIX_TM, D_MODEL), rows),
        out_shape=jax.ShapeDtypeStruct((BATCH, SEQ, D_MODEL), F32),
        scratch_shapes=(_weight_scratch(GROUP_WIDTH, D_MODEL, WEIGHT_CHUNKS)
                        + _weight_scratch(SSM_WIDTH, 2 * D_MODEL, WEIGHT_CHUNKS)
                        + _weight_scratch(D_MODEL, D_MODEL, WEIGHT_CHUNKS)),
        compiler_params=_cparams(("arbitrary", "arbitrary")),
        name="mix_out",
    )(x, mod, attn_o, y_ssm, gates, gates, w_ap, w_glu, w_out)


def kernel(x, c, w_ada, b_ada, norm_ffn1, w_ffn1_in, w_ffn1_out, norm_mix, w_in, rel_bias, lam_re, lam_im, log_dt, b_re, b_im, c_re, c_im, d_skip, w_glu, w_attn_proj, w_out, norm_ffn2, w_ffn2_in, w_ffn2_out, final_norm):
    c_pad = jnp.zeros((SUBLANES, D_MODEL), F32).at[:BATCH].set(c)
    mod = _ada(c_pad, w_ada, b_ada).reshape(DEPTH, SUBLANES, N_MOD, D_MODEL)
    bias_tabs = _attn_bias_tables(rel_bias)
    a_re, a_im, f_re, f_im = _ssm_params(lam_re, lam_im, log_dt)
    bb_re, bb_im, cc_re, cc_im = _ssm_matrices(f_re, f_im, b_re, b_im, c_re, c_im)
    a_re = a_re.reshape(DEPTH * SSM_SLABS, SLAB_TILES, LANES)
    a_im = a_im.reshape(DEPTH * SSM_SLABS, SLAB_TILES, LANES)
    d_skip = d_skip.reshape(DEPTH * SSM_SLABS, 1, SLAB_CH)
    for l in range(DEPTH):
        x = _ffn(x, mod, norm_ffn1, w_ffn1_in, w_ffn1_out, l, 0)

        *qkv_groups, u, gates = _inproj(x, mod, norm_mix, w_in, l)
        attn_o = _attention([qkv.reshape(BATCH, SEQ, QKV_WIDTH) for qkv in qkv_groups], bias_tabs)
        y_ssm = _ssm(u, bb_re, bb_im, cc_re, cc_im, a_re, a_im, d_skip, l)
        x = _mix(x, mod, attn_o, y_ssm, gates, w_attn_proj, w_glu, w_out, l)

        x = _ffn(x, mod, norm_ffn2, w_ffn2_in, w_ffn2_out, l, 6,
                 final_norm=final_norm if l == DEPTH - 1 else None)
    return x
```

```python
import functools
import math

import jax
import jax.numpy as jnp
import numpy as np
from jax import lax
from jax.experimental import pallas as pl
from jax.experimental.pallas import tpu as pltpu

D_MODEL = 1024
BATCH = 4
SEQ = 4096
DEPTH = 2
HEAD_DIM = 64
HEADS_PER_GROUP = 8
ATTN_GROUPS = ((128, 1), (512, 4), (2048, 16))
N_GROUPS = len(ATTN_GROUPS)
GROUP_WIDTH = HEADS_PER_GROUP * HEAD_DIM
BLOCK = 128
REL_BUCKETS = 32
REL_MAX_DIST = 2048
NEG = -1e30
SSM_WIDTH = 512
SSM_GROUP = 16
SSM_GROUPS = 32
SSM_STATE = 64
D_FF = 2816
QKV_COLS = 3 * N_GROUPS * GROUP_WIDTH
IN_COLS = QKV_COLS + SSM_WIDTH + 2 * D_MODEL
N_MOD = 9
EPS = 1e-6

LANES = 128
SUBLANES = 8
MXU_DIM = 256
FREE_STRIDE = 4
VMEM_LIMIT = 56 * 1024 * 1024

ADA_TN = 1152
FFN_TM = 1024
FFN_TF = MXU_DIM
PROJ_TM = 512
QKV_WIDTH = 3 * GROUP_WIDTH
FFN_MIX_TM = 512
WEIGHT_CHUNKS = 8
HEAD_PAIRS = GROUP_WIDTH // LANES
ATTN_UNROLL = 16
ATTN_ORDER = (2, 1, 0)
LOG2E = math.log2(math.e)
QSCALE = HEAD_DIM ** -0.5 * LOG2E
N_QBLOCKS = SEQ // BLOCK
SSM_TC = 512
SSM_PITCH = SSM_TC + SUBLANES
SLAB_CH = MXU_DIM
SSM_SLABS = SSM_WIDTH // SLAB_CH
SLAB_STATES = (SSM_GROUPS // SSM_SLABS) * SSM_STATE
SLAB_TILES = SLAB_STATES // LANES
HALF_TILES = SLAB_TILES * BATCH // SUBLANES

BF16 = jnp.bfloat16
F32 = jnp.float32


def _cparams(sem):
    return pltpu.CompilerParams(dimension_semantics=sem, vmem_limit_bytes=VMEM_LIMIT)


def _first_grid_step():
    return (pl.program_id(0) == 0) & (pl.program_id(1) == 0)


def _weight_scratch(k, n, chunks):
    return [pltpu.VMEM((k, n), BF16), pltpu.VMEM((2, k // chunks, n), F32),
            pltpu.SemaphoreType.DMA((2,))]


def _load_weight(w_hbm, layer, w_ref, stage_ref, sem):
    rows = stage_ref.shape[1]
    n_chunks = w_ref.shape[0] // rows

    def chunk(i, slot):
        return pltpu.make_async_copy(w_hbm.at[layer, pl.ds(i * rows, rows), :],
                                     stage_ref.at[slot], sem.at[slot])

    chunk(0, 0).start()

    def body(i, carry):
        slot = i % 2

        @pl.when(i + 1 < n_chunks)
        def _():
            chunk(i + 1, 1 - slot).start()

        chunk(i, slot).wait()
        w_ref[pl.ds(pl.multiple_of(i * rows, rows), rows), :] = stage_ref[slot].astype(BF16)
        return carry

    lax.fori_loop(0, n_chunks, body, 0)


def _bdot(a, b):
    return jnp.dot(a.astype(BF16), b.astype(BF16), preferred_element_type=F32)


def _sigmoid(x):
    return 0.5 * jnp.tanh(0.5 * x) + 0.5


def _norm_mod(x, g, shift, scale):
    ms = jnp.mean(x * x, axis=-1, keepdims=True)
    y = x * lax.rsqrt(ms + EPS) * g
    return y * (1.0 + scale) + shift


def _ada_kernel(c_ref, w_ref, b_ref, o_ref):
    c = c_ref[...]
    ca = c * _sigmoid(c)
    o_ref[...] = _bdot(ca, w_ref[...]) + b_ref[...]


def _ada(c_pad, w_ada, b_ada):
    n = N_MOD * D_MODEL
    return pl.pallas_call(
        _ada_kernel,
        grid=(DEPTH, n // ADA_TN),
        in_specs=[
            pl.BlockSpec((SUBLANES, D_MODEL), lambda l, j: (0, 0)),
            pl.BlockSpec((None, D_MODEL, ADA_TN), lambda l, j: (l, 0, j)),
            pl.BlockSpec((None, 1, ADA_TN), lambda l, j: (l, 0, j)),
        ],
        out_specs=pl.BlockSpec((None, SUBLANES, ADA_TN), lambda l, j: (l, 0, j)),
        out_shape=jax.ShapeDtypeStruct((DEPTH, SUBLANES, n), F32),
        compiler_params=_cparams(("arbitrary", "arbitrary")),
        name="ada_mod",
    )(c_pad, w_ada, b_ada.reshape(DEPTH, 1, n))


def _ffn_kernel(x_ref, mod_ref, g_ref, win_hbm, wout_hbm, *rest, layer, row0, final, mixer):
    if mixer:
        (ao_ref, ys_ref, ga_ref, gs_ref, wap_hbm, wglu_hbm, wmix_hbm), rest = rest[:7], rest[7:]
    if final:
        fn_ref, rest = rest[0], rest[1:]
    o_ref, act_ref, win_ref, wout_ref, a_stage, b_stage, o_stage, sem = rest[:8]
    if mixer:
        (wap_ref, wap_stage, wap_sem, wglu_ref, wglu_stage, wglu_sem,
         wmix_ref, wmix_stage, wmix_sem) = rest[8:]
    n_chunks = D_FF // FFN_TF

    def chunk_copies(j, slot):
        lo, hi = j * FFN_TF, (j + 1) * FFN_TF
        return (pltpu.make_async_copy(win_hbm.at[layer, :, lo:hi], a_stage.at[slot], sem.at[0, slot]),
                pltpu.make_async_copy(win_hbm.at[layer, :, D_FF + lo:D_FF + hi], b_stage.at[slot],
                                      sem.at[1, slot]),
                pltpu.make_async_copy(wout_hbm.at[layer, lo:hi, :], o_stage.at[slot], sem.at[2, slot]))

    def tile(load_weights):
        x = x_ref[...]
        if load_weights:
            for copy in chunk_copies(0, 0):
                copy.start()
        if mixer:
            if load_weights:
                _load_weight(wap_hbm, layer, wap_ref, wap_stage, wap_sem)
                _load_weight(wglu_hbm, layer, wglu_ref, wglu_stage, wglu_sem)
                _load_weight(wmix_hbm, layer, wmix_ref, wmix_stage, wmix_sem)
            y_attn = _bdot(ao_ref[...], wap_ref[...])
            gl = jnp.dot(ys_ref[...], wglu_ref[...], preferred_element_type=F32)
            y_ssm = gl[:, :D_MODEL] * _sigmoid(gl[:, D_MODEL:])
            mixed = (_sigmoid(ga_ref[...].astype(F32)) * y_attn
                     + _sigmoid(gs_ref[...].astype(F32)) * y_ssm)
            x = x + mod_ref[5:6, :] * _bdot(mixed, wmix_ref[...])
        h = _norm_mod(x, g_ref[...], mod_ref[row0:row0 + 1, :], mod_ref[row0 + 1:row0 + 2, :])
        hb = h.astype(BF16)
        for j in range(n_chunks):
            lo, hi = j * FFN_TF, (j + 1) * FFN_TF
            if load_weights:
                if j + 1 < n_chunks:
                    for copy in chunk_copies(j + 1, (j + 1) % 2):
                        copy.start()
                for copy in chunk_copies(j, j % 2):
                    copy.wait()
                win_ref[:, lo:hi] = a_stage[j % 2].astype(BF16)
                win_ref[:, D_FF + lo:D_FF + hi] = b_stage[j % 2].astype(BF16)
                wout_ref[lo:hi, :] = o_stage[j % 2].astype(BF16)
            a = jnp.dot(hb, win_ref[:, lo:hi], preferred_element_type=F32)
            b = jnp.dot(hb, win_ref[:, D_FF + lo:D_FF + hi], preferred_element_type=F32)
            act_ref[:, lo:hi] = (a * _sigmoid(a) * b).astype(BF16)
        y = jnp.dot(act_ref[...], wout_ref[...], preferred_element_type=F32)
        out = x + (0.5 * mod_ref[row0 + 2:row0 + 3, :]) * y
        if final:
            ms = jnp.mean(out * out, axis=-1, keepdims=True)
            out = out * lax.rsqrt(ms + EPS) * fn_ref[...]
        o_ref[...] = out

    first = _first_grid_step()
    pl.when(first)(functools.partial(tile, True))
    pl.when(jnp.logical_not(first))(functools.partial(tile, False))


def _ffn(x, mod, norm_g, w_in, w_out, layer, row0, final_norm=None, mixer=None):
    final = final_norm is not None
    tm = FFN_MIX_TM if mixer else FFN_TM
    rows = lambda b, i: (b, i, 0)
    in_specs = [
        pl.BlockSpec((None, tm, D_MODEL), rows),
        pl.BlockSpec((None, None, N_MOD, D_MODEL), lambda b, i: (layer, b, 0, 0)),
        pl.BlockSpec((None, 1, D_MODEL), lambda b, i: (layer, 0, 0)),
        pl.BlockSpec(memory_space=pl.ANY),
        pl.BlockSpec(memory_space=pl.ANY),
    ]
    args = [x, mod, norm_g.reshape(DEPTH, 1, D_MODEL), w_in, w_out]
    scratch = [
        pltpu.VMEM((tm, D_FF), BF16),
        pltpu.VMEM((D_MODEL, 2 * D_FF), BF16),
        pltpu.VMEM((D_FF, D_MODEL), BF16),
        pltpu.VMEM((2, D_MODEL, FFN_TF), F32),
        pltpu.VMEM((2, D_MODEL, FFN_TF), F32),
        pltpu.VMEM((2, FFN_TF, D_MODEL), F32),
        pltpu.SemaphoreType.DMA((3, 2)),
    ]
    if mixer:
        attn_o, y_ssm, gates, w_ap, w_glu, w_mix = mixer
        in_specs += [
            pl.BlockSpec((None, tm, GROUP_WIDTH), rows),
            pl.BlockSpec((None, tm, SSM_WIDTH), rows),
            pl.BlockSpec((None, tm, D_MODEL), rows),
            pl.BlockSpec((None, tm, D_MODEL), lambda b, i: (b, i, 1)),
            pl.BlockSpec(memory_space=pl.ANY),
            pl.BlockSpec(memory_space=pl.ANY),
            pl.BlockSpec(memory_space=pl.ANY),
        ]
        args += [attn_o, y_ssm, gates, gates, w_ap, w_glu, w_mix]
        scratch += (_weight_scratch(GROUP_WIDTH, D_MODEL, WEIGHT_CHUNKS)
                    + _weight_scratch(SSM_WIDTH, 2 * D_MODEL, WEIGHT_CHUNKS)
                    + _weight_scratch(D_MODEL, D_MODEL, WEIGHT_CHUNKS))
    if final:
        in_specs.append(pl.BlockSpec((1, D_MODEL), lambda b, i: (0, 0)))
        args.append(final_norm.reshape(1, D_MODEL))
    return pl.pallas_call(
        functools.partial(_ffn_kernel, layer=layer, row0=row0, final=final, mixer=bool(mixer)),
        grid=(BATCH, SEQ // tm),
        in_specs=in_specs,
        out_specs=pl.BlockSpec((None, tm, D_MODEL), rows),
        out_shape=jax.ShapeDtypeStruct((BATCH, SEQ, D_MODEL), F32),
        scratch_shapes=scratch,
        compiler_params=_cparams(("arbitrary", "arbitrary")),
        name=("mix_" if mixer else "") + ("ffn_final" if final else "ffn"),
    )(*args)


def _rows_by_residue(hf, rows_ref, half_ref):
    tiles = D_MODEL // LANES
    out = {1: hf.astype(BF16)}
    if all(d == 1 for _, d in ATTN_GROUPS):
        return out
    for t in range(tiles):
        rows_ref[t] = hf[:, t * LANES:(t + 1) * LANES]

    def gather(ref, pieces):
        return jnp.concatenate(
            [jnp.concatenate([ref[t, pl.ds(start, count, stride=stride), :]
                              for start, count, stride in pieces], axis=0)
             for t in range(tiles)], axis=1)

    per_inner = PROJ_TM // FREE_STRIDE
    inner = gather(rows_ref, [(b, per_inner, FREE_STRIDE) for b in range(FREE_STRIDE)])
    for _, dilation in ATTN_GROUPS:
        if dilation == 1 or dilation in out:
            continue
        if dilation == FREE_STRIDE:
            out[dilation] = inner.astype(BF16)
            continue
        outer = dilation // FREE_STRIDE
        assert dilation % FREE_STRIDE == 0 and outer <= FREE_STRIDE
        for t in range(tiles):
            half_ref[t] = inner[:, t * LANES:(t + 1) * LANES]
        pieces = [(b * per_inner + a, PROJ_TM // dilation, outer)
                  for a, b in (divmod(r, FREE_STRIDE) for r in range(dilation))]
        out[dilation] = gather(half_ref, pieces).astype(BF16)
    return out


def _inproj_kernel(x_ref, mod_ref, g_ref, w_hbm, qkv0_ref, qkv1_ref, qkv2_ref, u_ref, gate_ref,
                   w_ref, w_stage, w_sem, rows_ref, half_ref, *, layer):
    qkv_refs = (qkv0_ref, qkv1_ref, qkv2_ref)
    n_segs = IN_COLS // GROUP_WIDTH
    order = sorted(range(n_segs),
                   key=lambda seg: ATTN_GROUPS[seg % N_GROUPS][1] if seg < 3 * N_GROUPS else 1)

    def seg_copy(k, slot):
        seg = order[k]
        return pltpu.make_async_copy(
            w_hbm.at[layer, :, seg * GROUP_WIDTH:(seg + 1) * GROUP_WIDTH], w_stage.at[slot],
            w_sem.at[slot])

    def tile(load_weights):
        hf = _norm_mod(x_ref[...], g_ref[...], mod_ref[3:4, :], mod_ref[4:5, :])
        h = _rows_by_residue(hf, rows_ref, half_ref)
        if load_weights:
            seg_copy(0, 0).start()
        for k, seg in enumerate(order):
            cols = slice(seg * GROUP_WIDTH, (seg + 1) * GROUP_WIDTH)
            if load_weights:
                if k + 1 < n_segs:
                    seg_copy(k + 1, (k + 1) % 2).start()
                seg_copy(k, k % 2).wait()
                w_ref[:, cols] = w_stage[k % 2].astype(BF16)
            if seg < 3 * N_GROUPS:
                which, g = divmod(seg, N_GROUPS)
                dilation = ATTN_GROUPS[g][1]
                ys = jnp.dot(h[dilation], w_ref[:, cols], preferred_element_type=F32)
                if which == 0:
                    ys = ys * QSCALE
                n = PROJ_TM // dilation
                for r in range(dilation):
                    qkv_refs[g][r, :, which * GROUP_WIDTH:(which + 1) * GROUP_WIDTH] = (
                        ys[r * n:(r + 1) * n, :].astype(BF16))
            else:
                ys = jnp.dot(h[1], w_ref[:, cols], preferred_element_type=F32)
                if seg == 3 * N_GROUPS:
                    u_ref[...] = ys
                else:
                    g0 = (seg - 3 * N_GROUPS - 1) * GROUP_WIDTH
                    gate_ref[:, g0:g0 + GROUP_WIDTH] = ys.astype(BF16)

    first = _first_grid_step()
    pl.when(first)(functools.partial(tile, True))
    pl.when(jnp.logical_not(first))(functools.partial(tile, False))


def _inproj(x, mod, norm_g, w_in, layer):
    rows = lambda b, i: (b, i, 0)
    qkv_specs, qkv_shapes = [], []
    for _, dilation in ATTN_GROUPS:
        qkv_specs.append(pl.BlockSpec((None, dilation, PROJ_TM // dilation, QKV_WIDTH),
                                      lambda b, i: (b, 0, i, 0)))
        qkv_shapes.append(jax.ShapeDtypeStruct((BATCH, dilation, SEQ // dilation, QKV_WIDTH), BF16))
    tiles = D_MODEL // LANES
    return pl.pallas_call(
        functools.partial(_inproj_kernel, layer=layer),
        grid=(BATCH, SEQ // PROJ_TM),
        in_specs=[
            pl.BlockSpec((None, PROJ_TM, D_MODEL), rows),
            pl.BlockSpec((None, None, N_MOD, D_MODEL), lambda b, i: (layer, b, 0, 0)),
            pl.BlockSpec((None, 1, D_MODEL), lambda b, i: (layer, 0, 0)),
            pl.BlockSpec(memory_space=pl.ANY),
        ],
        out_specs=(*qkv_specs,
                   pl.BlockSpec((None, PROJ_TM, SSM_WIDTH), rows),
                   pl.BlockSpec((None, PROJ_TM, 2 * D_MODEL), rows)),
        out_shape=(*qkv_shapes,
                   jax.ShapeDtypeStruct((BATCH, SEQ, SSM_WIDTH), F32),
                   jax.ShapeDtypeStruct((BATCH, SEQ, 2 * D_MODEL), BF16)),
        scratch_shapes=[
            pltpu.VMEM((D_MODEL, IN_COLS), BF16),
            pltpu.VMEM((2, D_MODEL, GROUP_WIDTH), F32),
            pltpu.SemaphoreType.DMA((2,)),
            pltpu.VMEM((tiles, PROJ_TM, LANES), F32),
            pltpu.VMEM((tiles, PROJ_TM, LANES), F32),
        ],
        compiler_params=_cparams(("arbitrary", "arbitrary")),
        name="inproj",
    )(x, mod, norm_g.reshape(DEPTH, 1, D_MODEL), w_in)


def _t5_bucket(dist):
    max_exact = REL_BUCKETS // 2
    d = np.maximum(dist, max_exact).astype(np.float32)
    large = max_exact + (np.log(d / max_exact) / np.log(REL_MAX_DIST / max_exact)
                         * (REL_BUCKETS - max_exact)).astype(np.int32)
    large = np.minimum(large, REL_BUCKETS - 1)
    return np.where(dist < max_exact, dist, large).astype(np.int32)


def _attn_bias_tables(rel_bias):
    qi = np.arange(BLOCK)[:, None]
    kj = np.arange(2 * BLOCK)[None, :]
    rel = BLOCK + qi - kj
    tabs = []
    for g, (window, dilation) in enumerate(ATTN_GROUPS):
        band = (rel >= 0) & (rel <= window // dilation)
        bucket = _t5_bucket(np.clip(rel, 0, None) * dilation)
        tbl = rel_bias[:, g * HEADS_PER_GROUP:(g + 1) * HEADS_PER_GROUP]
        onehot = jnp.asarray(bucket[None] == np.arange(REL_BUCKETS)[:, None, None], F32)
        bias = jnp.einsum('rqk,rh->hqk', onehot, tbl.astype(F32),
                          precision=lax.Precision.HIGHEST)
        general = jnp.where(band[None], bias * LOG2E, NEG)
        masked = jnp.full((HEADS_PER_GROUP, BLOCK, BLOCK), NEG, F32)
        first = jnp.concatenate([masked, general[:, :, BLOCK:]], axis=2)
        very_first = jnp.concatenate([general[:, :, BLOCK:], masked], axis=2)
        tabs.append(jnp.stack([very_first, first, general]))
    return jnp.stack(tabs).reshape(N_GROUPS, 3, HEAD_PAIRS, 2 * BLOCK, 2 * BLOCK)


def _attn_group(q_ref, k_ref, v_ref, bias_ref, o_ref, part_ref, *, dilation, slot):
    nb = SEQ // dilation // BLOCK
    lane = lax.broadcasted_iota(jnp.int32, (BLOCK, LANES), 1)
    head0 = lane < HEAD_DIM
    keep0 = jnp.where(head0, 1.0, 0.0).astype(BF16)
    keep1 = jnp.where(head0, 0.0, 1.0).astype(BF16)
    ones_cols = jnp.ones((2 * BLOCK, LANES), BF16)
    contract_last = (((1,), (1,)), ((), ()))

    def token_rows(n):
        start = n // nb + (n % nb) * (BLOCK * dilation)
        if dilation == 1:
            start = pl.multiple_of(start, BLOCK)
        return pl.ds(start, BLOCK, stride=dilation)

    def key_rows(n):
        return pl.ds(pl.multiple_of(jnp.maximum(n - 1, 0) * BLOCK, BLOCK), 2 * BLOCK)

    def logits(n):
        qb = q_ref[pl.ds(pl.multiple_of(n * BLOCK, BLOCK), BLOCK), :]
        qq = jnp.concatenate([qb * keep0, qb * keep1], axis=0)
        tab = jnp.where(n == 0, 0, jnp.where(n % nb == 0, 1, 2))
        l = lax.dot_general(qq, k_ref[key_rows(n), :], contract_last,
                            preferred_element_type=F32) + bias_ref[tab]
        return l, jnp.max(l, axis=1, keepdims=True)

    def weighted(n, l, m_rows):
        v_aug = jnp.concatenate([v_ref[key_rows(n), :], ones_cols], axis=1)
        p = jnp.exp2(l - m_rows).astype(BF16)
        r = jnp.dot(p, v_aug, preferred_element_type=F32)
        return (jnp.where(head0, r[:BLOCK, :LANES], r[BLOCK:, :LANES]),
                jnp.where(head0, r[:BLOCK, LANES:], r[BLOCK:, LANES:]))

    def body(step, carry):
        ns = [step * ATTN_UNROLL + u for u in range(ATTN_UNROLL)]
        scores = [logits(n) for n in ns]
        for n, (l, m_rows) in zip(ns, scores):
            acc, den = weighted(n, l, m_rows)
            m_blk = jnp.where(head0, m_rows[:BLOCK], m_rows[BLOCK:])
            rows = token_rows(n)
            if slot is not None:
                part_ref[slot, 0, rows, :] = acc / den
                part_ref[slot, 1, rows, :] = m_blk + jnp.log2(den)
            else:
                others = [(part_ref[g, 0, rows, :], part_ref[g, 1, rows, :])
                          for g in range(N_GROUPS - 1)]
                m_all = functools.reduce(jnp.maximum, [m_blk] + [lse for _, lse in others])
                w_own = jnp.exp2(m_blk - m_all)
                num = w_own * acc
                den_all = w_own * den
                for out_g, lse in others:
                    w = jnp.exp2(lse - m_all)
                    num = num + w * out_g
                    den_all = den_all + w
                o_ref[rows, :] = num / den_all
        return carry

    lax.fori_loop(0, N_QBLOCKS // ATTN_UNROLL, body, 0)


def _attn_kernel(*refs):
    qkv_refs, (bias_ref, o_ref, part_ref) = refs[:3 * N_GROUPS], refs[3 * N_GROUPS:]
    for i, gi in enumerate(ATTN_ORDER):
        q_ref, k_ref, v_ref = qkv_refs[3 * gi:3 * gi + 3]
        _attn_group(q_ref, k_ref, v_ref, bias_ref.at[gi], o_ref, part_ref,
                    dilation=ATTN_GROUPS[gi][1], slot=i if i < N_GROUPS - 1 else None)


def _attention(qkv_groups, bias_tabs):
    in_specs, args = [], []
    for qkv in qkv_groups:
        for which in range(3):
            in_specs.append(pl.BlockSpec(
                (None, SEQ, LANES), lambda b, hp, which=which: (b, 0, which * HEAD_PAIRS + hp)))
            args.append(qkv)
    in_specs.append(pl.BlockSpec((N_GROUPS, 3, None, 2 * BLOCK, 2 * BLOCK),
                                 lambda b, hp: (0, 0, hp, 0, 0)))
    return pl.pallas_call(
        _attn_kernel,
        grid=(BATCH, HEAD_PAIRS),
        in_specs=in_specs,
        out_specs=pl.BlockSpec((None, SEQ, LANES), lambda b, hp: (b, 0, hp)),
        out_shape=jax.ShapeDtypeStruct((BATCH, SEQ, GROUP_WIDTH), F32),
        scratch_shapes=[pltpu.VMEM((N_GROUPS - 1, 2, SEQ, LANES), F32)],
        compiler_params=_cparams(("arbitrary", "arbitrary")),
        name="dilated_attn",
    )(*args, bias_tabs)


def _ssm_param_kernel(lre_ref, lim_ref, ldt_ref, are_ref, aim_ref, fre_ref, fim_ref):
    lam_re = lre_ref[...]
    lam_im = lim_ref[...]
    dt = jnp.exp(ldt_ref[...])
    mag = jnp.exp(lam_re * dt)
    ang = lam_im * dt
    a_re = mag * jnp.cos(ang)
    a_im = mag * jnp.sin(ang)
    den = lam_re * lam_re + lam_im * lam_im
    are_ref[...] = a_re
    aim_ref[...] = a_im
    fre_ref[...] = ((a_re - 1) * lam_re + a_im * lam_im) / den
    fim_ref[...] = (a_im * lam_re - (a_re - 1) * lam_im) / den


def _ssm_params(lam_re, lam_im, log_dt):
    n = DEPTH * SSM_GROUPS
    shp = jax.ShapeDtypeStruct((n, SSM_STATE), F32)
    return pl.pallas_call(
        _ssm_param_kernel, out_shape=(shp, shp, shp, shp), name="ssm_discretise",
    )(lam_re.reshape(n, SSM_STATE), lam_im.reshape(n, SSM_STATE), log_dt.reshape(n, 1))


def _ssm_kernel(u_ref, bre_ref, bim_ref, cre_ref, cim_ref, are_ref, aim_ref, d_ref,
                y_ref, sre_ref, sim_ref, xre_ref, xim_ref):
    c = pl.program_id(1)

    @pl.when(c == 0)
    def _():
        xre_ref[...] = jnp.zeros_like(xre_ref)
        xim_ref[...] = jnp.zeros_like(xim_ref)

    def plane_rows(t, b):
        half, pair = divmod(t, HALF_TILES)
        r0 = (half * BATCH + b) * SSM_PITCH
        return pair, slice(r0, r0 + SSM_TC)

    for b in range(BATCH):
        ub = u_ref[b].astype(BF16)
        bu_re = jnp.dot(ub, bre_ref[...], preferred_element_type=F32)
        bu_im = jnp.dot(ub, bim_ref[...], preferred_element_type=F32)
        for t in range(SLAB_TILES):
            pair, sl = plane_rows(t, b)
            sre_ref[pair, sl, :] = bu_re[:, t * LANES:(t + 1) * LANES]
            sim_ref[pair, sl, :] = bu_im[:, t * LANES:(t + 1) * LANES]

    def coeff(ref, pair):
        return jnp.concatenate(
            [jnp.broadcast_to(ref[half * HALF_TILES + pair:half * HALF_TILES + pair + 1, :],
                              (BATCH, LANES)) for half in range(2)], axis=0)

    a_re = [coeff(are_ref, p) for p in range(HALF_TILES)]
    a_im = [coeff(aim_ref, p) for p in range(HALF_TILES)]

    def step(i, carry):
        xs = list(carry)
        rows = pl.ds(i, SUBLANES, stride=SSM_PITCH)
        for p in range(HALF_TILES):
            xr, xi = xs[2 * p], xs[2 * p + 1]
            nr = a_re[p] * xr - a_im[p] * xi + sre_ref[p, rows, :]
            ni = a_re[p] * xi + a_im[p] * xr + sim_ref[p, rows, :]
            sre_ref[p, rows, :] = nr
            sim_ref[p, rows, :] = ni
            xs[2 * p], xs[2 * p + 1] = nr, ni
        return tuple(xs)

    init = []
    for p in range(HALF_TILES):
        init += [xre_ref[p], xim_ref[p]]
    fin = lax.fori_loop(0, SSM_TC, step, tuple(init), unroll=8)
    for p in range(HALF_TILES):
        xre_ref[p] = fin[2 * p]
        xim_ref[p] = fin[2 * p + 1]

    for b in range(BATCH):
        tiles = [plane_rows(t, b) for t in range(SLAB_TILES)]
        x_re = jnp.concatenate([sre_ref[pair, sl, :] for pair, sl in tiles], axis=1)
        x_im = jnp.concatenate([sim_ref[pair, sl, :] for pair, sl in tiles], axis=1)
        y = (_bdot(x_re, cre_ref[...]) + _bdot(x_im, cim_ref[...])
             + d_ref[...] * u_ref[b])
        y_ref[b] = jax.nn.gelu(y).astype(BF16)


def _ssm(u, bb_re, bb_im, cc_re, cc_im, a_re, a_im, d_skip, layer):
    slab = lambda s, c: (layer * SSM_SLABS + s, 0, 0)
    return pl.pallas_call(
        _ssm_kernel,
        grid=(SSM_SLABS, SEQ // SSM_TC),
        in_specs=[
            pl.BlockSpec((BATCH, SSM_TC, SLAB_CH), lambda s, c: (0, c, s)),
            pl.BlockSpec((None, SLAB_CH, SLAB_STATES), slab),
            pl.BlockSpec((None, SLAB_CH, SLAB_STATES), slab),
            pl.BlockSpec((None, SLAB_STATES, SLAB_CH), slab),
            pl.BlockSpec((None, SLAB_STATES, SLAB_CH), slab),
            pl.BlockSpec((None, SLAB_TILES, LANES), slab),
            pl.BlockSpec((None, SLAB_TILES, LANES), slab),
            pl.BlockSpec((None, 1, SLAB_CH), slab),
        ],
        out_specs=pl.BlockSpec((BATCH, SSM_TC, SLAB_CH), lambda s, c: (0, c, s)),
        out_shape=jax.ShapeDtypeStruct((BATCH, SEQ, SSM_WIDTH), BF16),
        scratch_shapes=[
            pltpu.VMEM((HALF_TILES, SUBLANES * SSM_PITCH, LANES), F32),
            pltpu.VMEM((HALF_TILES, SUBLANES * SSM_PITCH, LANES), F32),
            pltpu.VMEM((HALF_TILES, SUBLANES, LANES), F32),
            pltpu.VMEM((HALF_TILES, SUBLANES, LANES), F32),
        ],
        compiler_params=_cparams(("arbitrary", "arbitrary")),
        name="s5_scan",
    )(u, bb_re, bb_im, cc_re, cc_im, a_re, a_im, d_skip)


def _ssm_matrices(f_re, f_im, b_re, b_im, c_re, c_im):
    n = DEPTH * SSM_GROUPS
    f_re, f_im = f_re[..., None], f_im[..., None]
    b_re = b_re.reshape(n, SSM_STATE, SSM_GROUP)
    b_im = b_im.reshape(n, SSM_STATE, SSM_GROUP)
    bb_re = f_re * b_re - f_im * b_im
    bb_im = f_re * b_im + f_im * b_re
    gps = SSM_GROUPS // SSM_SLABS
    slabs = DEPTH * SSM_SLABS

    def block_diag(blocks):
        _, _, r, c = blocks.shape
        tiled = jnp.tile(blocks.reshape(slabs, gps * r, c), (1, 1, gps))
        on_diag = (np.arange(gps * r)[:, None] // r) == (np.arange(gps * c)[None, :] // c)
        return jnp.where(jnp.asarray(on_diag), tiled, 0.0)

    def in_map(bb):
        bb = bb.reshape(slabs, gps, SSM_STATE, SSM_GROUP)
        return block_diag(jnp.swapaxes(bb, 2, 3))

    def out_map(cc):
        cc = cc.reshape(slabs, gps, SSM_GROUP, SSM_STATE)
        return block_diag(jnp.swapaxes(cc, 2, 3))

    return (in_map(bb_re).astype(BF16), in_map(bb_im).astype(BF16),
            out_map(c_re).astype(BF16), out_map(-c_im).astype(BF16))


def kernel(x, c, w_ada, b_ada, norm_ffn1, w_ffn1_in, w_ffn1_out, norm_mix, w_in, rel_bias, lam_re, lam_im, log_dt, b_re, b_im, c_re, c_im, d_skip, w_glu, w_attn_proj, w_out, norm_ffn2, w_ffn2_in, w_ffn2_out, final_norm):
    c_pad = jnp.zeros((SUBLANES, D_MODEL), F32).at[:BATCH].set(c)
    mod = _ada(c_pad, w_ada, b_ada).reshape(DEPTH, SUBLANES, N_MOD, D_MODEL)
    bias_tabs = _attn_bias_tables(rel_bias)
    a_re, a_im, f_re, f_im = _ssm_params(lam_re, lam_im, log_dt)
    bb_re, bb_im, cc_re, cc_im = _ssm_matrices(f_re, f_im, b_re, b_im, c_re, c_im)
    a_re = a_re.reshape(DEPTH * SSM_SLABS, SLAB_TILES, LANES)
    a_im = a_im.reshape(DEPTH * SSM_SLABS, SLAB_TILES, LANES)
    d_skip = d_skip.reshape(DEPTH * SSM_SLABS, 1, SLAB_CH)
    for l in range(DEPTH):
        x = _ffn(x, mod, norm_ffn1, w_ffn1_in, w_ffn1_out, l, 0)

        *qkv_groups, u, gates = _inproj(x, mod, norm_mix, w_in, l)
        attn_o = _attention([qkv.reshape(BATCH, SEQ, QKV_WIDTH) for qkv in qkv_groups], bias_tabs)
        y_ssm = _ssm(u, bb_re, bb_im, cc_re, cc_im, a_re, a_im, d_skip, l)
        x = _ffn(x, mod, norm_ffn2, w_ffn2_in, w_ffn2_out, l, 6,
                 final_norm=final_norm if l == DEPTH - 1 else None,
                 mixer=(attn_o, y_ssm, gates, w_attn_proj, w_glu, w_out))
    return x
```

```python
import functools
import math

import jax
import jax.numpy as jnp
import numpy as np
from jax import lax
from jax.experimental import pallas as pl
from jax.experimental.pallas import tpu as pltpu

D_MODEL = 1024
BATCH = 4
SEQ = 4096
DEPTH = 2
HEAD_DIM = 64
HEADS_PER_GROUP = 8
ATTN_GROUPS = ((128, 1), (512, 4), (2048, 16))
N_GROUPS = len(ATTN_GROUPS)
GROUP_WIDTH = HEADS_PER_GROUP * HEAD_DIM
BLOCK = 128
REL_BUCKETS = 32
REL_MAX_DIST = 2048
NEG = -1e30
SSM_WIDTH = 512
SSM_GROUP = 16
SSM_GROUPS = 32
SSM_STATE = 64
D_FF = 2816
QKV_COLS = 3 * N_GROUPS * GROUP_WIDTH
IN_COLS = QKV_COLS + SSM_WIDTH + 2 * D_MODEL
N_MOD = 9
EPS = 1e-6

LANES = 128
SUBLANES = 8
MXU_DIM = 256
FREE_STRIDE = 4
VMEM_LIMIT = 56 * 1024 * 1024

ADA_TN = 1152
FFN_TM = 1024
FFN_TF = MXU_DIM
PROJ_TM = 512
QKV_WIDTH = 3 * GROUP_WIDTH
FFN_MIX_TM = 512
WEIGHT_CHUNKS = 8
HEAD_PAIRS = GROUP_WIDTH // LANES
ATTN_UNROLL = 16
ATTN_ORDER = (2, 1, 0)
LOG2E = math.log2(math.e)
QSCALE = HEAD_DIM ** -0.5 * LOG2E
N_QBLOCKS = SEQ // BLOCK
SSM_TC = 512
SSM_PITCH = SSM_TC + SUBLANES
SLAB_CH = MXU_DIM
SSM_SLABS = SSM_WIDTH // SLAB_CH
SLAB_STATES = (SSM_GROUPS // SSM_SLABS) * SSM_STATE
SLAB_TILES = SLAB_STATES // LANES
HALF_TILES = SLAB_TILES * BATCH // SUBLANES

BF16 = jnp.bfloat16
F32 = jnp.float32


def _cparams(sem):
    return pltpu.CompilerParams(dimension_semantics=sem, vmem_limit_bytes=VMEM_LIMIT)


def _first_grid_step():
    return (pl.program_id(0) == 0) & (pl.program_id(1) == 0)


def _weight_scratch(k, n, chunks):
    return [pltpu.VMEM((k, n), BF16), pltpu.VMEM((2, k // chunks, n), F32),
            pltpu.SemaphoreType.DMA((2,))]


def _load_weight(w_hbm, layer, w_ref, stage_ref, sem):
    rows = stage_ref.shape[1]
    n_chunks = w_ref.shape[0] // rows

    def chunk(i, slot):
        return pltpu.make_async_copy(w_hbm.at[layer, pl.ds(i * rows, rows), :],
                                     stage_ref.at[slot], sem.at[slot])

    chunk(0, 0).start()

    def body(i, carry):
        slot = i % 2

        @pl.when(i + 1 < n_chunks)
        def _():
            chunk(i + 1, 1 - slot).start()

        chunk(i, slot).wait()
        w_ref[pl.ds(pl.multiple_of(i * rows, rows), rows), :] = stage_ref[slot].astype(BF16)
        return carry

    lax.fori_loop(0, n_chunks, body, 0)


def _bdot(a, b):
    return jnp.dot(a.astype(BF16), b.astype(BF16), preferred_element_type=F32)


def _sigmoid(x):
    return 0.5 * jnp.tanh(0.5 * x) + 0.5


def _norm_mod(x, g, shift, scale):
    ms = jnp.mean(x * x, axis=-1, keepdims=True)
    y = x * lax.rsqrt(ms + EPS) * g
    return y * (1.0 + scale) + shift


def _ada_kernel(c_ref, w_ref, b_ref, o_ref):
    c = c_ref[...]
    ca = c * _sigmoid(c)
    o_ref[...] = _bdot(ca, w_ref[...]) + b_ref[...]


def _ada(c_pad, w_ada, b_ada):
    n = N_MOD * D_MODEL
    return pl.pallas_call(
        _ada_kernel,
        grid=(DEPTH, n // ADA_TN),
        in_specs=[
            pl.BlockSpec((SUBLANES, D_MODEL), lambda l, j: (0, 0)),
            pl.BlockSpec((None, D_MODEL, ADA_TN), lambda l, j: (l, 0, j)),
            pl.BlockSpec((None, 1, ADA_TN), lambda l, j: (l, 0, j)),
        ],
        out_specs=pl.BlockSpec((None, SUBLANES, ADA_TN), lambda l, j: (l, 0, j)),
        out_shape=jax.ShapeDtypeStruct((DEPTH, SUBLANES, n), F32),
        compiler_params=_cparams(("arbitrary", "arbitrary")),
        name="ada_mod",
    )(c_pad, w_ada, b_ada.reshape(DEPTH, 1, n))


def _ffn_kernel(x_ref, mod_ref, g_ref, win_hbm, wout_hbm, *rest, layer, row0, final, mixer):
    if mixer:
        (ao_ref, ys_ref, ga_ref, gs_ref, wap_hbm, wglu_hbm, wmix_hbm), rest = rest[:7], rest[7:]
    if final:
        fn_ref, rest = rest[0], rest[1:]
    o_ref, act_ref, win_ref, wout_ref, a_stage, b_stage, o_stage, sem = rest[:8]
    if mixer:
        (wap_ref, wap_stage, wap_sem, wglu_ref, wglu_stage, wglu_sem,
         wmix_ref, wmix_stage, wmix_sem) = rest[8:]
    n_chunks = D_FF // FFN_TF

    def chunk_copies(j, slot):
        lo, hi = j * FFN_TF, (j + 1) * FFN_TF
        return (pltpu.make_async_copy(win_hbm.at[layer, :, lo:hi], a_stage.at[slot], sem.at[0, slot]),
                pltpu.make_async_copy(win_hbm.at[layer, :, D_FF + lo:D_FF + hi], b_stage.at[slot],
                                      sem.at[1, slot]),
                pltpu.make_async_copy(wout_hbm.at[layer, lo:hi, :], o_stage.at[slot], sem.at[2, slot]))

    def tile(load_weights):
        x = x_ref[...]
        if load_weights:
            for copy in chunk_copies(0, 0):
                copy.start()
        if mixer:
            if load_weights:
                _load_weight(wap_hbm, layer, wap_ref, wap_stage, wap_sem)
                _load_weight(wglu_hbm, layer, wglu_ref, wglu_stage, wglu_sem)
                _load_weight(wmix_hbm, layer, wmix_ref, wmix_stage, wmix_sem)
            y_attn = _bdot(ao_ref[...], wap_ref[...])
            gl = jnp.dot(ys_ref[...], wglu_ref[...], preferred_element_type=F32)
            y_ssm = gl[:, :D_MODEL] * _sigmoid(gl[:, D_MODEL:])
            mixed = (_sigmoid(ga_ref[...].astype(F32)) * y_attn
                     + _sigmoid(gs_ref[...].astype(F32)) * y_ssm)
            x = x + mod_ref[5:6, :] * _bdot(mixed, wmix_ref[...])
        h = _norm_mod(x, g_ref[...], mod_ref[row0:row0 + 1, :], mod_ref[row0 + 1:row0 + 2, :])
        hb = h.astype(BF16)
        for j in range(n_chunks):
            lo, hi = j * FFN_TF, (j + 1) * FFN_TF
            if load_weights:
                if j + 1 < n_chunks:
                    for copy in chunk_copies(j + 1, (j + 1) % 2):
                        copy.start()
                for copy in chunk_copies(j, j % 2):
                    copy.wait()
                win_ref[:, lo:hi] = a_stage[j % 2].astype(BF16)
                win_ref[:, D_FF + lo:D_FF + hi] = b_stage[j % 2].astype(BF16)
                wout_ref[lo:hi, :] = o_stage[j % 2].astype(BF16)
            a = jnp.dot(hb, win_ref[:, lo:hi], preferred_element_type=F32)
            b = jnp.dot(hb, win_ref[:, D_FF + lo:D_FF + hi], preferred_element_type=F32)
            act_ref[:, lo:hi] = (a * _sigmoid(a) * b).astype(BF16)
        y = jnp.dot(act_ref[...], wout_ref[...], preferred_element_type=F32)
        out = x + (0.5 * mod_ref[row0 + 2:row0 + 3, :]) * y
        if final:
            ms = jnp.mean(out * out, axis=-1, keepdims=True)
            out = out * lax.rsqrt(ms + EPS) * fn_ref[...]
        o_ref[...] = out

    first = _first_grid_step()
    pl.when(first)(functools.partial(tile, True))
    pl.when(jnp.logical_not(first))(functools.partial(tile, False))


def _ffn(x, mod, norm_g, w_in, w_out, layer, row0, final_norm=None, mixer=None):
    final = final_norm is not None
    tm = FFN_MIX_TM if mixer else FFN_TM
    rows = lambda b, i: (b, i, 0)
    in_specs = [
        pl.BlockSpec((None, tm, D_MODEL), rows),
        pl.BlockSpec((None, None, N_MOD, D_MODEL), lambda b, i: (layer, b, 0, 0)),
        pl.BlockSpec((None, 1, D_MODEL), lambda b, i: (layer, 0, 0)),
        pl.BlockSpec(memory_space=pl.ANY),
        pl.BlockSpec(memory_space=pl.ANY),
    ]
    args = [x, mod, norm_g.reshape(DEPTH, 1, D_MODEL), w_in, w_out]
    scratch = [
        pltpu.VMEM((tm, D_FF), BF16),
        pltpu.VMEM((D_MODEL, 2 * D_FF), BF16),
        pltpu.VMEM((D_FF, D_MODEL), BF16),
        pltpu.VMEM((2, D_MODEL, FFN_TF), F32),
        pltpu.VMEM((2, D_MODEL, FFN_TF), F32),
        pltpu.VMEM((2, FFN_TF, D_MODEL), F32),
        pltpu.SemaphoreType.DMA((3, 2)),
    ]
    if mixer:
        attn_o, y_ssm, gates, w_ap, w_glu, w_mix = mixer
        in_specs += [
            pl.BlockSpec((None, tm, GROUP_WIDTH), rows),
            pl.BlockSpec((None, tm, SSM_WIDTH), rows),
            pl.BlockSpec((None, tm, D_MODEL), rows),
            pl.BlockSpec((None, tm, D_MODEL), lambda b, i: (b, i, 1)),
            pl.BlockSpec(memory_space=pl.ANY),
            pl.BlockSpec(memory_space=pl.ANY),
            pl.BlockSpec(memory_space=pl.ANY),
        ]
        args += [attn_o, y_ssm, gates, gates, w_ap, w_glu, w_mix]
        scratch += (_weight_scratch(GROUP_WIDTH, D_MODEL, WEIGHT_CHUNKS)
                    + _weight_scratch(SSM_WIDTH, 2 * D_MODEL, WEIGHT_CHUNKS)
                    + _weight_scratch(D_MODEL, D_MODEL, WEIGHT_CHUNKS))
    if final:
        in_specs.append(pl.BlockSpec((1, D_MODEL), lambda b, i: (0, 0)))
        args.append(final_norm.reshape(1, D_MODEL))
    return pl.pallas_call(
        functools.partial(_ffn_kernel, layer=layer, row0=row0, final=final, mixer=bool(mixer)),
        grid=(BATCH, SEQ // tm),
        in_specs=in_specs,
        out_specs=pl.BlockSpec((None, tm, D_MODEL), rows),
        out_shape=jax.ShapeDtypeStruct((BATCH, SEQ, D_MODEL), F32),
        scratch_shapes=scratch,
        compiler_params=_cparams(("arbitrary", "arbitrary")),
        name=("mix_" if mixer else "") + ("ffn_final" if final else "ffn"),
    )(*args)


def _rows_by_residue(hf, rows_ref, half_ref):
    tiles = D_MODEL // LANES
    out = {1: hf.astype(BF16)}
    if all(d == 1 for _, d in ATTN_GROUPS):
        return out
    for t in range(tiles):
        rows_ref[t] = hf[:, t * LANES:(t + 1) * LANES]

    def gather(ref, pieces):
        return jnp.concatenate(
            [jnp.concatenate([ref[t, pl.ds(start, count, stride=stride), :]
                              for start, count, stride in pieces], axis=0)
             for t in range(tiles)], axis=1)

    per_inner = PROJ_TM // FREE_STRIDE
    inner = gather(rows_ref, [(b, per_inner, FREE_STRIDE) for b in range(FREE_STRIDE)])
    for _, dilation in ATTN_GROUPS:
        if dilation == 1 or dilation in out:
            continue
        if dilation == FREE_STRIDE:
            out[dilation] = inner.astype(BF16)
            continue
        outer = dilation // FREE_STRIDE
        assert dilation % FREE_STRIDE == 0 and outer <= FREE_STRIDE
        for t in range(tiles):
            half_ref[t] = inner[:, t * LANES:(t + 1) * LANES]
        pieces = [(b * per_inner + a, PROJ_TM // dilation, outer)
                  for a, b in (divmod(r, FREE_STRIDE) for r in range(dilation))]
        out[dilation] = gather(half_ref, pieces).astype(BF16)
    return out


def _inproj_kernel(x_ref, mod_ref, g_ref, w_hbm, qkv0_ref, qkv1_ref, qkv2_ref, u_ref, gate_ref,
                   w_ref, w_stage, w_sem, rows_ref, half_ref, *, layer):
    qkv_refs = (qkv0_ref, qkv1_ref, qkv2_ref)
    n_segs = IN_COLS // GROUP_WIDTH
    order = sorted(range(n_segs),
                   key=lambda seg: ATTN_GROUPS[seg % N_GROUPS][1] if seg < 3 * N_GROUPS else 1)

    def seg_copy(k, slot):
        seg = order[k]
        return pltpu.make_async_copy(
            w_hbm.at[layer, :, seg * GROUP_WIDTH:(seg + 1) * GROUP_WIDTH], w_stage.at[slot],
            w_sem.at[slot])

    def tile(load_weights):
        hf = _norm_mod(x_ref[...], g_ref[...], mod_ref[3:4, :], mod_ref[4:5, :])
        h = _rows_by_residue(hf, rows_ref, half_ref)
        if load_weights:
            seg_copy(0, 0).start()
        for k, seg in enumerate(order):
            cols = slice(seg * GROUP_WIDTH, (seg + 1) * GROUP_WIDTH)
            if load_weights:
                if k + 1 < n_segs:
                    seg_copy(k + 1, (k + 1) % 2).start()
                seg_copy(k, k % 2).wait()
                w_ref[:, cols] = w_stage[k % 2].astype(BF16)
            if seg < 3 * N_GROUPS:
                which, g = divmod(seg, N_GROUPS)
                dilation = ATTN_GROUPS[g][1]
                ys = jnp.dot(h[dilation], w_ref[:, cols], preferred_element_type=F32)
                if which == 0:
                    ys = ys * QSCALE
                n = PROJ_TM // dilation
                for r in range(dilation):
                    qkv_refs[g][r, :, which * GROUP_WIDTH:(which + 1) * GROUP_WIDTH] = (
                        ys[r * n:(r + 1) * n, :].astype(BF16))
            else:
                ys = jnp.dot(h[1], w_ref[:, cols], preferred_element_type=F32)
                if seg == 3 * N_GROUPS:
                    u_ref[...] = ys
                else:
                    g0 = (seg - 3 * N_GROUPS - 1) * GROUP_WIDTH
                    gate_ref[:, g0:g0 + GROUP_WIDTH] = ys.astype(BF16)

    first = _first_grid_step()
    pl.when(first)(functools.partial(tile, True))
    pl.when(jnp.logical_not(first))(functools.partial(tile, False))


def _inproj(x, mod, norm_g, w_in, layer):
    rows = lambda b, i: (b, i, 0)
    qkv_specs, qkv_shapes = [], []
    for _, dilation in ATTN_GROUPS:
        qkv_specs.append(pl.BlockSpec((None, dilation, PROJ_TM // dilation, QKV_WIDTH),
                                      lambda b, i: (b, 0, i, 0)))
        qkv_shapes.append(jax.ShapeDtypeStruct((BATCH, dilation, SEQ // dilation, QKV_WIDTH), BF16))
    tiles = D_MODEL // LANES
    return pl.pallas_call(
        functools.partial(_inproj_kernel, layer=layer),
        grid=(BATCH, SEQ // PROJ_TM),
        in_specs=[
            pl.BlockSpec((None, PROJ_TM, D_MODEL), rows),
            pl.BlockSpec((None, None, N_MOD, D_MODEL), lambda b, i: (layer, b, 0, 0)),
            pl.BlockSpec((None, 1, D_MODEL), lambda b, i: (layer, 0, 0)),
            pl.BlockSpec(memory_space=pl.ANY),
        ],
        out_specs=(*qkv_specs,
                   pl.BlockSpec((None, PROJ_TM, SSM_WIDTH), rows),
                   pl.BlockSpec((None, PROJ_TM, 2 * D_MODEL), rows)),
        out_shape=(*qkv_shapes,
                   jax.ShapeDtypeStruct((BATCH, SEQ, SSM_WIDTH), F32),
                   jax.ShapeDtypeStruct((BATCH, SEQ, 2 * D_MODEL), BF16)),
        scratch_shapes=[
            pltpu.VMEM((D_MODEL, IN_COLS), BF16),
            pltpu.VMEM((2, D_MODEL, GROUP_WIDTH), F32),
            pltpu.SemaphoreType.DMA((2,)),
            pltpu.VMEM((tiles, PROJ_TM, LANES), F32),
            pltpu.VMEM((tiles, PROJ_TM, LANES), F32),
        ],
        compiler_params=_cparams(("arbitrary", "arbitrary")),
        name="inproj",
    )(x, mod, norm_g.reshape(DEPTH, 1, D_MODEL), w_in)


def _t5_bucket(dist):
    max_exact = REL_BUCKETS // 2
    d = np.maximum(dist, max_exact).astype(np.float32)
    large = max_exact + (np.log(d / max_exact) / np.log(REL_MAX_DIST / max_exact)
                         * (REL_BUCKETS - max_exact)).astype(np.int32)
    large = np.minimum(large, REL_BUCKETS - 1)
    return np.where(dist < max_exact, dist, large).astype(np.int32)


def _attn_bias_tables(rel_bias):
    qi = np.arange(BLOCK)[:, None]
    kj = np.arange(2 * BLOCK)[None, :]
    rel = BLOCK + qi - kj
    tabs = []
    for g, (window, dilation) in enumerate(ATTN_GROUPS):
        band = (rel >= 0) & (rel <= window // dilation)
        bucket = _t5_bucket(np.clip(rel, 0, None) * dilation)
        tbl = rel_bias[:, g * HEADS_PER_GROUP:(g + 1) * HEADS_PER_GROUP]
        onehot = jnp.asarray(bucket[None] == np.arange(REL_BUCKETS)[:, None, None], F32)
        bias = jnp.einsum('rqk,rh->hqk', onehot, tbl.astype(F32),
                          precision=lax.Precision.HIGHEST)
        general = jnp.where(band[None], bias * LOG2E, NEG)
        masked = jnp.full((HEADS_PER_GROUP, BLOCK, BLOCK), NEG, F32)
        first = jnp.concatenate([masked, general[:, :, BLOCK:]], axis=2)
        very_first = jnp.concatenate([general[:, :, BLOCK:], masked], axis=2)
        tabs.append(jnp.stack([very_first, first, general]))
    return jnp.stack(tabs).reshape(N_GROUPS, 3, HEAD_PAIRS, 2 * BLOCK, 2 * BLOCK)


def _attn_group(q_ref, k_ref, v_ref, bias_ref, o_ref, part_ref, *, dilation, slot):
    nb = SEQ // dilation // BLOCK
    lane = lax.broadcasted_iota(jnp.int32, (BLOCK, LANES), 1)
    head0 = lane < HEAD_DIM
    keep0 = jnp.where(head0, 1.0, 0.0).astype(BF16)
    keep1 = jnp.where(head0, 0.0, 1.0).astype(BF16)
    ones_cols = jnp.ones((2 * BLOCK, LANES), BF16)
    contract_last = (((1,), (1,)), ((), ()))

    def token_rows(n):
        start = n // nb + (n % nb) * (BLOCK * dilation)
        if dilation == 1:
            start = pl.multiple_of(start, BLOCK)
        return pl.ds(start, BLOCK, stride=dilation)

    def key_rows(n):
        return pl.ds(pl.multiple_of(jnp.maximum(n - 1, 0) * BLOCK, BLOCK), 2 * BLOCK)

    def logits(n):
        qb = q_ref[pl.ds(pl.multiple_of(n * BLOCK, BLOCK), BLOCK), :]
        qq = jnp.concatenate([qb * keep0, qb * keep1], axis=0)
        tab = jnp.where(n == 0, 0, jnp.where(n % nb == 0, 1, 2))
        l = lax.dot_general(qq, k_ref[key_rows(n), :], contract_last,
                            preferred_element_type=F32) + bias_ref[tab]
        return l, jnp.max(l, axis=1, keepdims=True)

    def weighted(n, l, m_rows):
        v_aug = jnp.concatenate([v_ref[key_rows(n), :], ones_cols], axis=1)
        p = jnp.exp2(l - m_rows).astype(BF16)
        r = jnp.dot(p, v_aug, preferred_element_type=F32)
        return (jnp.where(head0, r[:BLOCK, :LANES], r[BLOCK:, :LANES]),
                jnp.where(head0, r[:BLOCK, LANES:], r[BLOCK:, LANES:]))

    def body(step, carry):
        ns = [step * ATTN_UNROLL + u for u in range(ATTN_UNROLL)]
        scores = [logits(n) for n in ns]
        for n, (l, m_rows) in zip(ns, scores):
            acc, den = weighted(n, l, m_rows)
            m_blk = jnp.where(head0, m_rows[:BLOCK], m_rows[BLOCK:])
            rows = token_rows(n)
            if slot is not None:
                part_ref[slot, 0, rows, :] = acc / den
                part_ref[slot, 1, rows, :] = m_blk + jnp.log2(den)
            else:
                others = [(part_ref[g, 0, rows, :], part_ref[g, 1, rows, :])
                          for g in range(N_GROUPS - 1)]
                m_all = functools.reduce(jnp.maximum, [m_blk] + [lse for _, lse in others])
                w_own = jnp.exp2(m_blk - m_all)
                num = w_own * acc
                den_all = w_own * den
                for out_g, lse in others:
                    w = jnp.exp2(lse - m_all)
                    num = num + w * out_g
                    den_all = den_all + w
                o_ref[rows, :] = num / den_all
        return carry

    lax.fori_loop(0, N_QBLOCKS // ATTN_UNROLL, body, 0)


def _attn_kernel(*refs):
    qkv_refs, (bias_ref, o_ref, part_ref) = refs[:3 * N_GROUPS], refs[3 * N_GROUPS:]
    for i, gi in enumerate(ATTN_ORDER):
        q_ref, k_ref, v_ref = qkv_refs[3 * gi:3 * gi + 3]
        _attn_group(q_ref, k_ref, v_ref, bias_ref.at[gi], o_ref, part_ref,
                    dilation=ATTN_GROUPS[gi][1], slot=i if i < N_GROUPS - 1 else None)


def _attention(qkv_groups, bias_tabs):
    in_specs, args = [], []
    for qkv in qkv_groups:
        for which in range(3):
            in_specs.append(pl.BlockSpec(
                (None, SEQ, LANES), lambda b, hp, which=which: (b, 0, which * HEAD_PAIRS + hp)))
            args.append(qkv)
    in_specs.append(pl.BlockSpec((N_GROUPS, 3, None, 2 * BLOCK, 2 * BLOCK),
                                 lambda b, hp: (0, 0, hp, 0, 0)))
    return pl.pallas_call(
        _attn_kernel,
        grid=(BATCH, HEAD_PAIRS),
        in_specs=in_specs,
        out_specs=pl.BlockSpec((None, SEQ, LANES), lambda b, hp: (b, 0, hp)),
        out_shape=jax.ShapeDtypeStruct((BATCH, SEQ, GROUP_WIDTH), F32),
        scratch_shapes=[pltpu.VMEM((N_GROUPS - 1, 2, SEQ, LANES), F32)],
        compiler_params=_cparams(("arbitrary", "arbitrary")),
        name="dilated_attn",
    )(*args, bias_tabs)


def _ssm_param_kernel(lre_ref, lim_ref, ldt_ref, are_ref, aim_ref, fre_ref, fim_ref):
    lam_re = lre_ref[...]
    lam_im = lim_ref[...]
    dt = jnp.exp(ldt_ref[...])
    mag = jnp.exp(lam_re * dt)
    ang = lam_im * dt
    a_re = mag * jnp.cos(ang)
    a_im = mag * jnp.sin(ang)
    den = lam_re * lam_re + lam_im * lam_im
    are_ref[...] = a_re
    aim_ref[...] = a_im
    fre_ref[...] = ((a_re - 1) * lam_re + a_im * lam_im) / den
    fim_ref[...] = (a_im * lam_re - (a_re - 1) * lam_im) / den


def _ssm_params(lam_re, lam_im, log_dt):
    n = DEPTH * SSM_GROUPS
    shp = jax.ShapeDtypeStruct((n, SSM_STATE), F32)
    return pl.pallas_call(
        _ssm_param_kernel, out_shape=(shp, shp, shp, shp), name="ssm_discretise",
    )(lam_re.reshape(n, SSM_STATE), lam_im.reshape(n, SSM_STATE), log_dt.reshape(n, 1))


def _ssm_kernel(u_ref, bre_ref, bim_ref, cre_ref, cim_ref, are_ref, aim_ref, d_ref,
                y_ref, bre_s, bim_s, sre_ref, sim_ref, xre_ref, xim_ref):
    c = pl.program_id(1)

    @pl.when(c == 0)
    def _():
        xre_ref[...] = jnp.zeros_like(xre_ref)
        xim_ref[...] = jnp.zeros_like(xim_ref)

    def plane_rows(t, b):
        half, pair = divmod(t, HALF_TILES)
        return pair, pl.ds(half * BATCH + b, SSM_TC, stride=SUBLANES)

    def input_rows(t, b):
        half, pair = divmod(t, HALF_TILES)
        r0 = (half * BATCH + b) * SSM_PITCH
        return pair, slice(r0, r0 + SSM_TC)

    for b in range(BATCH):
        ub = u_ref[b].astype(BF16)
        bu_re = jnp.dot(ub, bre_ref[...], preferred_element_type=F32)
        bu_im = jnp.dot(ub, bim_ref[...], preferred_element_type=F32)
        for t in range(SLAB_TILES):
            pair, sl = input_rows(t, b)
            bre_s[pair, sl, :] = bu_re[:, t * LANES:(t + 1) * LANES]
            bim_s[pair, sl, :] = bu_im[:, t * LANES:(t + 1) * LANES]

    def coeff(ref, pair):
        return jnp.concatenate(
            [jnp.broadcast_to(ref[half * HALF_TILES + pair:half * HALF_TILES + pair + 1, :],
                              (BATCH, LANES)) for half in range(2)], axis=0)

    a_re = [coeff(are_ref, p) for p in range(HALF_TILES)]
    a_im = [coeff(aim_ref, p) for p in range(HALF_TILES)]

    def step(i, carry):
        xs = list(carry)
        rows = pl.ds(pl.multiple_of(i * SUBLANES, SUBLANES), SUBLANES)
        in_rows = pl.ds(i, SUBLANES, stride=SSM_PITCH)
        for p in range(HALF_TILES):
            xr, xi = xs[2 * p], xs[2 * p + 1]
            nr = a_re[p] * xr - a_im[p] * xi + bre_s[p, in_rows, :]
            ni = a_re[p] * xi + a_im[p] * xr + bim_s[p, in_rows, :]
            sre_ref[p, rows, :] = nr
            sim_ref[p, rows, :] = ni
            xs[2 * p], xs[2 * p + 1] = nr, ni
        return tuple(xs)

    init = []
    for p in range(HALF_TILES):
        init += [xre_ref[p], xim_ref[p]]
    fin = lax.fori_loop(0, SSM_TC, step, tuple(init), unroll=8)
    for p in range(HALF_TILES):
        xre_ref[p] = fin[2 * p]
        xim_ref[p] = fin[2 * p + 1]

    for b in range(BATCH):
        tiles = [plane_rows(t, b) for t in range(SLAB_TILES)]
        x_re = jnp.concatenate([sre_ref[pair, sl, :] for pair, sl in tiles], axis=1)
        x_im = jnp.concatenate([sim_ref[pair, sl, :] for pair, sl in tiles], axis=1)
        y = (_bdot(x_re, cre_ref[...]) + _bdot(x_im, cim_ref[...])
             + d_ref[...] * u_ref[b])
        y_ref[b] = jax.nn.gelu(y).astype(BF16)


def _ssm(u, bb_re, bb_im, cc_re, cc_im, a_re, a_im, d_skip, layer):
    slab = lambda s, c: (layer * SSM_SLABS + s, 0, 0)
    return pl.pallas_call(
        _ssm_kernel,
        grid=(SSM_SLABS, SEQ // SSM_TC),
        in_specs=[
            pl.BlockSpec((BATCH, SSM_TC, SLAB_CH), lambda s, c: (0, c, s)),
            pl.BlockSpec((None, SLAB_CH, SLAB_STATES), slab),
            pl.BlockSpec((None, SLAB_CH, SLAB_STATES), slab),
            pl.BlockSpec((None, SLAB_STATES, SLAB_CH), slab),
            pl.BlockSpec((None, SLAB_STATES, SLAB_CH), slab),
            pl.BlockSpec((None, SLAB_TILES, LANES), slab),
            pl.BlockSpec((None, SLAB_TILES, LANES), slab),
            pl.BlockSpec((None, 1, SLAB_CH), slab),
        ],
        out_specs=pl.BlockSpec((BATCH, SSM_TC, SLAB_CH), lambda s, c: (0, c, s)),
        out_shape=jax.ShapeDtypeStruct((BATCH, SEQ, SSM_WIDTH), BF16),
        scratch_shapes=[
            pltpu.VMEM((HALF_TILES, SUBLANES * SSM_PITCH, LANES), F32),
            pltpu.VMEM((HALF_TILES, SUBLANES * SSM_PITCH, LANES), F32),
            pltpu.VMEM((HALF_TILES, SUBLANES * SSM_TC, LANES), F32),
            pltpu.VMEM((HALF_TILES, SUBLANES * SSM_TC, LANES), F32),
            pltpu.VMEM((HALF_TILES, SUBLANES, LANES), F32),
            pltpu.VMEM((HALF_TILES, SUBLANES, LANES), F32),
        ],
        compiler_params=_cparams(("arbitrary", "arbitrary")),
        name="s5_scan",
    )(u, bb_re, bb_im, cc_re, cc_im, a_re, a_im, d_skip)


def _ssm_matrices(f_re, f_im, b_re, b_im, c_re, c_im):
    n = DEPTH * SSM_GROUPS
    f_re, f_im = f_re[..., None], f_im[..., None]
    b_re = b_re.reshape(n, SSM_STATE, SSM_GROUP)
    b_im = b_im.reshape(n, SSM_STATE, SSM_GROUP)
    bb_re = f_re * b_re - f_im * b_im
    bb_im = f_re * b_im + f_im * b_re
    gps = SSM_GROUPS // SSM_SLABS
    slabs = DEPTH * SSM_SLABS

    def block_diag(blocks):
        _, _, r, c = blocks.shape
        tiled = jnp.tile(blocks.reshape(slabs, gps * r, c), (1, 1, gps))
        on_diag = (np.arange(gps * r)[:, None] // r) == (np.arange(gps * c)[None, :] // c)
        return jnp.where(jnp.asarray(on_diag), tiled, 0.0)

    def in_map(bb):
        bb = bb.reshape(slabs, gps, SSM_STATE, SSM_GROUP)
        return block_diag(jnp.swapaxes(bb, 2, 3))

    def out_map(cc):
        cc = cc.reshape(slabs, gps, SSM_GROUP, SSM_STATE)
        return block_diag(jnp.swapaxes(cc, 2, 3))

    return (in_map(bb_re).astype(BF16), in_map(bb_im).astype(BF16),
            out_map(c_re).astype(BF16), out_map(-c_im).astype(BF16))


def kernel(x, c, w_ada, b_ada, norm_ffn1, w_ffn1_in, w_ffn1_out, norm_mix, w_in, rel_bias, lam_re, lam_im, log_dt, b_re, b_im, c_re, c_im, d_skip, w_glu, w_attn_proj, w_out, norm_ffn2, w_ffn2_in, w_ffn2_out, final_norm):
    c_pad = jnp.zeros((SUBLANES, D_MODEL), F32).at[:BATCH].set(c)
    mod = _ada(c_pad, w_ada, b_ada).reshape(DEPTH, SUBLANES, N_MOD, D_MODEL)
    bias_tabs = _attn_bias_tables(rel_bias)
    a_re, a_im, f_re, f_im = _ssm_params(lam_re, lam_im, log_dt)
    bb_re, bb_im, cc_re, cc_im = _ssm_matrices(f_re, f_im, b_re, b_im, c_re, c_im)
    a_re = a_re.reshape(DEPTH * SSM_SLABS, SLAB_TILES, LANES)
    a_im = a_im.reshape(DEPTH * SSM_SLABS, SLAB_TILES, LANES)
    d_skip = d_skip.reshape(DEPTH * SSM_SLABS, 1, SLAB_CH)
    for l in range(DEPTH):
        x = _ffn(x, mod, norm_ffn1, w_ffn1_in, w_ffn1_out, l, 0)

        *qkv_groups, u, gates = _inproj(x, mod, norm_mix, w_in, l)
        attn_o = _attention([qkv.reshape(BATCH, SEQ, QKV_WIDTH) for qkv in qkv_groups], bias_tabs)
        y_ssm = _ssm(u, bb_re, bb_im, cc_re, cc_im, a_re, a_im, d_skip, l)
        x = _ffn(x, mod, norm_ffn2, w_ffn2_in, w_ffn2_out, l, 6,
                 final_norm=final_norm if l == DEPTH - 1 else None,
                 mixer=(attn_o, y_ssm, gates, w_attn_proj, w_glu, w_out))
    return x
```

```python
import functools
import math

import jax
import jax.numpy as jnp
import numpy as np
from jax import lax
from jax.experimental import pallas as pl
from jax.experimental.pallas import tpu as pltpu

D_MODEL = 1024
BATCH = 4
SEQ = 4096
DEPTH = 2
HEAD_DIM = 64
HEADS_PER_GROUP = 8
ATTN_GROUPS = ((128, 1), (512, 4), (2048, 16))
N_GROUPS = len(ATTN_GROUPS)
GROUP_WIDTH = HEADS_PER_GROUP * HEAD_DIM
BLOCK = 128
REL_BUCKETS = 32
REL_MAX_DIST = 2048
NEG = -1e30
SSM_WIDTH = 512
SSM_GROUP = 16
SSM_GROUPS = 32
SSM_STATE = 64
D_FF = 2816
QKV_COLS = 3 * N_GROUPS * GROUP_WIDTH
IN_COLS = QKV_COLS + SSM_WIDTH + 2 * D_MODEL
N_MOD = 9
EPS = 1e-6

LANES = 128
SUBLANES = 8
MXU_DIM = 256
FREE_STRIDE = 4
VMEM_LIMIT = 56 * 1024 * 1024

ADA_TN = 1152
FFN_TM = 1024
FFN_TF = MXU_DIM
PROJ_TM = 512
QKV_WIDTH = 3 * GROUP_WIDTH
FFN_MIX_TM = 512
WEIGHT_CHUNKS = 8
HEAD_PAIRS = GROUP_WIDTH // LANES
ATTN_UNROLL = 16
ATTN_ORDER = (2, 1, 0)
LOG2E = math.log2(math.e)
QSCALE = HEAD_DIM ** -0.5 * LOG2E
N_QBLOCKS = SEQ // BLOCK
SSM_TC = 512
SSM_PITCH = SSM_TC + SUBLANES
SLAB_CH = MXU_DIM
SSM_SLABS = SSM_WIDTH // SLAB_CH
SLAB_STATES = (SSM_GROUPS // SSM_SLABS) * SSM_STATE
SLAB_TILES = SLAB_STATES // LANES
HALF_TILES = SLAB_TILES * BATCH // SUBLANES

BF16 = jnp.bfloat16
F32 = jnp.float32


def _cparams(sem):
    return pltpu.CompilerParams(dimension_semantics=sem, vmem_limit_bytes=VMEM_LIMIT)


def _first_grid_step():
    return (pl.program_id(0) == 0) & (pl.program_id(1) == 0)


def _weight_scratch(k, n, chunks):
    return [pltpu.VMEM((k, n), BF16), pltpu.VMEM((2, k // chunks, n), F32),
            pltpu.SemaphoreType.DMA((2,))]


def _load_weight(w_hbm, layer, w_ref, stage_ref, sem):
    rows = stage_ref.shape[1]
    n_chunks = w_ref.shape[0] // rows

    def chunk(i, slot):
        return pltpu.make_async_copy(w_hbm.at[layer, pl.ds(i * rows, rows), :],
                                     stage_ref.at[slot], sem.at[slot])

    chunk(0, 0).start()

    def body(i, carry):
        slot = i % 2

        @pl.when(i + 1 < n_chunks)
        def _():
            chunk(i + 1, 1 - slot).start()

        chunk(i, slot).wait()
        w_ref[pl.ds(pl.multiple_of(i * rows, rows), rows), :] = stage_ref[slot].astype(BF16)
        return carry

    lax.fori_loop(0, n_chunks, body, 0)


def _bdot(a, b):
    return jnp.dot(a.astype(BF16), b.astype(BF16), preferred_element_type=F32)


def _sigmoid(x):
    return 0.5 * jnp.tanh(0.5 * x) + 0.5


def _norm_mod(x, g, shift, scale):
    ms = jnp.mean(x * x, axis=-1, keepdims=True)
    y = x * lax.rsqrt(ms + EPS) * g
    return y * (1.0 + scale) + shift


def _ada_kernel(c_ref, w_ref, b_ref, o_ref):
    c = c_ref[...]
    ca = c * _sigmoid(c)
    o_ref[...] = _bdot(ca, w_ref[...]) + b_ref[...]


def _ada(c_pad, w_ada, b_ada):
    n = N_MOD * D_MODEL
    return pl.pallas_call(
        _ada_kernel,
        grid=(DEPTH, n // ADA_TN),
        in_specs=[
            pl.BlockSpec((SUBLANES, D_MODEL), lambda l, j: (0, 0)),
            pl.BlockSpec((None, D_MODEL, ADA_TN), lambda l, j: (l, 0, j)),
            pl.BlockSpec((None, 1, ADA_TN), lambda l, j: (l, 0, j)),
        ],
        out_specs=pl.BlockSpec((None, SUBLANES, ADA_TN), lambda l, j: (l, 0, j)),
        out_shape=jax.ShapeDtypeStruct((DEPTH, SUBLANES, n), F32),
        compiler_params=_cparams(("arbitrary", "arbitrary")),
        name="ada_mod",
    )(c_pad, w_ada, b_ada.reshape(DEPTH, 1, n))


def _ffn_kernel(x_ref, mod_ref, g_ref, win_hbm, wout_hbm, *rest, layer, row0, final, mixer):
    if mixer:
        (ao_ref, ys_ref, ga_ref, gs_ref, wap_hbm, wglu_hbm, wmix_hbm), rest = rest[:7], rest[7:]
    if final:
        fn_ref, rest = rest[0], rest[1:]
    o_ref, act_ref, win_ref, wout_ref, a_stage, b_stage, o_stage, sem = rest[:8]
    if mixer:
        (wap_ref, wap_stage, wap_sem, wglu_ref, wglu_stage, wglu_sem,
         wmix_ref, wmix_stage, wmix_sem) = rest[8:]
    n_chunks = D_FF // FFN_TF

    def chunk_copies(j, slot):
        lo, hi = j * FFN_TF, (j + 1) * FFN_TF
        return (pltpu.make_async_copy(win_hbm.at[layer, :, lo:hi], a_stage.at[slot], sem.at[0, slot]),
                pltpu.make_async_copy(win_hbm.at[layer, :, D_FF + lo:D_FF + hi], b_stage.at[slot],
                                      sem.at[1, slot]),
                pltpu.make_async_copy(wout_hbm.at[layer, lo:hi, :], o_stage.at[slot], sem.at[2, slot]))

    def tile(load_weights):
        x = x_ref[...]
        if load_weights:
            for copy in chunk_copies(0, 0):
                copy.start()
        if mixer:
            if load_weights:
                _load_weight(wap_hbm, layer, wap_ref, wap_stage, wap_sem)
                _load_weight(wglu_hbm, layer, wglu_ref, wglu_stage, wglu_sem)
                _load_weight(wmix_hbm, layer, wmix_ref, wmix_stage, wmix_sem)
            y_attn = _bdot(ao_ref[...], wap_ref[...])
            gl = jnp.dot(ys_ref[...], wglu_ref[...], preferred_element_type=F32)
            y_ssm = gl[:, :D_MODEL] * _sigmoid(gl[:, D_MODEL:])
            mixed = (_sigmoid(ga_ref[...].astype(F32)) * y_attn
                     + _sigmoid(gs_ref[...].astype(F32)) * y_ssm)
            x = x + mod_ref[5:6, :] * _bdot(mixed, wmix_ref[...])
        h = _norm_mod(x, g_ref[...], mod_ref[row0:row0 + 1, :], mod_ref[row0 + 1:row0 + 2, :])
        hb = h.astype(BF16)
        for j in range(n_chunks):
            lo, hi = j * FFN_TF, (j + 1) * FFN_TF
            if load_weights:
                if j + 1 < n_chunks:
                    for copy in chunk_copies(j + 1, (j + 1) % 2):
                        copy.start()
                for copy in chunk_copies(j, j % 2):
                    copy.wait()
                win_ref[:, lo:hi] = a_stage[j % 2].astype(BF16)
                win_ref[:, D_FF + lo:D_FF + hi] = b_stage[j % 2].astype(BF16)
                wout_ref[lo:hi, :] = o_stage[j % 2].astype(BF16)
            a = jnp.dot(hb, win_ref[:, lo:hi], preferred_element_type=F32)
            b = jnp.dot(hb, win_ref[:, D_FF + lo:D_FF + hi], preferred_element_type=F32)
            act_ref[:, lo:hi] = (a * _sigmoid(a) * b).astype(BF16)
        y = jnp.dot(act_ref[...], wout_ref[...], preferred_element_type=F32)
        out = x + (0.5 * mod_ref[row0 + 2:row0 + 3, :]) * y
        if final:
            ms = jnp.mean(out * out, axis=-1, keepdims=True)
            out = out * lax.rsqrt(ms + EPS) * fn_ref[...]
        o_ref[...] = out

    first = _first_grid_step()
    pl.when(first)(functools.partial(tile, True))
    pl.when(jnp.logical_not(first))(functools.partial(tile, False))


def _ffn(x, mod, norm_g, w_in, w_out, layer, row0, final_norm=None, mixer=None):
    final = final_norm is not None
    tm = FFN_MIX_TM if mixer else FFN_TM
    rows = lambda b, i: (b, i, 0)
    in_specs = [
        pl.BlockSpec((None, tm, D_MODEL), rows),
        pl.BlockSpec((None, None, N_MOD, D_MODEL), lambda b, i: (layer, b, 0, 0)),
        pl.BlockSpec((None, 1, D_MODEL), lambda b, i: (layer, 0, 0)),
        pl.BlockSpec(memory_space=pl.ANY),
        pl.BlockSpec(memory_space=pl.ANY),
    ]
    args = [x, mod, norm_g.reshape(DEPTH, 1, D_MODEL), w_in, w_out]
    scratch = [
        pltpu.VMEM((tm, D_FF), BF16),
        pltpu.VMEM((D_MODEL, 2 * D_FF), BF16),
        pltpu.VMEM((D_FF, D_MODEL), BF16),
        pltpu.VMEM((2, D_MODEL, FFN_TF), F32),
        pltpu.VMEM((2, D_MODEL, FFN_TF), F32),
        pltpu.VMEM((2, FFN_TF, D_MODEL), F32),
        pltpu.SemaphoreType.DMA((3, 2)),
    ]
    if mixer:
        attn_o, y_ssm, gates, w_ap, w_glu, w_mix = mixer
        in_specs += [
            pl.BlockSpec((None, tm, GROUP_WIDTH), rows),
            pl.BlockSpec((None, tm, SSM_WIDTH), rows),
            pl.BlockSpec((None, tm, D_MODEL), rows),
            pl.BlockSpec((None, tm, D_MODEL), lambda b, i: (b, i, 1)),
            pl.BlockSpec(memory_space=pl.ANY),
            pl.BlockSpec(memory_space=pl.ANY),
            pl.BlockSpec(memory_space=pl.ANY),
        ]
        args += [attn_o, y_ssm, gates, gates, w_ap, w_glu, w_mix]
        scratch += (_weight_scratch(GROUP_WIDTH, D_MODEL, WEIGHT_CHUNKS)
                    + _weight_scratch(SSM_WIDTH, 2 * D_MODEL, WEIGHT_CHUNKS)
                    + _weight_scratch(D_MODEL, D_MODEL, WEIGHT_CHUNKS))
    if final:
        in_specs.append(pl.BlockSpec((1, D_MODEL), lambda b, i: (0, 0)))
        args.append(final_norm.reshape(1, D_MODEL))
    return pl.pallas_call(
        functools.partial(_ffn_kernel, layer=layer, row0=row0, final=final, mixer=bool(mixer)),
        grid=(BATCH, SEQ // tm),
        in_specs=in_specs,
        out_specs=pl.BlockSpec((None, tm, D_MODEL), rows),
        out_shape=jax.ShapeDtypeStruct((BATCH, SEQ, D_MODEL), F32),
        scratch_shapes=scratch,
        compiler_params=_cparams(("arbitrary", "arbitrary")),
        name=("mix_" if mixer else "") + ("ffn_final" if final else "ffn"),
    )(*args)


def _rows_by_residue(hf, rows_ref, half_ref):
    tiles = D_MODEL // LANES
    out = {1: hf.astype(BF16)}
    if all(d == 1 for _, d in ATTN_GROUPS):
        return out
    for t in range(tiles):
        rows_ref[t] = hf[:, t * LANES:(t + 1) * LANES]

    def gather(ref, pieces):
        return jnp.concatenate(
            [jnp.concatenate([ref[t, pl.ds(start, count, stride=stride), :]
                              for start, count, stride in pieces], axis=0)
             for t in range(tiles)], axis=1)

    per_inner = PROJ_TM // FREE_STRIDE
    inner = gather(rows_ref, [(b, per_inner, FREE_STRIDE) for b in range(FREE_STRIDE)])
    for _, dilation in ATTN_GROUPS:
        if dilation == 1 or dilation in out:
            continue
        if dilation == FREE_STRIDE:
            out[dilation] = inner.astype(BF16)
            continue
        outer = dilation // FREE_STRIDE
        assert dilation % FREE_STRIDE == 0 and outer <= FREE_STRIDE
        for t in range(tiles):
            half_ref[t] = inner[:, t * LANES:(t + 1) * LANES]
        pieces = [(b * per_inner + a, PROJ_TM // dilation, outer)
                  for a, b in (divmod(r, FREE_STRIDE) for r in range(dilation))]
        out[dilation] = gather(half_ref, pieces).astype(BF16)
    return out


def _inproj_kernel(x_ref, mod_ref, g_ref, w_hbm, qkv0_ref, qkv1_ref, qkv2_ref, u_ref, gate_ref,
                   w_ref, w_stage, w_sem, rows_ref, half_ref, *, layer):
    qkv_refs = (qkv0_ref, qkv1_ref, qkv2_ref)
    n_segs = IN_COLS // GROUP_WIDTH
    order = sorted(range(n_segs),
                   key=lambda seg: ATTN_GROUPS[seg % N_GROUPS][1] if seg < 3 * N_GROUPS else 1)

    def seg_copy(k, slot):
        seg = order[k]
        return pltpu.make_async_copy(
            w_hbm.at[layer, :, seg * GROUP_WIDTH:(seg + 1) * GROUP_WIDTH], w_stage.at[slot],
            w_sem.at[slot])

    def tile(load_weights):
        hf = _norm_mod(x_ref[...], g_ref[...], mod_ref[3:4, :], mod_ref[4:5, :])
        h = _rows_by_residue(hf, rows_ref, half_ref)
        if load_weights:
            seg_copy(0, 0).start()
        for k, seg in enumerate(order):
            cols = slice(seg * GROUP_WIDTH, (seg + 1) * GROUP_WIDTH)
            if load_weights:
                if k + 1 < n_segs:
                    seg_copy(k + 1, (k + 1) % 2).start()
                seg_copy(k, k % 2).wait()
                w_ref[:, cols] = w_stage[k % 2].astype(BF16)
            if seg < 3 * N_GROUPS:
                which, g = divmod(seg, N_GROUPS)
                dilation = ATTN_GROUPS[g][1]
                ys = jnp.dot(h[dilation], w_ref[:, cols], preferred_element_type=F32)
                if which == 0:
                    ys = ys * QSCALE
                n = PROJ_TM // dilation
                for r in range(dilation):
                    qkv_refs[g][r, :, which * GROUP_WIDTH:(which + 1) * GROUP_WIDTH] = (
                        ys[r * n:(r + 1) * n, :].astype(BF16))
            else:
                ys = jnp.dot(h[1], w_ref[:, cols], preferred_element_type=F32)
                if seg == 3 * N_GROUPS:
                    u_ref[...] = ys
                else:
                    g0 = (seg - 3 * N_GROUPS - 1) * GROUP_WIDTH
                    gate_ref[:, g0:g0 + GROUP_WIDTH] = ys.astype(BF16)

    first = _first_grid_step()
    pl.when(first)(functools.partial(tile, True))
    pl.when(jnp.logical_not(first))(functools.partial(tile, False))


def _inproj(x, mod, norm_g, w_in, layer):
    rows = lambda b, i: (b, i, 0)
    qkv_specs, qkv_shapes = [], []
    for _, dilation in ATTN_GROUPS:
        qkv_specs.append(pl.BlockSpec((None, dilation, PROJ_TM // dilation, QKV_WIDTH),
                                      lambda b, i: (b, 0, i, 0)))
        qkv_shapes.append(jax.ShapeDtypeStruct((BATCH, dilation, SEQ // dilation, QKV_WIDTH), BF16))
    tiles = D_MODEL // LANES
    return pl.pallas_call(
        functools.partial(_inproj_kernel, layer=layer),
        grid=(BATCH, SEQ // PROJ_TM),
        in_specs=[
            pl.BlockSpec((None, PROJ_TM, D_MODEL), rows),
            pl.BlockSpec((None, None, N_MOD, D_MODEL), lambda b, i: (layer, b, 0, 0)),
            pl.BlockSpec((None, 1, D_MODEL), lambda b, i: (layer, 0, 0)),
            pl.BlockSpec(memory_space=pl.ANY),
        ],
        out_specs=(*qkv_specs,
                   pl.BlockSpec((None, PROJ_TM, SSM_WIDTH), rows),
                   pl.BlockSpec((None, PROJ_TM, 2 * D_MODEL), rows)),
        out_shape=(*qkv_shapes,
                   jax.ShapeDtypeStruct((BATCH, SEQ, SSM_WIDTH), F32),
                   jax.ShapeDtypeStruct((BATCH, SEQ, 2 * D_MODEL), BF16)),
        scratch_shapes=[
            pltpu.VMEM((D_MODEL, IN_COLS), BF16),
            pltpu.VMEM((2, D_MODEL, GROUP_WIDTH), F32),
            pltpu.SemaphoreType.DMA((2,)),
            pltpu.VMEM((tiles, PROJ_TM, LANES), F32),
            pltpu.VMEM((tiles, PROJ_TM, LANES), F32),
        ],
        compiler_params=_cparams(("arbitrary", "arbitrary")),
        name="inproj",
    )(x, mod, norm_g.reshape(DEPTH, 1, D_MODEL), w_in)


def _t5_bucket(dist):
    max_exact = REL_BUCKETS // 2
    d = np.maximum(dist, max_exact).astype(np.float32)
    large = max_exact + (np.log(d / max_exact) / np.log(REL_MAX_DIST / max_exact)
                         * (REL_BUCKETS - max_exact)).astype(np.int32)
    large = np.minimum(large, REL_BUCKETS - 1)
    return np.where(dist < max_exact, dist, large).astype(np.int32)


def _attn_bias_tables(rel_bias):
    qi = np.arange(BLOCK)[:, None]
    kj = np.arange(2 * BLOCK)[None, :]
    rel = BLOCK + qi - kj
    tabs = []
    for g, (window, dilation) in enumerate(ATTN_GROUPS):
        band = (rel >= 0) & (rel <= window // dilation)
        bucket = _t5_bucket(np.clip(rel, 0, None) * dilation)
        tbl = rel_bias[:, g * HEADS_PER_GROUP:(g + 1) * HEADS_PER_GROUP]
        onehot = jnp.asarray(bucket[None] == np.arange(REL_BUCKETS)[:, None, None], F32)
        bias = jnp.einsum('rqk,rh->hqk', onehot, tbl.astype(F32),
                          precision=lax.Precision.HIGHEST)
        general = jnp.where(band[None], bias * LOG2E, NEG)
        masked = jnp.full((HEADS_PER_GROUP, BLOCK, BLOCK), NEG, F32)
        first = jnp.concatenate([masked, general[:, :, BLOCK:]], axis=2)
        very_first = jnp.concatenate([general[:, :, BLOCK:], masked], axis=2)
        tabs.append(jnp.stack([very_first, first, general]))
    return jnp.stack(tabs).reshape(N_GROUPS, 3, HEAD_PAIRS, 2 * BLOCK, 2 * BLOCK)


def _attn_group(q_ref, k_ref, v_ref, bias_ref, o_ref, part_ref, *, dilation, next_dilation,
                src, dst):
    nb = SEQ // dilation // BLOCK
    lane = lax.broadcasted_iota(jnp.int32, (BLOCK, LANES), 1)
    head0 = lane < HEAD_DIM
    keep0 = jnp.where(head0, 1.0, 0.0).astype(BF16)
    keep1 = jnp.where(head0, 0.0, 1.0).astype(BF16)
    ones_cols = jnp.ones((2 * BLOCK, LANES), BF16)
    contract_last = (((1,), (1,)), ((), ()))

    step_out = dilation // next_dilation
    assert dilation % next_dilation == 0 and step_out <= FREE_STRIDE

    def next_rows(n):
        r = n // nb
        start = ((r % next_dilation) * (SEQ // next_dilation) + r // next_dilation
                 + (n % nb) * (BLOCK * step_out))
        if step_out == 1:
            start = pl.multiple_of(start, BLOCK)
        return pl.ds(start, BLOCK, stride=step_out)

    def key_rows(n):
        return pl.ds(pl.multiple_of(jnp.maximum(n - 1, 0) * BLOCK, BLOCK), 2 * BLOCK)

    def logits(n):
        qb = q_ref[pl.ds(pl.multiple_of(n * BLOCK, BLOCK), BLOCK), :]
        qq = jnp.concatenate([qb * keep0, qb * keep1], axis=0)
        tab = jnp.where(n == 0, 0, jnp.where(n % nb == 0, 1, 2))
        l = lax.dot_general(qq, k_ref[key_rows(n), :], contract_last,
                            preferred_element_type=F32) + bias_ref[tab]
        return l, jnp.max(l, axis=1, keepdims=True)

    def weighted(n, l, m_rows):
        v_aug = jnp.concatenate([v_ref[key_rows(n), :], ones_cols], axis=1)
        p = jnp.exp2(l - m_rows).astype(BF16)
        r = jnp.dot(p, v_aug, preferred_element_type=F32)
        return (jnp.where(head0, r[:BLOCK, :LANES], r[BLOCK:, :LANES]),
                jnp.where(head0, r[:BLOCK, LANES:], r[BLOCK:, LANES:]))

    def body(step, carry):
        ns = [step * ATTN_UNROLL + u for u in range(ATTN_UNROLL)]
        scores = [logits(n) for n in ns]
        for n, (l, m_rows) in zip(ns, scores):
            acc, den = weighted(n, l, m_rows)
            m_blk = jnp.where(head0, m_rows[:BLOCK], m_rows[BLOCK:])
            if src is not None:
                own = pl.ds(pl.multiple_of(n * BLOCK, BLOCK), BLOCK)
                out_prev, lse_prev = part_ref[src, 0, own, :], part_ref[src, 1, own, :]
                m_all = jnp.maximum(m_blk, lse_prev)
                w_own = jnp.exp2(m_blk - m_all)
                w_prev = jnp.exp2(lse_prev - m_all)
                acc = w_own * acc + w_prev * out_prev
                den = w_own * den + w_prev
                m_blk = m_all
            rows = next_rows(n)
            if dst is not None:
                part_ref[dst, 0, rows, :] = acc / den
                part_ref[dst, 1, rows, :] = m_blk + jnp.log2(den)
            else:
                o_ref[rows, :] = acc / den
        return carry

    lax.fori_loop(0, N_QBLOCKS // ATTN_UNROLL, body, 0)


def _attn_kernel(*refs):
    qkv_refs, (bias_ref, o_ref, part_ref) = refs[:3 * N_GROUPS], refs[3 * N_GROUPS:]
    dilations = [ATTN_GROUPS[gi][1] for gi in ATTN_ORDER] + [1]
    for i, gi in enumerate(ATTN_ORDER):
        q_ref, k_ref, v_ref = qkv_refs[3 * gi:3 * gi + 3]
        _attn_group(q_ref, k_ref, v_ref, bias_ref.at[gi], o_ref, part_ref,
                    dilation=dilations[i], next_dilation=dilations[i + 1],
                    src=i - 1 if i > 0 else None, dst=i if i < N_GROUPS - 1 else None)


def _attention(qkv_groups, bias_tabs):
    in_specs, args = [], []
    for qkv in qkv_groups:
        for which in range(3):
            in_specs.append(pl.BlockSpec(
                (None, SEQ, LANES), lambda b, hp, which=which: (b, 0, which * HEAD_PAIRS + hp)))
            args.append(qkv)
    in_specs.append(pl.BlockSpec((N_GROUPS, 3, None, 2 * BLOCK, 2 * BLOCK),
                                 lambda b, hp: (0, 0, hp, 0, 0)))
    return pl.pallas_call(
        _attn_kernel,
        grid=(BATCH, HEAD_PAIRS),
        in_specs=in_specs,
        out_specs=pl.BlockSpec((None, SEQ, LANES), lambda b, hp: (b, 0, hp)),
        out_shape=jax.ShapeDtypeStruct((BATCH, SEQ, GROUP_WIDTH), F32),
        scratch_shapes=[pltpu.VMEM((N_GROUPS - 1, 2, SEQ, LANES), F32)],
        compiler_params=_cparams(("arbitrary", "arbitrary")),
        name="dilated_attn",
    )(*args, bias_tabs)


def _ssm_param_kernel(lre_ref, lim_ref, ldt_ref, are_ref, aim_ref, fre_ref, fim_ref):
    lam_re = lre_ref[...]
    lam_im = lim_ref[...]
    dt = jnp.exp(ldt_ref[...])
    mag = jnp.exp(lam_re * dt)
    ang = lam_im * dt
    a_re = mag * jnp.cos(ang)
    a_im = mag * jnp.sin(ang)
    den = lam_re * lam_re + lam_im * lam_im
    are_ref[...] = a_re
    aim_ref[...] = a_im
    fre_ref[...] = ((a_re - 1) * lam_re + a_im * lam_im) / den
    fim_ref[...] = (a_im * lam_re - (a_re - 1) * lam_im) / den


def _ssm_params(lam_re, lam_im, log_dt):
    n = DEPTH * SSM_GROUPS
    shp = jax.ShapeDtypeStruct((n, SSM_STATE), F32)
    return pl.pallas_call(
        _ssm_param_kernel, out_shape=(shp, shp, shp, shp), name="ssm_discretise",
    )(lam_re.reshape(n, SSM_STATE), lam_im.reshape(n, SSM_STATE), log_dt.reshape(n, 1))


def _ssm_kernel(u_ref, bre_ref, bim_ref, cre_ref, cim_ref, are_ref, aim_ref, d_ref,
                y_ref, bre_s, bim_s, sre_ref, sim_ref, xre_ref, xim_ref):
    c = pl.program_id(1)

    @pl.when(c == 0)
    def _():
        xre_ref[...] = jnp.zeros_like(xre_ref)
        xim_ref[...] = jnp.zeros_like(xim_ref)

    def plane_rows(t, b):
        half, pair = divmod(t, HALF_TILES)
        return pair, pl.ds(half * BATCH + b, SSM_TC, stride=SUBLANES)

    def input_rows(t, b):
        half, pair = divmod(t, HALF_TILES)
        r0 = (half * BATCH + b) * SSM_PITCH
        return pair, slice(r0, r0 + SSM_TC)

    for b in range(BATCH):
        ub = u_ref[b].astype(BF16)
        bu_re = jnp.dot(ub, bre_ref[...], preferred_element_type=F32)
        bu_im = jnp.dot(ub, bim_ref[...], preferred_element_type=F32)
        for t in range(SLAB_TILES):
            pair, sl = input_rows(t, b)
            bre_s[pair, sl, :] = bu_re[:, t * LANES:(t + 1) * LANES]
            bim_s[pair, sl, :] = bu_im[:, t * LANES:(t + 1) * LANES]

    def coeff(ref, pair):
        return jnp.concatenate(
            [jnp.broadcast_to(ref[half * HALF_TILES + pair:half * HALF_TILES + pair + 1, :],
                              (BATCH, LANES)) for half in range(2)], axis=0)

    a_re = [coeff(are_ref, p) for p in range(HALF_TILES)]
    a_im = [coeff(aim_ref, p) for p in range(HALF_TILES)]

    def step(i, carry):
        xs = list(carry)
        rows = pl.ds(pl.multiple_of(i * SUBLANES, SUBLANES), SUBLANES)
        in_rows = pl.ds(i, SUBLANES, stride=SSM_PITCH)
        for p in range(HALF_TILES):
            xr, xi = xs[2 * p], xs[2 * p + 1]
            nr = a_re[p] * xr - a_im[p] * xi + bre_s[p, in_rows, :]
            ni = a_re[p] * xi + a_im[p] * xr + bim_s[p, in_rows, :]
            sre_ref[p, rows, :] = nr
            sim_ref[p, rows, :] = ni
            xs[2 * p], xs[2 * p + 1] = nr, ni
        return tuple(xs)

    init = []
    for p in range(HALF_TILES):
        init += [xre_ref[p], xim_ref[p]]
    fin = lax.fori_loop(0, SSM_TC, step, tuple(init), unroll=8)
    for p in range(HALF_TILES):
        xre_ref[p] = fin[2 * p]
        xim_ref[p] = fin[2 * p + 1]

    for b in range(BATCH):
        tiles = [plane_rows(t, b) for t in range(SLAB_TILES)]
        x_re = jnp.concatenate([sre_ref[pair, sl, :] for pair, sl in tiles], axis=1)
        x_im = jnp.concatenate([sim_ref[pair, sl, :] for pair, sl in tiles], axis=1)
        y = (_bdot(x_re, cre_ref[...]) + _bdot(x_im, cim_ref[...])
             + d_ref[...] * u_ref[b])
        y_ref[b] = jax.nn.gelu(y).astype(BF16)


def _ssm(u, bb_re, bb_im, cc_re, cc_im, a_re, a_im, d_skip, layer):
    slab = lambda s, c: (layer * SSM_SLABS + s, 0, 0)
    return pl.pallas_call(
        _ssm_kernel,
        grid=(SSM_SLABS, SEQ // SSM_TC),
        in_specs=[
            pl.BlockSpec((BATCH, SSM_TC, SLAB_CH), lambda s, c: (0, c, s)),
            pl.BlockSpec((None, SLAB_CH, SLAB_STATES), slab),
            pl.BlockSpec((None, SLAB_CH, SLAB_STATES), slab),
            pl.BlockSpec((None, SLAB_STATES, SLAB_CH), slab),
            pl.BlockSpec((None, SLAB_STATES, SLAB_CH), slab),
            pl.BlockSpec((None, SLAB_TILES, LANES), slab),
            pl.BlockSpec((None, SLAB_TILES, LANES), slab),
            pl.BlockSpec((None, 1, SLAB_CH), slab),
        ],
        out_specs=pl.BlockSpec((BATCH, SSM_TC, SLAB_CH), lambda s, c: (0, c, s)),
        out_shape=jax.ShapeDtypeStruct((BATCH, SEQ, SSM_WIDTH), BF16),
        scratch_shapes=[
            pltpu.VMEM((HALF_TILES, SUBLANES * SSM_PITCH, LANES), F32),
            pltpu.VMEM((HALF_TILES, SUBLANES * SSM_PITCH, LANES), F32),
            pltpu.VMEM((HALF_TILES, SUBLANES * SSM_TC, LANES), F32),
            pltpu.VMEM((HALF_TILES, SUBLANES * SSM_TC, LANES), F32),
            pltpu.VMEM((HALF_TILES, SUBLANES, LANES), F32),
            pltpu.VMEM((HALF_TILES, SUBLANES, LANES), F32),
        ],
        compiler_params=_cparams(("arbitrary", "arbitrary")),
        name="s5_scan",
    )(u, bb_re, bb_im, cc_re, cc_im, a_re, a_im, d_skip)


def _ssm_matrices(f_re, f_im, b_re, b_im, c_re, c_im):
    n = DEPTH * SSM_GROUPS
    f_re, f_im = f_re[..., None], f_im[..., None]
    b_re = b_re.reshape(n, SSM_STATE, SSM_GROUP)
    b_im = b_im.reshape(n, SSM_STATE, SSM_GROUP)
    bb_re = f_re * b_re - f_im * b_im
    bb_im = f_re * b_im + f_im * b_re
    gps = SSM_GROUPS // SSM_SLABS
    slabs = DEPTH * SSM_SLABS

    def block_diag(blocks):
        _, _, r, c = blocks.shape
        tiled = jnp.tile(blocks.reshape(slabs, gps * r, c), (1, 1, gps))
        on_diag = (np.arange(gps * r)[:, None] // r) == (np.arange(gps * c)[None, :] // c)
        return jnp.where(jnp.asarray(on_diag), tiled, 0.0)

    def in_map(bb):
        bb = bb.reshape(slabs, gps, SSM_STATE, SSM_GROUP)
        return block_diag(jnp.swapaxes(bb, 2, 3))

    def out_map(cc):
        cc = cc.reshape(slabs, gps, SSM_GROUP, SSM_STATE)
        return block_diag(jnp.swapaxes(cc, 2, 3))

    return (in_map(bb_re).astype(BF16), in_map(bb_im).astype(BF16),
            out_map(c_re).astype(BF16), out_map(-c_im).astype(BF16))


def kernel(x, c, w_ada, b_ada, norm_ffn1, w_ffn1_in, w_ffn1_out, norm_mix, w_in, rel_bias, lam_re, lam_im, log_dt, b_re, b_im, c_re, c_im, d_skip, w_glu, w_attn_proj, w_out, norm_ffn2, w_ffn2_in, w_ffn2_out, final_norm):
    c_pad = jnp.zeros((SUBLANES, D_MODEL), F32).at[:BATCH].set(c)
    mod = _ada(c_pad, w_ada, b_ada).reshape(DEPTH, SUBLANES, N_MOD, D_MODEL)
    bias_tabs = _attn_bias_tables(rel_bias)
    a_re, a_im, f_re, f_im = _ssm_params(lam_re, lam_im, log_dt)
    bb_re, bb_im, cc_re, cc_im = _ssm_matrices(f_re, f_im, b_re, b_im, c_re, c_im)
    a_re = a_re.reshape(DEPTH * SSM_SLABS, SLAB_TILES, LANES)
    a_im = a_im.reshape(DEPTH * SSM_SLABS, SLAB_TILES, LANES)
    d_skip = d_skip.reshape(DEPTH * SSM_SLABS, 1, SLAB_CH)
    for l in range(DEPTH):
        x = _ffn(x, mod, norm_ffn1, w_ffn1_in, w_ffn1_out, l, 0)

        *qkv_groups, u, gates = _inproj(x, mod, norm_mix, w_in, l)
        attn_o = _attention([qkv.reshape(BATCH, SEQ, QKV_WIDTH) for qkv in qkv_groups], bias_tabs)
        y_ssm = _ssm(u, bb_re, bb_im, cc_re, cc_im, a_re, a_im, d_skip, l)
        x = _ffn(x, mod, norm_ffn2, w_ffn2_in, w_ffn2_out, l, 6,
                 final_norm=final_norm if l == DEPTH - 1 else None,
                 mixer=(attn_o, y_ssm, gates, w_attn_proj, w_glu, w_out))
    return x
```

```python
import functools
import math

import jax
import jax.numpy as jnp
import numpy as np
from jax import lax
from jax.experimental import pallas as pl
from jax.experimental.pallas import tpu as pltpu

D_MODEL = 1024
BATCH = 4
SEQ = 4096
DEPTH = 2
HEAD_DIM = 64
HEADS_PER_GROUP = 8
ATTN_GROUPS = ((128, 1), (512, 4), (2048, 16))
N_GROUPS = len(ATTN_GROUPS)
GROUP_WIDTH = HEADS_PER_GROUP * HEAD_DIM
BLOCK = 128
REL_BUCKETS = 32
REL_MAX_DIST = 2048
NEG = -1e30
SSM_WIDTH = 512
SSM_GROUP = 16
SSM_GROUPS = 32
SSM_STATE = 64
D_FF = 2816
QKV_COLS = 3 * N_GROUPS * GROUP_WIDTH
IN_COLS = QKV_COLS + SSM_WIDTH + 2 * D_MODEL
N_MOD = 9
EPS = 1e-6

LANES = 128
SUBLANES = 8
MXU_DIM = 256
FREE_STRIDE = 4
VMEM_LIMIT = 56 * 1024 * 1024

ADA_TN = 1152
FFN_TM = 1024
FFN_TF = MXU_DIM
PROJ_TM = 512
QKV_WIDTH = 3 * GROUP_WIDTH
FFN_MIX_TM = 512
WEIGHT_CHUNKS = 8
HEAD_PAIRS = GROUP_WIDTH // LANES
ATTN_UNROLL = 32
ATTN_ORDER = (2, 1, 0)
LOG2E = math.log2(math.e)
QSCALE = HEAD_DIM ** -0.5 * LOG2E
N_QBLOCKS = SEQ // BLOCK
SSM_TC = 512
SSM_PITCH = SSM_TC + SUBLANES
SLAB_CH = MXU_DIM
SSM_SLABS = SSM_WIDTH // SLAB_CH
SLAB_STATES = (SSM_GROUPS // SSM_SLABS) * SSM_STATE
SLAB_TILES = SLAB_STATES // LANES
HALF_TILES = SLAB_TILES * BATCH // SUBLANES

BF16 = jnp.bfloat16
F32 = jnp.float32


def _cparams(sem):
    return pltpu.CompilerParams(dimension_semantics=sem, vmem_limit_bytes=VMEM_LIMIT)


def _first_grid_step():
    return (pl.program_id(0) == 0) & (pl.program_id(1) == 0)


def _weight_scratch(k, n, chunks):
    return [pltpu.VMEM((k, n), BF16), pltpu.VMEM((2, k // chunks, n), F32),
            pltpu.SemaphoreType.DMA((2,))]


def _load_weight(w_hbm, layer, w_ref, stage_ref, sem):
    rows = stage_ref.shape[1]
    n_chunks = w_ref.shape[0] // rows

    def chunk(i, slot):
        return pltpu.make_async_copy(w_hbm.at[layer, pl.ds(i * rows, rows), :],
                                     stage_ref.at[slot], sem.at[slot])

    chunk(0, 0).start()

    def body(i, carry):
        slot = i % 2

        @pl.when(i + 1 < n_chunks)
        def _():
            chunk(i + 1, 1 - slot).start()

        chunk(i, slot).wait()
        w_ref[pl.ds(pl.multiple_of(i * rows, rows), rows), :] = stage_ref[slot].astype(BF16)
        return carry

    lax.fori_loop(0, n_chunks, body, 0)


def _bdot(a, b):
    return jnp.dot(a.astype(BF16), b.astype(BF16), preferred_element_type=F32)


def _sigmoid(x):
    return 0.5 * jnp.tanh(0.5 * x) + 0.5


def _norm_mod(x, g, shift, scale):
    ms = jnp.mean(x * x, axis=-1, keepdims=True)
    y = x * lax.rsqrt(ms + EPS) * g
    return y * (1.0 + scale) + shift


def _ada_kernel(c_ref, w_ref, b_ref, o_ref):
    c = c_ref[...]
    ca = c * _sigmoid(c)
    o_ref[...] = _bdot(ca, w_ref[...]) + b_ref[...]


def _ada(c_pad, w_ada, b_ada):
    n = N_MOD * D_MODEL
    return pl.pallas_call(
        _ada_kernel,
        grid=(DEPTH, n // ADA_TN),
        in_specs=[
            pl.BlockSpec((SUBLANES, D_MODEL), lambda l, j: (0, 0)),
            pl.BlockSpec((None, D_MODEL, ADA_TN), lambda l, j: (l, 0, j)),
            pl.BlockSpec((None, 1, ADA_TN), lambda l, j: (l, 0, j)),
        ],
        out_specs=pl.BlockSpec((None, SUBLANES, ADA_TN), lambda l, j: (l, 0, j)),
        out_shape=jax.ShapeDtypeStruct((DEPTH, SUBLANES, n), F32),
        compiler_params=_cparams(("arbitrary", "arbitrary")),
        name="ada_mod",
    )(c_pad, w_ada, b_ada.reshape(DEPTH, 1, n))


def _ffn_kernel(x_ref, mod_ref, g_ref, win_hbm, wout_hbm, *rest, layer, row0, final, mixer):
    if mixer:
        (ao_ref, ys_ref, ga_ref, gs_ref, wap_hbm, wglu_hbm, wmix_hbm), rest = rest[:7], rest[7:]
    if final:
        fn_ref, rest = rest[0], rest[1:]
    o_ref, act_ref, win_ref, wout_ref, a_stage, b_stage, o_stage, sem = rest[:8]
    if mixer:
        (wap_ref, wap_stage, wap_sem, wglu_ref, wglu_stage, wglu_sem,
         wmix_ref, wmix_stage, wmix_sem) = rest[8:]
    n_chunks = D_FF // FFN_TF

    def chunk_copies(j, slot):
        lo, hi = j * FFN_TF, (j + 1) * FFN_TF
        return (pltpu.make_async_copy(win_hbm.at[layer, :, lo:hi], a_stage.at[slot], sem.at[0, slot]),
                pltpu.make_async_copy(win_hbm.at[layer, :, D_FF + lo:D_FF + hi], b_stage.at[slot],
                                      sem.at[1, slot]),
                pltpu.make_async_copy(wout_hbm.at[layer, lo:hi, :], o_stage.at[slot], sem.at[2, slot]))

    def tile(load_weights):
        x = x_ref[...]
        if load_weights:
            for copy in chunk_copies(0, 0):
                copy.start()
        if mixer:
            if load_weights:
                _load_weight(wap_hbm, layer, wap_ref, wap_stage, wap_sem)
                _load_weight(wglu_hbm, layer, wglu_ref, wglu_stage, wglu_sem)
                _load_weight(wmix_hbm, layer, wmix_ref, wmix_stage, wmix_sem)
            y_attn = _bdot(ao_ref[...], wap_ref[...])
            gl = jnp.dot(ys_ref[...], wglu_ref[...], preferred_element_type=F32)
            y_ssm = gl[:, :D_MODEL] * _sigmoid(gl[:, D_MODEL:])
            mixed = (_sigmoid(ga_ref[...].astype(F32)) * y_attn
                     + _sigmoid(gs_ref[...].astype(F32)) * y_ssm)
            x = x + mod_ref[5:6, :] * _bdot(mixed, wmix_ref[...])
        h = _norm_mod(x, g_ref[...], mod_ref[row0:row0 + 1, :], mod_ref[row0 + 1:row0 + 2, :])
        hb = h.astype(BF16)
        for j in range(n_chunks):
            lo, hi = j * FFN_TF, (j + 1) * FFN_TF
            if load_weights:
                if j + 1 < n_chunks:
                    for copy in chunk_copies(j + 1, (j + 1) % 2):
                        copy.start()
                for copy in chunk_copies(j, j % 2):
                    copy.wait()
                win_ref[:, lo:hi] = a_stage[j % 2].astype(BF16)
                win_ref[:, D_FF + lo:D_FF + hi] = b_stage[j % 2].astype(BF16)
                wout_ref[lo:hi, :] = o_stage[j % 2].astype(BF16)
            a = jnp.dot(hb, win_ref[:, lo:hi], preferred_element_type=F32)
            b = jnp.dot(hb, win_ref[:, D_FF + lo:D_FF + hi], preferred_element_type=F32)
            act_ref[:, lo:hi] = (a * _sigmoid(a) * b).astype(BF16)
        y = jnp.dot(act_ref[...], wout_ref[...], preferred_element_type=F32)
        out = x + (0.5 * mod_ref[row0 + 2:row0 + 3, :]) * y
        if final:
            ms = jnp.mean(out * out, axis=-1, keepdims=True)
            out = out * lax.rsqrt(ms + EPS) * fn_ref[...]
        o_ref[...] = out

    first = _first_grid_step()
    pl.when(first)(functools.partial(tile, True))
    pl.when(jnp.logical_not(first))(functools.partial(tile, False))


def _ffn(x, mod, norm_g, w_in, w_out, layer, row0, final_norm=None, mixer=None):
    final = final_norm is not None
    tm = FFN_MIX_TM if mixer else FFN_TM
    rows = lambda b, i: (b, i, 0)
    in_specs = [
        pl.BlockSpec((None, tm, D_MODEL), rows),
        pl.BlockSpec((None, None, N_MOD, D_MODEL), lambda b, i: (layer, b, 0, 0)),
        pl.BlockSpec((None, 1, D_MODEL), lambda b, i: (layer, 0, 0)),
        pl.BlockSpec(memory_space=pl.ANY),
        pl.BlockSpec(memory_space=pl.ANY),
    ]
    args = [x, mod, norm_g.reshape(DEPTH, 1, D_MODEL), w_in, w_out]
    scratch = [
        pltpu.VMEM((tm, D_FF), BF16),
        pltpu.VMEM((D_MODEL, 2 * D_FF), BF16),
        pltpu.VMEM((D_FF, D_MODEL), BF16),
        pltpu.VMEM((2, D_MODEL, FFN_TF), F32),
        pltpu.VMEM((2, D_MODEL, FFN_TF), F32),
        pltpu.VMEM((2, FFN_TF, D_MODEL), F32),
        pltpu.SemaphoreType.DMA((3, 2)),
    ]
    if mixer:
        attn_o, y_ssm, gates, w_ap, w_glu, w_mix = mixer
        in_specs += [
            pl.BlockSpec((None, tm, GROUP_WIDTH), rows),
            pl.BlockSpec((None, tm, SSM_WIDTH), rows),
            pl.BlockSpec((None, tm, D_MODEL), rows),
            pl.BlockSpec((None, tm, D_MODEL), lambda b, i: (b, i, 1)),
            pl.BlockSpec(memory_space=pl.ANY),
            pl.BlockSpec(memory_space=pl.ANY),
            pl.BlockSpec(memory_space=pl.ANY),
        ]
        args += [attn_o, y_ssm, gates, gates, w_ap, w_glu, w_mix]
        scratch += (_weight_scratch(GROUP_WIDTH, D_MODEL, WEIGHT_CHUNKS)
                    + _weight_scratch(SSM_WIDTH, 2 * D_MODEL, WEIGHT_CHUNKS)
                    + _weight_scratch(D_MODEL, D_MODEL, WEIGHT_CHUNKS))
    if final:
        in_specs.append(pl.BlockSpec((1, D_MODEL), lambda b, i: (0, 0)))
        args.append(final_norm.reshape(1, D_MODEL))
    return pl.pallas_call(
        functools.partial(_ffn_kernel, layer=layer, row0=row0, final=final, mixer=bool(mixer)),
        grid=(BATCH, SEQ // tm),
        in_specs=in_specs,
        out_specs=pl.BlockSpec((None, tm, D_MODEL), rows),
        out_shape=jax.ShapeDtypeStruct((BATCH, SEQ, D_MODEL), F32),
        scratch_shapes=scratch,
        compiler_params=_cparams(("arbitrary", "arbitrary")),
        name=("mix_" if mixer else "") + ("ffn_final" if final else "ffn"),
    )(*args)


def _rows_by_residue(hf, rows_ref, half_ref):
    tiles = D_MODEL // LANES
    out = {1: hf.astype(BF16)}
    if all(d == 1 for _, d in ATTN_GROUPS):
        return out
    for t in range(tiles):
        rows_ref[t] = hf[:, t * LANES:(t + 1) * LANES]

    def gather(ref, pieces):
        return jnp.concatenate(
            [jnp.concatenate([ref[t, pl.ds(start, count, stride=stride), :]
                              for start, count, stride in pieces], axis=0)
             for t in range(tiles)], axis=1)

    per_inner = PROJ_TM // FREE_STRIDE
    inner = gather(rows_ref, [(b, per_inner, FREE_STRIDE) for b in range(FREE_STRIDE)])
    for _, dilation in ATTN_GROUPS:
        if dilation == 1 or dilation in out:
            continue
        if dilation == FREE_STRIDE:
            out[dilation] = inner.astype(BF16)
            continue
        outer = dilation // FREE_STRIDE
        assert dilation % FREE_STRIDE == 0 and outer <= FREE_STRIDE
        for t in range(tiles):
            half_ref[t] = inner[:, t * LANES:(t + 1) * LANES]
        pieces = [(b * per_inner + a, PROJ_TM // dilation, outer)
                  for a, b in (divmod(r, FREE_STRIDE) for r in range(dilation))]
        out[dilation] = gather(half_ref, pieces).astype(BF16)
    return out


def _inproj_kernel(x_ref, mod_ref, g_ref, w_hbm, qkv0_ref, qkv1_ref, qkv2_ref, u_ref, gate_ref,
                   w_ref, w_stage, w_sem, rows_ref, half_ref, *, layer):
    qkv_refs = (qkv0_ref, qkv1_ref, qkv2_ref)
    n_segs = IN_COLS // GROUP_WIDTH
    order = sorted(range(n_segs),
                   key=lambda seg: ATTN_GROUPS[seg % N_GROUPS][1] if seg < 3 * N_GROUPS else 1)

    def seg_copy(k, slot):
        seg = order[k]
        return pltpu.make_async_copy(
            w_hbm.at[layer, :, seg * GROUP_WIDTH:(seg + 1) * GROUP_WIDTH], w_stage.at[slot],
            w_sem.at[slot])

    def tile(load_weights):
        hf = _norm_mod(x_ref[...], g_ref[...], mod_ref[3:4, :], mod_ref[4:5, :])
        h = _rows_by_residue(hf, rows_ref, half_ref)
        if load_weights:
            seg_copy(0, 0).start()
        for k, seg in enumerate(order):
            cols = slice(seg * GROUP_WIDTH, (seg + 1) * GROUP_WIDTH)
            if load_weights:
                if k + 1 < n_segs:
                    seg_copy(k + 1, (k + 1) % 2).start()
                seg_copy(k, k % 2).wait()
                w_ref[:, cols] = w_stage[k % 2].astype(BF16)
            if seg < 3 * N_GROUPS:
                which, g = divmod(seg, N_GROUPS)
                dilation = ATTN_GROUPS[g][1]
                ys = jnp.dot(h[dilation], w_ref[:, cols], preferred_element_type=F32)
                if which == 0:
                    ys = ys * QSCALE
                n = PROJ_TM // dilation
                for r in range(dilation):
                    qkv_refs[g][r, :, which * GROUP_WIDTH:(which + 1) * GROUP_WIDTH] = (
                        ys[r * n:(r + 1) * n, :].astype(BF16))
            else:
                ys = jnp.dot(h[1], w_ref[:, cols], preferred_element_type=F32)
                if seg == 3 * N_GROUPS:
                    u_ref[...] = ys
                else:
                    g0 = (seg - 3 * N_GROUPS - 1) * GROUP_WIDTH
                    gate_ref[:, g0:g0 + GROUP_WIDTH] = ys.astype(BF16)

    first = _first_grid_step()
    pl.when(first)(functools.partial(tile, True))
    pl.when(jnp.logical_not(first))(functools.partial(tile, False))


def _inproj(x, mod, norm_g, w_in, layer):
    rows = lambda b, i: (b, i, 0)
    qkv_specs, qkv_shapes = [], []
    for _, dilation in ATTN_GROUPS:
        qkv_specs.append(pl.BlockSpec((None, dilation, PROJ_TM // dilation, QKV_WIDTH),
                                      lambda b, i: (b, 0, i, 0)))
        qkv_shapes.append(jax.ShapeDtypeStruct((BATCH, dilation, SEQ // dilation, QKV_WIDTH), BF16))
    tiles = D_MODEL // LANES
    return pl.pallas_call(
        functools.partial(_inproj_kernel, layer=layer),
        grid=(BATCH, SEQ // PROJ_TM),
        in_specs=[
            pl.BlockSpec((None, PROJ_TM, D_MODEL), rows),
            pl.BlockSpec((None, None, N_MOD, D_MODEL), lambda b, i: (layer, b, 0, 0)),
            pl.BlockSpec((None, 1, D_MODEL), lambda b, i: (layer, 0, 0)),
            pl.BlockSpec(memory_space=pl.ANY),
        ],
        out_specs=(*qkv_specs,
                   pl.BlockSpec((None, PROJ_TM, SSM_WIDTH), rows),
                   pl.BlockSpec((None, PROJ_TM, 2 * D_MODEL), rows)),
        out_shape=(*qkv_shapes,
                   jax.ShapeDtypeStruct((BATCH, SEQ, SSM_WIDTH), F32),
                   jax.ShapeDtypeStruct((BATCH, SEQ, 2 * D_MODEL), BF16)),
        scratch_shapes=[
            pltpu.VMEM((D_MODEL, IN_COLS), BF16),
            pltpu.VMEM((2, D_MODEL, GROUP_WIDTH), F32),
            pltpu.SemaphoreType.DMA((2,)),
            pltpu.VMEM((tiles, PROJ_TM, LANES), F32),
            pltpu.VMEM((tiles, PROJ_TM, LANES), F32),
        ],
        compiler_params=_cparams(("arbitrary", "arbitrary")),
        name="inproj",
    )(x, mod, norm_g.reshape(DEPTH, 1, D_MODEL), w_in)


def _t5_bucket(dist):
    max_exact = REL_BUCKETS // 2
    d = np.maximum(dist, max_exact).astype(np.float32)
    large = max_exact + (np.log(d / max_exact) / np.log(REL_MAX_DIST / max_exact)
                         * (REL_BUCKETS - max_exact)).astype(np.int32)
    large = np.minimum(large, REL_BUCKETS - 1)
    return np.where(dist < max_exact, dist, large).astype(np.int32)


def _attn_bias_tables(rel_bias):
    qi = np.arange(BLOCK)[:, None]
    kj = np.arange(2 * BLOCK)[None, :]
    rel = BLOCK + qi - kj
    tabs = []
    for g, (window, dilation) in enumerate(ATTN_GROUPS):
        band = (rel >= 0) & (rel <= window // dilation)
        bucket = _t5_bucket(np.clip(rel, 0, None) * dilation)
        tbl = rel_bias[:, g * HEADS_PER_GROUP:(g + 1) * HEADS_PER_GROUP]
        onehot = jnp.asarray(bucket[None] == np.arange(REL_BUCKETS)[:, None, None], F32)
        bias = jnp.einsum('rqk,rh->hqk', onehot, tbl.astype(F32),
                          precision=lax.Precision.HIGHEST)
        general = jnp.where(band[None], bias * LOG2E, NEG)
        masked = jnp.full((HEADS_PER_GROUP, BLOCK, BLOCK), NEG, F32)
        first = jnp.concatenate([masked, general[:, :, BLOCK:]], axis=2)
        very_first = jnp.concatenate([general[:, :, BLOCK:], masked], axis=2)
        tabs.append(jnp.stack([very_first, first, general]))
    return jnp.stack(tabs).reshape(N_GROUPS, 3, HEAD_PAIRS, 2 * BLOCK, 2 * BLOCK)


def _attn_group(q_ref, k_ref, v_ref, bias_ref, o_ref, part_ref, *, dilation, next_dilation,
                src, dst):
    nb = SEQ // dilation // BLOCK
    lane = lax.broadcasted_iota(jnp.int32, (BLOCK, LANES), 1)
    head0 = lane < HEAD_DIM
    keep0 = jnp.where(head0, 1.0, 0.0).astype(BF16)
    keep1 = jnp.where(head0, 0.0, 1.0).astype(BF16)
    ones_cols = jnp.ones((2 * BLOCK, LANES), BF16)
    contract_last = (((1,), (1,)), ((), ()))

    step_out = dilation // next_dilation
    assert dilation % next_dilation == 0 and step_out <= FREE_STRIDE

    def next_rows(n):
        r = n // nb
        start = ((r % next_dilation) * (SEQ // next_dilation) + r // next_dilation
                 + (n % nb) * (BLOCK * step_out))
        if step_out == 1:
            start = pl.multiple_of(start, BLOCK)
        return pl.ds(start, BLOCK, stride=step_out)

    def key_rows(n):
        return pl.ds(pl.multiple_of(jnp.maximum(n - 1, 0) * BLOCK, BLOCK), 2 * BLOCK)

    def logits(n):
        qb = q_ref[pl.ds(pl.multiple_of(n * BLOCK, BLOCK), BLOCK), :]
        qq = jnp.concatenate([qb * keep0, qb * keep1], axis=0)
        tab = jnp.where(n == 0, 0, jnp.where(n % nb == 0, 1, 2))
        l = lax.dot_general(qq, k_ref[key_rows(n), :], contract_last,
                            preferred_element_type=F32) + bias_ref[tab]
        return l, jnp.max(l, axis=1, keepdims=True)

    def weighted(n, l, m_rows):
        v_aug = jnp.concatenate([v_ref[key_rows(n), :], ones_cols], axis=1)
        p = jnp.exp2(l - m_rows).astype(BF16)
        r = jnp.dot(p, v_aug, preferred_element_type=F32)
        return (jnp.where(head0, r[:BLOCK, :LANES], r[BLOCK:, :LANES]),
                jnp.where(head0, r[:BLOCK, LANES:], r[BLOCK:, LANES:]))

    def body(step, carry):
        ns = [step * ATTN_UNROLL + u for u in range(ATTN_UNROLL)]
        scores = [logits(n) for n in ns]
        for n, (l, m_rows) in zip(ns, scores):
            acc, den = weighted(n, l, m_rows)
            m_blk = jnp.where(head0, m_rows[:BLOCK], m_rows[BLOCK:])
            if src is not None:
                own = pl.ds(pl.multiple_of(n * BLOCK, BLOCK), BLOCK)
                out_prev, lse_prev = part_ref[src, 0, own, :], part_ref[src, 1, own, :]
                m_all = jnp.maximum(m_blk, lse_prev)
                w_own = jnp.exp2(m_blk - m_all)
                w_prev = jnp.exp2(lse_prev - m_all)
                acc = w_own * acc + w_prev * out_prev
                den = w_own * den + w_prev
                m_blk = m_all
            rows = next_rows(n)
            if dst is not None:
                part_ref[dst, 0, rows, :] = acc / den
                part_ref[dst, 1, rows, :] = m_blk + jnp.log2(den)
            else:
                o_ref[rows, :] = acc / den
        return carry

    lax.fori_loop(0, N_QBLOCKS // ATTN_UNROLL, body, 0)


def _attn_kernel(*refs):
    qkv_refs, (bias_ref, o_ref, part_ref) = refs[:3 * N_GROUPS], refs[3 * N_GROUPS:]
    dilations = [ATTN_GROUPS[gi][1] for gi in ATTN_ORDER] + [1]
    for i, gi in enumerate(ATTN_ORDER):
        q_ref, k_ref, v_ref = qkv_refs[3 * gi:3 * gi + 3]
        _attn_group(q_ref, k_ref, v_ref, bias_ref.at[gi], o_ref, part_ref,
                    dilation=dilations[i], next_dilation=dilations[i + 1],
                    src=i - 1 if i > 0 else None, dst=i if i < N_GROUPS - 1 else None)


def _attention(qkv_groups, bias_tabs):
    in_specs, args = [], []
    for qkv in qkv_groups:
        for which in range(3):
            in_specs.append(pl.BlockSpec(
                (None, SEQ, LANES), lambda b, hp, which=which: (b, 0, which * HEAD_PAIRS + hp)))
            args.append(qkv)
    in_specs.append(pl.BlockSpec((N_GROUPS, 3, None, 2 * BLOCK, 2 * BLOCK),
                                 lambda b, hp: (0, 0, hp, 0, 0)))
    return pl.pallas_call(
        _attn_kernel,
        grid=(BATCH, HEAD_PAIRS),
        in_specs=in_specs,
        out_specs=pl.BlockSpec((None, SEQ, LANES), lambda b, hp: (b, 0, hp)),
        out_shape=jax.ShapeDtypeStruct((BATCH, SEQ, GROUP_WIDTH), F32),
        scratch_shapes=[pltpu.VMEM((N_GROUPS - 1, 2, SEQ, LANES), F32)],
        compiler_params=_cparams(("arbitrary", "arbitrary")),
        name="dilated_attn",
    )(*args, bias_tabs)


def _ssm_param_kernel(lre_ref, lim_ref, ldt_ref, are_ref, aim_ref, fre_ref, fim_ref):
    lam_re = lre_ref[...]
    lam_im = lim_ref[...]
    dt = jnp.exp(ldt_ref[...])
    mag = jnp.exp(lam_re * dt)
    ang = lam_im * dt
    a_re = mag * jnp.cos(ang)
    a_im = mag * jnp.sin(ang)
    den = lam_re * lam_re + lam_im * lam_im
    are_ref[...] = a_re
    aim_ref[...] = a_im
    fre_ref[...] = ((a_re - 1) * lam_re + a_im * lam_im) / den
    fim_ref[...] = (a_im * lam_re - (a_re - 1) * lam_im) / den


def _ssm_params(lam_re, lam_im, log_dt):
    n = DEPTH * SSM_GROUPS
    shp = jax.ShapeDtypeStruct((n, SSM_STATE), F32)
    return pl.pallas_call(
        _ssm_param_kernel, out_shape=(shp, shp, shp, shp), name="ssm_discretise",
    )(lam_re.reshape(n, SSM_STATE), lam_im.reshape(n, SSM_STATE), log_dt.reshape(n, 1))


def _ssm_kernel(u_ref, bre_ref, bim_ref, cre_ref, cim_ref, are_ref, aim_ref, d_ref,
                y_ref, bre_s, bim_s, sre_ref, sim_ref, xre_ref, xim_ref):
    c = pl.program_id(1)

    @pl.when(c == 0)
    def _():
        xre_ref[...] = jnp.zeros_like(xre_ref)
        xim_ref[...] = jnp.zeros_like(xim_ref)

    def plane_rows(t, b):
        half, pair = divmod(t, HALF_TILES)
        return pair, pl.ds(half * BATCH + b, SSM_TC, stride=SUBLANES)

    def input_rows(t, b):
        half, pair = divmod(t, HALF_TILES)
        r0 = (half * BATCH + b) * SSM_PITCH
        return pair, slice(r0, r0 + SSM_TC)

    for b in range(BATCH):
        ub = u_ref[b].astype(BF16)
        bu_re = jnp.dot(ub, bre_ref[...], preferred_element_type=F32)
        bu_im = jnp.dot(ub, bim_ref[...], preferred_element_type=F32)
        for t in range(SLAB_TILES):
            pair, sl = input_rows(t, b)
            bre_s[pair, sl, :] = bu_re[:, t * LANES:(t + 1) * LANES]
            bim_s[pair, sl, :] = bu_im[:, t * LANES:(t + 1) * LANES]

    def coeff(ref, pair):
        return jnp.concatenate(
            [jnp.broadcast_to(ref[half * HALF_TILES + pair:half * HALF_TILES + pair + 1, :],
                              (BATCH, LANES)) for half in range(2)], axis=0)

    a_re = [coeff(are_ref, p) for p in range(HALF_TILES)]
    a_im = [coeff(aim_ref, p) for p in range(HALF_TILES)]

    def step(i, carry):
        xs = list(carry)
        rows = pl.ds(pl.multiple_of(i * SUBLANES, SUBLANES), SUBLANES)
        in_rows = pl.ds(i, SUBLANES, stride=SSM_PITCH)
        for p in range(HALF_TILES):
            xr, xi = xs[2 * p], xs[2 * p + 1]
            nr = a_re[p] * xr - a_im[p] * xi + bre_s[p, in_rows, :]
            ni = a_re[p] * xi + a_im[p] * xr + bim_s[p, in_rows, :]
            sre_ref[p, rows, :] = nr
            sim_ref[p, rows, :] = ni
            xs[2 * p], xs[2 * p + 1] = nr, ni
        return tuple(xs)

    init = []
    for p in range(HALF_TILES):
        init += [xre_ref[p], xim_ref[p]]
    fin = lax.fori_loop(0, SSM_TC, step, tuple(init), unroll=8)
    for p in range(HALF_TILES):
        xre_ref[p] = fin[2 * p]
        xim_ref[p] = fin[2 * p + 1]

    for b in range(BATCH):
        tiles = [plane_rows(t, b) for t in range(SLAB_TILES)]
        x_re = jnp.concatenate([sre_ref[pair, sl, :] for pair, sl in tiles], axis=1)
        x_im = jnp.concatenate([sim_ref[pair, sl, :] for pair, sl in tiles], axis=1)
        y = (_bdot(x_re, cre_ref[...]) + _bdot(x_im, cim_ref[...])
             + d_ref[...] * u_ref[b])
        y_ref[b] = jax.nn.gelu(y).astype(BF16)


def _ssm(u, bb_re, bb_im, cc_re, cc_im, a_re, a_im, d_skip, layer):
    slab = lambda s, c: (layer * SSM_SLABS + s, 0, 0)
    return pl.pallas_call(
        _ssm_kernel,
        grid=(SSM_SLABS, SEQ // SSM_TC),
        in_specs=[
            pl.BlockSpec((BATCH, SSM_TC, SLAB_CH), lambda s, c: (0, c, s)),
            pl.BlockSpec((None, SLAB_CH, SLAB_STATES), slab),
            pl.BlockSpec((None, SLAB_CH, SLAB_STATES), slab),
            pl.BlockSpec((None, SLAB_STATES, SLAB_CH), slab),
            pl.BlockSpec((None, SLAB_STATES, SLAB_CH), slab),
            pl.BlockSpec((None, SLAB_TILES, LANES), slab),
            pl.BlockSpec((None, SLAB_TILES, LANES), slab),
            pl.BlockSpec((None, 1, SLAB_CH), slab),
        ],
        out_specs=pl.BlockSpec((BATCH, SSM_TC, SLAB_CH), lambda s, c: (0, c, s)),
        out_shape=jax.ShapeDtypeStruct((BATCH, SEQ, SSM_WIDTH), BF16),
        scratch_shapes=[
            pltpu.VMEM((HALF_TILES, SUBLANES * SSM_PITCH, LANES), F32),
            pltpu.VMEM((HALF_TILES, SUBLANES * SSM_PITCH, LANES), F32),
            pltpu.VMEM((HALF_TILES, SUBLANES * SSM_TC, LANES), F32),
            pltpu.VMEM((HALF_TILES, SUBLANES * SSM_TC, LANES), F32),
            pltpu.VMEM((HALF_TILES, SUBLANES, LANES), F32),
            pltpu.VMEM((HALF_TILES, SUBLANES, LANES), F32),
        ],
        compiler_params=_cparams(("arbitrary", "arbitrary")),
        name="s5_scan",
    )(u, bb_re, bb_im, cc_re, cc_im, a_re, a_im, d_skip)


def _ssm_matrices(f_re, f_im, b_re, b_im, c_re, c_im):
    n = DEPTH * SSM_GROUPS
    f_re, f_im = f_re[..., None], f_im[..., None]
    b_re = b_re.reshape(n, SSM_STATE, SSM_GROUP)
    b_im = b_im.reshape(n, SSM_STATE, SSM_GROUP)
    bb_re = f_re * b_re - f_im * b_im
    bb_im = f_re * b_im + f_im * b_re
    gps = SSM_GROUPS // SSM_SLABS
    slabs = DEPTH * SSM_SLABS

    def block_diag(blocks):
        _, _, r, c = blocks.shape
        tiled = jnp.tile(blocks.reshape(slabs, gps * r, c), (1, 1, gps))
        on_diag = (np.arange(gps * r)[:, None] // r) == (np.arange(gps * c)[None, :] // c)
        return jnp.where(jnp.asarray(on_diag), tiled, 0.0)

    def in_map(bb):
        bb = bb.reshape(slabs, gps, SSM_STATE, SSM_GROUP)
        return block_diag(jnp.swapaxes(bb, 2, 3))

    def out_map(cc):
        cc = cc.reshape(slabs, gps, SSM_GROUP, SSM_STATE)
        return block_diag(jnp.swapaxes(cc, 2, 3))

    return (in_map(bb_re).astype(BF16), in_map(bb_im).astype(BF16),
            out_map(c_re).astype(BF16), out_map(-c_im).astype(BF16))


def kernel(x, c, w_ada, b_ada, norm_ffn1, w_ffn1_in, w_ffn1_out, norm_mix, w_in, rel_bias, lam_re, lam_im, log_dt, b_re, b_im, c_re, c_im, d_skip, w_glu, w_attn_proj, w_out, norm_ffn2, w_ffn2_in, w_ffn2_out, final_norm):
    c_pad = jnp.zeros((SUBLANES, D_MODEL), F32).at[:BATCH].set(c)
    mod = _ada(c_pad, w_ada, b_ada).reshape(DEPTH, SUBLANES, N_MOD, D_MODEL)
    bias_tabs = _attn_bias_tables(rel_bias)
    a_re, a_im, f_re, f_im = _ssm_params(lam_re, lam_im, log_dt)
    bb_re, bb_im, cc_re, cc_im = _ssm_matrices(f_re, f_im, b_re, b_im, c_re, c_im)
    a_re = a_re.reshape(DEPTH * SSM_SLABS, SLAB_TILES, LANES)
    a_im = a_im.reshape(DEPTH * SSM_SLABS, SLAB_TILES, LANES)
    d_skip = d_skip.reshape(DEPTH * SSM_SLABS, 1, SLAB_CH)
    for l in range(DEPTH):
        x = _ffn(x, mod, norm_ffn1, w_ffn1_in, w_ffn1_out, l, 0)

        *qkv_groups, u, gates = _inproj(x, mod, norm_mix, w_in, l)
        attn_o = _attention([qkv.reshape(BATCH, SEQ, QKV_WIDTH) for qkv in qkv_groups], bias_tabs)
        y_ssm = _ssm(u, bb_re, bb_im, cc_re, cc_im, a_re, a_im, d_skip, l)
        x = _ffn(x, mod, norm_ffn2, w_ffn2_in, w_ffn2_out, l, 6,
                 final_norm=final_norm if l == DEPTH - 1 else None,
                 mixer=(attn_o, y_ssm, gates, w_attn_proj, w_glu, w_out))
    return x
```

```python
import functools
import math

import jax
import jax.numpy as jnp
import numpy as np
from jax import lax
from jax.experimental import pallas as pl
from jax.experimental.pallas import tpu as pltpu

D_MODEL = 1024
BATCH = 4
SEQ = 4096
DEPTH = 2
HEAD_DIM = 64
HEADS_PER_GROUP = 8
ATTN_GROUPS = ((128, 1), (512, 4), (2048, 16))
N_GROUPS = len(ATTN_GROUPS)
GROUP_WIDTH = HEADS_PER_GROUP * HEAD_DIM
BLOCK = 128
REL_BUCKETS = 32
REL_MAX_DIST = 2048
NEG = -1e30
SSM_WIDTH = 512
SSM_GROUP = 16
SSM_GROUPS = 32
SSM_STATE = 64
D_FF = 2816
QKV_COLS = 3 * N_GROUPS * GROUP_WIDTH
IN_COLS = QKV_COLS + SSM_WIDTH + 2 * D_MODEL
N_MOD = 9
EPS = 1e-6

LANES = 128
SUBLANES = 8
MXU_DIM = 256
FREE_STRIDE = 4
VMEM_LIMIT = 56 * 1024 * 1024

ADA_TN = 1152
FFN_TM = 1024
FFN_TF = MXU_DIM
PROJ_TM = 512
QKV_WIDTH = 3 * GROUP_WIDTH
FFN_MIX_TM = 512
WEIGHT_CHUNKS = 8
HEAD_PAIRS = GROUP_WIDTH // LANES
ATTN_ORDER = (2, 1, 0)
LOG2E = math.log2(math.e)
QSCALE = HEAD_DIM ** -0.5 * LOG2E
N_QBLOCKS = SEQ // BLOCK
SSM_TC = 512
SSM_PITCH = SSM_TC + SUBLANES
SLAB_CH = MXU_DIM
SSM_SLABS = SSM_WIDTH // SLAB_CH
SLAB_STATES = (SSM_GROUPS // SSM_SLABS) * SSM_STATE
SLAB_TILES = SLAB_STATES // LANES
HALF_TILES = SLAB_TILES * BATCH // SUBLANES

BF16 = jnp.bfloat16
F32 = jnp.float32


def _cparams(sem):
    return pltpu.CompilerParams(dimension_semantics=sem, vmem_limit_bytes=VMEM_LIMIT)


def _first_grid_step():
    return (pl.program_id(0) == 0) & (pl.program_id(1) == 0)


def _weight_scratch(k, n, chunks):
    return [pltpu.VMEM((k, n), BF16), pltpu.VMEM((2, k // chunks, n), F32),
            pltpu.SemaphoreType.DMA((2,))]


def _load_weight(w_hbm, layer, w_ref, stage_ref, sem):
    rows = stage_ref.shape[1]
    n_chunks = w_ref.shape[0] // rows

    def chunk(i, slot):
        return pltpu.make_async_copy(w_hbm.at[layer, pl.ds(i * rows, rows), :],
                                     stage_ref.at[slot], sem.at[slot])

    chunk(0, 0).start()

    def body(i, carry):
        slot = i % 2

        @pl.when(i + 1 < n_chunks)
        def _():
            chunk(i + 1, 1 - slot).start()

        chunk(i, slot).wait()
        w_ref[pl.ds(pl.multiple_of(i * rows, rows), rows), :] = stage_ref[slot].astype(BF16)
        return carry

    lax.fori_loop(0, n_chunks, body, 0)


def _bdot(a, b):
    return jnp.dot(a.astype(BF16), b.astype(BF16), preferred_element_type=F32)


def _sigmoid(x):
    return 0.5 * jnp.tanh(0.5 * x) + 0.5


def _norm_mod(x, g, shift, scale):
    ms = jnp.mean(x * x, axis=-1, keepdims=True)
    y = x * lax.rsqrt(ms + EPS) * g
    return y * (1.0 + scale) + shift


def _ada_kernel(c_ref, w_ref, b_ref, o_ref):
    c = c_ref[...]
    ca = c * _sigmoid(c)
    o_ref[...] = _bdot(ca, w_ref[...]) + b_ref[...]


def _ada(c_pad, w_ada, b_ada):
    n = N_MOD * D_MODEL
    return pl.pallas_call(
        _ada_kernel,
        grid=(DEPTH, n // ADA_TN),
        in_specs=[
            pl.BlockSpec((SUBLANES, D_MODEL), lambda l, j: (0, 0)),
            pl.BlockSpec((None, D_MODEL, ADA_TN), lambda l, j: (l, 0, j)),
            pl.BlockSpec((None, 1, ADA_TN), lambda l, j: (l, 0, j)),
        ],
        out_specs=pl.BlockSpec((None, SUBLANES, ADA_TN), lambda l, j: (l, 0, j)),
        out_shape=jax.ShapeDtypeStruct((DEPTH, SUBLANES, n), F32),
        compiler_params=_cparams(("arbitrary", "arbitrary")),
        name="ada_mod",
    )(c_pad, w_ada, b_ada.reshape(DEPTH, 1, n))


def _ffn_kernel(x_ref, mod_ref, g_ref, win_hbm, wout_hbm, *rest, layer, row0, final, mixer):
    if mixer:
        (ao_ref, ys_ref, ga_ref, gs_ref, wap_hbm, wglu_hbm, wmix_hbm), rest = rest[:7], rest[7:]
    if final:
        fn_ref, rest = rest[0], rest[1:]
    o_ref, act_ref, win_ref, wout_ref, a_stage, b_stage, o_stage, sem = rest[:8]
    if mixer:
        (wap_ref, wap_stage, wap_sem, wglu_ref, wglu_stage, wglu_sem,
         wmix_ref, wmix_stage, wmix_sem) = rest[8:]
    n_chunks = D_FF // FFN_TF

    def chunk_copies(j, slot):
        lo, hi = j * FFN_TF, (j + 1) * FFN_TF
        return (pltpu.make_async_copy(win_hbm.at[layer, :, lo:hi], a_stage.at[slot], sem.at[0, slot]),
                pltpu.make_async_copy(win_hbm.at[layer, :, D_FF + lo:D_FF + hi], b_stage.at[slot],
                                      sem.at[1, slot]),
                pltpu.make_async_copy(wout_hbm.at[layer, lo:hi, :], o_stage.at[slot], sem.at[2, slot]))

    def tile(load_weights):
        x = x_ref[...]
        if load_weights:
            for copy in chunk_copies(0, 0):
                copy.start()
        if mixer:
            if load_weights:
                _load_weight(wap_hbm, layer, wap_ref, wap_stage, wap_sem)
                _load_weight(wglu_hbm, layer, wglu_ref, wglu_stage, wglu_sem)
                _load_weight(wmix_hbm, layer, wmix_ref, wmix_stage, wmix_sem)
            y_attn = _bdot(ao_ref[...], wap_ref[...])
            gl = jnp.dot(ys_ref[...], wglu_ref[...], preferred_element_type=F32)
            y_ssm = gl[:, :D_MODEL] * _sigmoid(gl[:, D_MODEL:])
            mixed = (_sigmoid(ga_ref[...].astype(F32)) * y_attn
                     + _sigmoid(gs_ref[...].astype(F32)) * y_ssm)
            x = x + mod_ref[5:6, :] * _bdot(mixed, wmix_ref[...])
        h = _norm_mod(x, g_ref[...], mod_ref[row0:row0 + 1, :], mod_ref[row0 + 1:row0 + 2, :])
        hb = h.astype(BF16)
        for j in range(n_chunks):
            lo, hi = j * FFN_TF, (j + 1) * FFN_TF
            if load_weights:
                if j + 1 < n_chunks:
                    for copy in chunk_copies(j + 1, (j + 1) % 2):
                        copy.start()
                for copy in chunk_copies(j, j % 2):
                    copy.wait()
                win_ref[:, lo:hi] = a_stage[j % 2].astype(BF16)
                win_ref[:, D_FF + lo:D_FF + hi] = b_stage[j % 2].astype(BF16)
                wout_ref[lo:hi, :] = o_stage[j % 2].astype(BF16)
            a = jnp.dot(hb, win_ref[:, lo:hi], preferred_element_type=F32)
            b = jnp.dot(hb, win_ref[:, D_FF + lo:D_FF + hi], preferred_element_type=F32)
            act_ref[:, lo:hi] = (a * _sigmoid(a) * b).astype(BF16)
        y = jnp.dot(act_ref[...], wout_ref[...], preferred_element_type=F32)
        out = x + (0.5 * mod_ref[row0 + 2:row0 + 3, :]) * y
        if final:
            ms = jnp.mean(out * out, axis=-1, keepdims=True)
            out = out * lax.rsqrt(ms + EPS) * fn_ref[...]
        o_ref[...] = out

    first = _first_grid_step()
    pl.when(first)(functools.partial(tile, True))
    pl.when(jnp.logical_not(first))(functools.partial(tile, False))


def _ffn(x, mod, norm_g, w_in, w_out, layer, row0, final_norm=None, mixer=None):
    final = final_norm is not None
    tm = FFN_MIX_TM if mixer else FFN_TM
    rows = lambda b, i: (b, i, 0)
    in_specs = [
        pl.BlockSpec((None, tm, D_MODEL), rows),
        pl.BlockSpec((None, None, N_MOD, D_MODEL), lambda b, i: (layer, b, 0, 0)),
        pl.BlockSpec((None, 1, D_MODEL), lambda b, i: (layer, 0, 0)),
        pl.BlockSpec(memory_space=pl.ANY),
        pl.BlockSpec(memory_space=pl.ANY),
    ]
    args = [x, mod, norm_g.reshape(DEPTH, 1, D_MODEL), w_in, w_out]
    scratch = [
        pltpu.VMEM((tm, D_FF), BF16),
        pltpu.VMEM((D_MODEL, 2 * D_FF), BF16),
        pltpu.VMEM((D_FF, D_MODEL), BF16),
        pltpu.VMEM((2, D_MODEL, FFN_TF), F32),
        pltpu.VMEM((2, D_MODEL, FFN_TF), F32),
        pltpu.VMEM((2, FFN_TF, D_MODEL), F32),
        pltpu.SemaphoreType.DMA((3, 2)),
    ]
    if mixer:
        attn_o, y_ssm, gates, w_ap, w_glu, w_mix = mixer
        in_specs += [
            pl.BlockSpec((None, tm, GROUP_WIDTH), rows),
            pl.BlockSpec((None, tm, SSM_WIDTH), rows),
            pl.BlockSpec((None, tm, D_MODEL), rows),
            pl.BlockSpec((None, tm, D_MODEL), lambda b, i: (b, i, 1)),
            pl.BlockSpec(memory_space=pl.ANY),
            pl.BlockSpec(memory_space=pl.ANY),
            pl.BlockSpec(memory_space=pl.ANY),
        ]
        args += [attn_o, y_ssm, gates, gates, w_ap, w_glu, w_mix]
        scratch += (_weight_scratch(GROUP_WIDTH, D_MODEL, WEIGHT_CHUNKS)
                    + _weight_scratch(SSM_WIDTH, 2 * D_MODEL, WEIGHT_CHUNKS)
                    + _weight_scratch(D_MODEL, D_MODEL, WEIGHT_CHUNKS))
    if final:
        in_specs.append(pl.BlockSpec((1, D_MODEL), lambda b, i: (0, 0)))
        args.append(final_norm.reshape(1, D_MODEL))
    return pl.pallas_call(
        functools.partial(_ffn_kernel, layer=layer, row0=row0, final=final, mixer=bool(mixer)),
        grid=(BATCH, SEQ // tm),
        in_specs=in_specs,
        out_specs=pl.BlockSpec((None, tm, D_MODEL), rows),
        out_shape=jax.ShapeDtypeStruct((BATCH, SEQ, D_MODEL), F32),
        scratch_shapes=scratch,
        compiler_params=_cparams(("arbitrary", "arbitrary")),
        name=("mix_" if mixer else "") + ("ffn_final" if final else "ffn"),
    )(*args)


def _rows_by_residue(hf, rows_ref, half_ref):
    tiles = D_MODEL // LANES
    out = {1: hf.astype(BF16)}
    if all(d == 1 for _, d in ATTN_GROUPS):
        return out
    for t in range(tiles):
        rows_ref[t] = hf[:, t * LANES:(t + 1) * LANES]

    def gather(ref, pieces):
        return jnp.concatenate(
            [jnp.concatenate([ref[t, pl.ds(start, count, stride=stride), :]
                              for start, count, stride in pieces], axis=0)
             for t in range(tiles)], axis=1)

    per_inner = PROJ_TM // FREE_STRIDE
    inner = gather(rows_ref, [(b, per_inner, FREE_STRIDE) for b in range(FREE_STRIDE)])
    for _, dilation in ATTN_GROUPS:
        if dilation == 1 or dilation in out:
            continue
        if dilation == FREE_STRIDE:
            out[dilation] = inner.astype(BF16)
            continue
        outer = dilation // FREE_STRIDE
        assert dilation % FREE_STRIDE == 0 and outer <= FREE_STRIDE
        for t in range(tiles):
            half_ref[t] = inner[:, t * LANES:(t + 1) * LANES]
        pieces = [(b * per_inner + a, PROJ_TM // dilation, outer)
                  for a, b in (divmod(r, FREE_STRIDE) for r in range(dilation))]
        out[dilation] = gather(half_ref, pieces).astype(BF16)
    return out


def _inproj_kernel(x_ref, mod_ref, g_ref, w_hbm, qkv0_ref, qkv1_ref, qkv2_ref, u_ref, gate_ref,
                   w_ref, w_stage, w_sem, rows_ref, half_ref, *, layer):
    qkv_refs = (qkv0_ref, qkv1_ref, qkv2_ref)
    n_segs = IN_COLS // GROUP_WIDTH
    order = sorted(range(n_segs),
                   key=lambda seg: ATTN_GROUPS[seg % N_GROUPS][1] if seg < 3 * N_GROUPS else 1)

    def seg_copy(k, slot):
        seg = order[k]
        return pltpu.make_async_copy(
            w_hbm.at[layer, :, seg * GROUP_WIDTH:(seg + 1) * GROUP_WIDTH], w_stage.at[slot],
            w_sem.at[slot])

    def tile(load_weights):
        hf = _norm_mod(x_ref[...], g_ref[...], mod_ref[3:4, :], mod_ref[4:5, :])
        h = _rows_by_residue(hf, rows_ref, half_ref)
        if load_weights:
            seg_copy(0, 0).start()
        for k, seg in enumerate(order):
            cols = slice(seg * GROUP_WIDTH, (seg + 1) * GROUP_WIDTH)
            if load_weights:
                if k + 1 < n_segs:
                    seg_copy(k + 1, (k + 1) % 2).start()
                seg_copy(k, k % 2).wait()
                w_ref[:, cols] = w_stage[k % 2].astype(BF16)
            if seg < 3 * N_GROUPS:
                which, g = divmod(seg, N_GROUPS)
                dilation = ATTN_GROUPS[g][1]
                ys = jnp.dot(h[dilation], w_ref[:, cols], preferred_element_type=F32)
                if which == 0:
                    ys = ys * QSCALE
                n = PROJ_TM // dilation
                for r in range(dilation):
                    qkv_refs[g][r, :, which * GROUP_WIDTH:(which + 1) * GROUP_WIDTH] = (
                        ys[r * n:(r + 1) * n, :].astype(BF16))
            else:
                ys = jnp.dot(h[1], w_ref[:, cols], preferred_element_type=F32)
                if seg == 3 * N_GROUPS:
                    u_ref[...] = ys
                else:
                    g0 = (seg - 3 * N_GROUPS - 1) * GROUP_WIDTH
                    gate_ref[:, g0:g0 + GROUP_WIDTH] = ys.astype(BF16)

    first = _first_grid_step()
    pl.when(first)(functools.partial(tile, True))
    pl.when(jnp.logical_not(first))(functools.partial(tile, False))


def _inproj(x, mod, norm_g, w_in, layer):
    rows = lambda b, i: (b, i, 0)
    qkv_specs, qkv_shapes = [], []
    for _, dilation in ATTN_GROUPS:
        qkv_specs.append(pl.BlockSpec((None, dilation, PROJ_TM // dilation, QKV_WIDTH),
                                      lambda b, i: (b, 0, i, 0)))
        qkv_shapes.append(jax.ShapeDtypeStruct((BATCH, dilation, SEQ // dilation, QKV_WIDTH), BF16))
    tiles = D_MODEL // LANES
    return pl.pallas_call(
        functools.partial(_inproj_kernel, layer=layer),
        grid=(BATCH, SEQ // PROJ_TM),
        in_specs=[
            pl.BlockSpec((None, PROJ_TM, D_MODEL), rows),
            pl.BlockSpec((None, None, N_MOD, D_MODEL), lambda b, i: (layer, b, 0, 0)),
            pl.BlockSpec((None, 1, D_MODEL), lambda b, i: (layer, 0, 0)),
            pl.BlockSpec(memory_space=pl.ANY),
        ],
        out_specs=(*qkv_specs,
                   pl.BlockSpec((None, PROJ_TM, SSM_WIDTH), rows),
                   pl.BlockSpec((None, PROJ_TM, 2 * D_MODEL), rows)),
        out_shape=(*qkv_shapes,
                   jax.ShapeDtypeStruct((BATCH, SEQ, SSM_WIDTH), F32),
                   jax.ShapeDtypeStruct((BATCH, SEQ, 2 * D_MODEL), BF16)),
        scratch_shapes=[
            pltpu.VMEM((D_MODEL, IN_COLS), BF16),
            pltpu.VMEM((2, D_MODEL, GROUP_WIDTH), F32),
            pltpu.SemaphoreType.DMA((2,)),
            pltpu.VMEM((tiles, PROJ_TM, LANES), F32),
            pltpu.VMEM((tiles, PROJ_TM, LANES), F32),
        ],
        compiler_params=_cparams(("arbitrary", "arbitrary")),
        name="inproj",
    )(x, mod, norm_g.reshape(DEPTH, 1, D_MODEL), w_in)


def _t5_bucket(dist):
    max_exact = REL_BUCKETS // 2
    d = np.maximum(dist, max_exact).astype(np.float32)
    large = max_exact + (np.log(d / max_exact) / np.log(REL_MAX_DIST / max_exact)
                         * (REL_BUCKETS - max_exact)).astype(np.int32)
    large = np.minimum(large, REL_BUCKETS - 1)
    return np.where(dist < max_exact, dist, large).astype(np.int32)


def _attn_bias_tables(rel_bias):
    qi = np.arange(BLOCK)[:, None]
    kj = np.arange(2 * BLOCK)[None, :]
    rel = BLOCK + qi - kj
    tabs = []
    for g, (window, dilation) in enumerate(ATTN_GROUPS):
        band = (rel >= 0) & (rel <= window // dilation)
        bucket = _t5_bucket(np.clip(rel, 0, None) * dilation)
        tbl = rel_bias[:, g * HEADS_PER_GROUP:(g + 1) * HEADS_PER_GROUP]
        onehot = jnp.asarray(bucket[None] == np.arange(REL_BUCKETS)[:, None, None], F32)
        bias = jnp.einsum('rqk,rh->hqk', onehot, tbl.astype(F32),
                          precision=lax.Precision.HIGHEST)
        general = jnp.where(band[None], bias * LOG2E, NEG)
        masked = jnp.full((HEADS_PER_GROUP, BLOCK, BLOCK), NEG, F32)
        first = jnp.concatenate([masked, general[:, :, BLOCK:]], axis=2)
        very_first = jnp.concatenate([general[:, :, BLOCK:], masked], axis=2)
        tabs.append(jnp.stack([very_first, first, general]))
    return jnp.stack(tabs).reshape(N_GROUPS, 3, HEAD_PAIRS, 2 * BLOCK, 2 * BLOCK)


def _attn_group(q_ref, k_ref, v_ref, bias_ref, o_ref, part_ref, *, dilation, next_dilation,
                src, dst):
    nb = SEQ // dilation // BLOCK
    lane = lax.broadcasted_iota(jnp.int32, (BLOCK, LANES), 1)
    head0 = lane < HEAD_DIM
    keep0 = jnp.where(head0, 1.0, 0.0).astype(BF16)
    keep1 = jnp.where(head0, 0.0, 1.0).astype(BF16)
    ones_cols = jnp.ones((2 * BLOCK, LANES), BF16)
    contract_last = (((1,), (1,)), ((), ()))

    step_out = dilation // next_dilation
    assert dilation % next_dilation == 0 and step_out <= FREE_STRIDE

    def next_rows(n):
        r = n // nb
        start = ((r % next_dilation) * (SEQ // next_dilation) + r // next_dilation
                 + (n % nb) * (BLOCK * step_out))
        return pl.ds(start, BLOCK, stride=step_out)

    def key_rows(n):
        return pl.ds(max(n - 1, 0) * BLOCK, 2 * BLOCK)

    def logits(n):
        qb = q_ref[pl.ds(n * BLOCK, BLOCK), :]
        qq = jnp.concatenate([qb * keep0, qb * keep1], axis=0)
        tab = 0 if n == 0 else (1 if n % nb == 0 else 2)
        l = lax.dot_general(qq, k_ref[key_rows(n), :], contract_last,
                            preferred_element_type=F32) + bias_ref[tab]
        return l, jnp.max(l, axis=1, keepdims=True)

    def weighted(n, l, m_rows):
        v_aug = jnp.concatenate([v_ref[key_rows(n), :], ones_cols], axis=1)
        p = jnp.exp2(l - m_rows).astype(BF16)
        r = jnp.dot(p, v_aug, preferred_element_type=F32)
        return (jnp.where(head0, r[:BLOCK, :LANES], r[BLOCK:, :LANES]),
                jnp.where(head0, r[:BLOCK, LANES:], r[BLOCK:, LANES:]))

    scores = [logits(n) for n in range(N_QBLOCKS)]
    for n, (l, m_rows) in enumerate(scores):
        acc, den = weighted(n, l, m_rows)
        m_blk = jnp.where(head0, m_rows[:BLOCK], m_rows[BLOCK:])
        if src is not None:
            own = pl.ds(n * BLOCK, BLOCK)
            out_prev, lse_prev = part_ref[src, 0, own, :], part_ref[src, 1, own, :]
            m_all = jnp.maximum(m_blk, lse_prev)
            w_own = jnp.exp2(m_blk - m_all)
            w_prev = jnp.exp2(lse_prev - m_all)
            acc = w_own * acc + w_prev * out_prev
            den = w_own * den + w_prev
            m_blk = m_all
        rows = next_rows(n)
        if dst is not None:
            part_ref[dst, 0, rows, :] = acc / den
            part_ref[dst, 1, rows, :] = m_blk + jnp.log2(den)
        else:
            o_ref[rows, :] = acc / den


def _attn_kernel(*refs):
    qkv_refs, (bias_ref, o_ref, part_ref) = refs[:3 * N_GROUPS], refs[3 * N_GROUPS:]
    dilations = [ATTN_GROUPS[gi][1] for gi in ATTN_ORDER] + [1]
    for i, gi in enumerate(ATTN_ORDER):
        q_ref, k_ref, v_ref = qkv_refs[3 * gi:3 * gi + 3]
        _attn_group(q_ref, k_ref, v_ref, bias_ref.at[gi], o_ref, part_ref,
                    dilation=dilations[i], next_dilation=dilations[i + 1],
                    src=i - 1 if i > 0 else None, dst=i if i < N_GROUPS - 1 else None)


def _attention(qkv_groups, bias_tabs):
    in_specs, args = [], []
    for qkv in qkv_groups:
        for which in range(3):
            in_specs.append(pl.BlockSpec(
                (None, SEQ, LANES), lambda b, hp, which=which: (b, 0, which * HEAD_PAIRS + hp)))
            args.append(qkv)
    in_specs.append(pl.BlockSpec((N_GROUPS, 3, None, 2 * BLOCK, 2 * BLOCK),
                                 lambda b, hp: (0, 0, hp, 0, 0)))
    return pl.pallas_call(
        _attn_kernel,
        grid=(BATCH, HEAD_PAIRS),
        in_specs=in_specs,
        out_specs=pl.BlockSpec((None, SEQ, LANES), lambda b, hp: (b, 0, hp)),
        out_shape=jax.ShapeDtypeStruct((BATCH, SEQ, GROUP_WIDTH), F32),
        scratch_shapes=[pltpu.VMEM((N_GROUPS - 1, 2, SEQ, LANES), F32)],
        compiler_params=_cparams(("arbitrary", "arbitrary")),
        name="dilated_attn",
    )(*args, bias_tabs)


def _ssm_param_kernel(lre_ref, lim_ref, ldt_ref, are_ref, aim_ref, fre_ref, fim_ref):
    lam_re = lre_ref[...]
    lam_im = lim_ref[...]
    dt = jnp.exp(ldt_ref[...])
    mag = jnp.exp(lam_re * dt)
    ang = lam_im * dt
    a_re = mag * jnp.cos(ang)
    a_im = mag * jnp.sin(ang)
    den = lam_re * lam_re + lam_im * lam_im
    are_ref[...] = a_re
    aim_ref[...] = a_im
    fre_ref[...] = ((a_re - 1) * lam_re + a_im * lam_im) / den
    fim_ref[...] = (a_im * lam_re - (a_re - 1) * lam_im) / den


def _ssm_params(lam_re, lam_im, log_dt):
    n = DEPTH * SSM_GROUPS
    shp = jax.ShapeDtypeStruct((n, SSM_STATE), F32)
    return pl.pallas_call(
        _ssm_param_kernel, out_shape=(shp, shp, shp, shp), name="ssm_discretise",
    )(lam_re.reshape(n, SSM_STATE), lam_im.reshape(n, SSM_STATE), log_dt.reshape(n, 1))


def _ssm_kernel(u_ref, bre_ref, bim_ref, cre_ref, cim_ref, are_ref, aim_ref, d_ref,
                y_ref, bre_s, bim_s, sre_ref, sim_ref, xre_ref, xim_ref):
    c = pl.program_id(1)

    @pl.when(c == 0)
    def _():
        xre_ref[...] = jnp.zeros_like(xre_ref)
        xim_ref[...] = jnp.zeros_like(xim_ref)

    def plane_rows(t, b):
        half, pair = divmod(t, HALF_TILES)
        return pair, pl.ds(half * BATCH + b, SSM_TC, stride=SUBLANES)

    def input_rows(t, b):
        half, pair = divmod(t, HALF_TILES)
        r0 = (half * BATCH + b) * SSM_PITCH
        return pair, slice(r0, r0 + SSM_TC)

    for b in range(BATCH):
        ub = u_ref[b].astype(BF16)
        bu_re = jnp.dot(ub, bre_ref[...], preferred_element_type=F32)
        bu_im = jnp.dot(ub, bim_ref[...], preferred_element_type=F32)
        for t in range(SLAB_TILES):
            pair, sl = input_rows(t, b)
            bre_s[pair, sl, :] = bu_re[:, t * LANES:(t + 1) * LANES]
            bim_s[pair, sl, :] = bu_im[:, t * LANES:(t + 1) * LANES]

    def coeff(ref, pair):
        return jnp.concatenate(
            [jnp.broadcast_to(ref[half * HALF_TILES + pair:half * HALF_TILES + pair + 1, :],
                              (BATCH, LANES)) for half in range(2)], axis=0)

    a_re = [coeff(are_ref, p) for p in range(HALF_TILES)]
    a_im = [coeff(aim_ref, p) for p in range(HALF_TILES)]

    def step(i, carry):
        xs = list(carry)
        rows = pl.ds(pl.multiple_of(i * SUBLANES, SUBLANES), SUBLANES)
        in_rows = pl.ds(i, SUBLANES, stride=SSM_PITCH)
        for p in range(HALF_TILES):
            xr, xi = xs[2 * p], xs[2 * p + 1]
            nr = a_re[p] * xr - a_im[p] * xi + bre_s[p, in_rows, :]
            ni = a_re[p] * xi + a_im[p] * xr + bim_s[p, in_rows, :]
            sre_ref[p, rows, :] = nr
            sim_ref[p, rows, :] = ni
            xs[2 * p], xs[2 * p + 1] = nr, ni
        return tuple(xs)

    init = []
    for p in range(HALF_TILES):
        init += [xre_ref[p], xim_ref[p]]
    fin = lax.fori_loop(0, SSM_TC, step, tuple(init), unroll=8)
    for p in range(HALF_TILES):
        xre_ref[p] = fin[2 * p]
        xim_ref[p] = fin[2 * p + 1]

    for b in range(BATCH):
        tiles = [plane_rows(t, b) for t in range(SLAB_TILES)]
        x_re = jnp.concatenate([sre_ref[pair, sl, :] for pair, sl in tiles], axis=1)
        x_im = jnp.concatenate([sim_ref[pair, sl, :] for pair, sl in tiles], axis=1)
        y = (_bdot(x_re, cre_ref[...]) + _bdot(x_im, cim_ref[...])
             + d_ref[...] * u_ref[b])
        y_ref[b] = jax.nn.gelu(y).astype(BF16)


def _ssm(u, bb_re, bb_im, cc_re, cc_im, a_re, a_im, d_skip, layer):
    slab = lambda s, c: (layer * SSM_SLABS + s, 0, 0)
    return pl.pallas_call(
        _ssm_kernel,
        grid=(SSM_SLABS, SEQ // SSM_TC),
        in_specs=[
            pl.BlockSpec((BATCH, SSM_TC, SLAB_CH), lambda s, c: (0, c, s)),
            pl.BlockSpec((None, SLAB_CH, SLAB_STATES), slab),
            pl.BlockSpec((None, SLAB_CH, SLAB_STATES), slab),
            pl.BlockSpec((None, SLAB_STATES, SLAB_CH), slab),
            pl.BlockSpec((None, SLAB_STATES, SLAB_CH), slab),
            pl.BlockSpec((None, SLAB_TILES, LANES), slab),
            pl.BlockSpec((None, SLAB_TILES, LANES), slab),
            pl.BlockSpec((None, 1, SLAB_CH), slab),
        ],
        out_specs=pl.BlockSpec((BATCH, SSM_TC, SLAB_CH), lambda s, c: (0, c, s)),
        out_shape=jax.ShapeDtypeStruct((BATCH, SEQ, SSM_WIDTH), BF16),
        scratch_shapes=[
            pltpu.VMEM((HALF_TILES, SUBLANES * SSM_PITCH, LANES), F32),
            pltpu.VMEM((HALF_TILES, SUBLANES * SSM_PITCH, LANES), F32),
            pltpu.VMEM((HALF_TILES, SUBLANES * SSM_TC, LANES), F32),
            pltpu.VMEM((HALF_TILES, SUBLANES * SSM_TC, LANES), F32),
            pltpu.VMEM((HALF_TILES, SUBLANES, LANES), F32),
            pltpu.VMEM((HALF_TILES, SUBLANES, LANES), F32),
        ],
        compiler_params=_cparams(("arbitrary", "arbitrary")),
        name="s5_scan",
    )(u, bb_re, bb_im, cc_re, cc_im, a_re, a_im, d_skip)


def _ssm_matrices(f_re, f_im, b_re, b_im, c_re, c_im):
    n = DEPTH * SSM_GROUPS
    f_re, f_im = f_re[..., None], f_im[..., None]
    b_re = b_re.reshape(n, SSM_STATE, SSM_GROUP)
    b_im = b_im.reshape(n, SSM_STATE, SSM_GROUP)
    bb_re = f_re * b_re - f_im * b_im
    bb_im = f_re * b_im + f_im * b_re
    gps = SSM_GROUPS // SSM_SLABS
    slabs = DEPTH * SSM_SLABS

    def block_diag(blocks):
        _, _, r, c = blocks.shape
        tiled = jnp.tile(blocks.reshape(slabs, gps * r, c), (1, 1, gps))
        on_diag = (np.arange(gps * r)[:, None] // r) == (np.arange(gps * c)[None, :] // c)
        return jnp.where(jnp.asarray(on_diag), tiled, 0.0)

    def in_map(bb):
        bb = bb.reshape(slabs, gps, SSM_STATE, SSM_GROUP)
        return block_diag(jnp.swapaxes(bb, 2, 3))

    def out_map(cc):
        cc = cc.reshape(slabs, gps, SSM_GROUP, SSM_STATE)
        return block_diag(jnp.swapaxes(cc, 2, 3))

    return (in_map(bb_re).astype(BF16), in_map(bb_im).astype(BF16),
            out_map(c_re).astype(BF16), out_map(-c_im).astype(BF16))


def kernel(x, c, w_ada, b_ada, norm_ffn1, w_ffn1_in, w_ffn1_out, norm_mix, w_in, rel_bias, lam_re, lam_im, log_dt, b_re, b_im, c_re, c_im, d_skip, w_glu, w_attn_proj, w_out, norm_ffn2, w_ffn2_in, w_ffn2_out, final_norm):
    c_pad = jnp.zeros((SUBLANES, D_MODEL), F32).at[:BATCH].set(c)
    mod = _ada(c_pad, w_ada, b_ada).reshape(DEPTH, SUBLANES, N_MOD, D_MODEL)
    bias_tabs = _attn_bias_tables(rel_bias)
    a_re, a_im, f_re, f_im = _ssm_params(lam_re, lam_im, log_dt)
    bb_re, bb_im, cc_re, cc_im = _ssm_matrices(f_re, f_im, b_re, b_im, c_re, c_im)
    a_re = a_re.reshape(DEPTH * SSM_SLABS, SLAB_TILES, LANES)
    a_im = a_im.reshape(DEPTH * SSM_SLABS, SLAB_TILES, LANES)
    d_skip = d_skip.reshape(DEPTH * SSM_SLABS, 1, SLAB_CH)
    for l in range(DEPTH):
        x = _ffn(x, mod, norm_ffn1, w_ffn1_in, w_ffn1_out, l, 0)

        *qkv_groups, u, gates = _inproj(x, mod, norm_mix, w_in, l)
        attn_o = _attention([qkv.reshape(BATCH, SEQ, QKV_WIDTH) for qkv in qkv_groups], bias_tabs)
        y_ssm = _ssm(u, bb_re, bb_im, cc_re, cc_im, a_re, a_im, d_skip, l)
        x = _ffn(x, mod, norm_ffn2, w_ffn2_in, w_ffn2_out, l, 6,
                 final_norm=final_norm if l == DEPTH - 1 else None,
                 mixer=(attn_o, y_ssm, gates, w_attn_proj, w_glu, w_out))
    return x
```

```python
import functools
import math

import jax
import jax.numpy as jnp
import numpy as np
from jax import lax
from jax.experimental import pallas as pl
from jax.experimental.pallas import tpu as pltpu

D_MODEL = 1024
BATCH = 4
SEQ = 4096
DEPTH = 2
HEAD_DIM = 64
HEADS_PER_GROUP = 8
ATTN_GROUPS = ((128, 1), (512, 4), (2048, 16))
N_GROUPS = len(ATTN_GROUPS)
GROUP_WIDTH = HEADS_PER_GROUP * HEAD_DIM
BLOCK = 128
REL_BUCKETS = 32
REL_MAX_DIST = 2048
NEG = -1e30
SSM_WIDTH = 512
SSM_GROUP = 16
SSM_GROUPS = 32
SSM_STATE = 64
D_FF = 2816
QKV_COLS = 3 * N_GROUPS * GROUP_WIDTH
IN_COLS = QKV_COLS + SSM_WIDTH + 2 * D_MODEL
N_MOD = 9
EPS = 1e-6

LANES = 128
SUBLANES = 8
MXU_DIM = 256
FREE_STRIDE = 4
VMEM_LIMIT = 56 * 1024 * 1024

ADA_TN = 1152
FFN_TM = 1024
FFN_TF = MXU_DIM
PROJ_TM = 512
QKV_WIDTH = 3 * GROUP_WIDTH
FFN_MIX_TM = 512
WEIGHT_CHUNKS = 8
HEAD_PAIRS = GROUP_WIDTH // LANES
ATTN_ORDER = (2, 1, 0)
LOG2E = math.log2(math.e)
QSCALE = HEAD_DIM ** -0.5 * LOG2E
N_QBLOCKS = SEQ // BLOCK
SSM_TC = 512
SSM_PITCH = SSM_TC + FREE_STRIDE
SLAB_CH = MXU_DIM
SSM_SLABS = SSM_WIDTH // SLAB_CH
SLAB_STATES = (SSM_GROUPS // SSM_SLABS) * SSM_STATE
SLAB_TILES = SLAB_STATES // LANES
HALF_TILES = SLAB_TILES * BATCH // SUBLANES

BF16 = jnp.bfloat16
F32 = jnp.float32


def _cparams(sem):
    return pltpu.CompilerParams(dimension_semantics=sem, vmem_limit_bytes=VMEM_LIMIT)


def _first_grid_step():
    return (pl.program_id(0) == 0) & (pl.program_id(1) == 0)


def _weight_scratch(k, n, chunks):
    return [pltpu.VMEM((k, n), BF16), pltpu.VMEM((2, k // chunks, n), F32),
            pltpu.SemaphoreType.DMA((2,))]


def _load_weight(w_hbm, layer, w_ref, stage_ref, sem):
    rows = stage_ref.shape[1]
    n_chunks = w_ref.shape[0] // rows

    def chunk(i, slot):
        return pltpu.make_async_copy(w_hbm.at[layer, pl.ds(i * rows, rows), :],
                                     stage_ref.at[slot], sem.at[slot])

    chunk(0, 0).start()

    def body(i, carry):
        slot = i % 2

        @pl.when(i + 1 < n_chunks)
        def _():
            chunk(i + 1, 1 - slot).start()

        chunk(i, slot).wait()
        w_ref[pl.ds(pl.multiple_of(i * rows, rows), rows), :] = stage_ref[slot].astype(BF16)
        return carry

    lax.fori_loop(0, n_chunks, body, 0)


def _bdot(a, b):
    return jnp.dot(a.astype(BF16), b.astype(BF16), preferred_element_type=F32)


def _sigmoid(x):
    return 0.5 * jnp.tanh(0.5 * x) + 0.5


def _norm_mod(x, g, shift, scale):
    ms = jnp.mean(x * x, axis=-1, keepdims=True)
    y = x * lax.rsqrt(ms + EPS) * g
    return y * (1.0 + scale) + shift


def _ada_kernel(c_ref, w_ref, b_ref, o_ref):
    c = c_ref[...]
    ca = c * _sigmoid(c)
    o_ref[...] = _bdot(ca, w_ref[...]) + b_ref[...]


def _ada(c_pad, w_ada, b_ada):
    n = N_MOD * D_MODEL
    return pl.pallas_call(
        _ada_kernel,
        grid=(DEPTH, n // ADA_TN),
        in_specs=[
            pl.BlockSpec((SUBLANES, D_MODEL), lambda l, j: (0, 0)),
            pl.BlockSpec((None, D_MODEL, ADA_TN), lambda l, j: (l, 0, j)),
            pl.BlockSpec((None, 1, ADA_TN), lambda l, j: (l, 0, j)),
        ],
        out_specs=pl.BlockSpec((None, SUBLANES, ADA_TN), lambda l, j: (l, 0, j)),
        out_shape=jax.ShapeDtypeStruct((DEPTH, SUBLANES, n), F32),
        compiler_params=_cparams(("arbitrary", "arbitrary")),
        name="ada_mod",
    )(c_pad, w_ada, b_ada.reshape(DEPTH, 1, n))


def _ffn_kernel(x_ref, mod_ref, g_ref, win_hbm, wout_hbm, *rest, layer, row0, final, mixer):
    if mixer:
        (ao_ref, ys_ref, ga_ref, gs_ref, wap_hbm, wglu_hbm, wmix_hbm), rest = rest[:7], rest[7:]
    if final:
        fn_ref, rest = rest[0], rest[1:]
    o_ref, act_ref, win_ref, wout_ref, a_stage, b_stage, o_stage, sem = rest[:8]
    if mixer:
        (wap_ref, wap_stage, wap_sem, wglu_ref, wglu_stage, wglu_sem,
         wmix_ref, wmix_stage, wmix_sem) = rest[8:]
    n_chunks = D_FF // FFN_TF

    def chunk_copies(j, slot):
        lo, hi = j * FFN_TF, (j + 1) * FFN_TF
        return (pltpu.make_async_copy(win_hbm.at[layer, :, lo:hi], a_stage.at[slot], sem.at[0, slot]),
                pltpu.make_async_copy(win_hbm.at[layer, :, D_FF + lo:D_FF + hi], b_stage.at[slot],
                                      sem.at[1, slot]),
                pltpu.make_async_copy(wout_hbm.at[layer, lo:hi, :], o_stage.at[slot], sem.at[2, slot]))

    def tile(load_weights):
        x = x_ref[...]
        if load_weights:
            for copy in chunk_copies(0, 0):
                copy.start()
        if mixer:
            if load_weights:
                _load_weight(wap_hbm, layer, wap_ref, wap_stage, wap_sem)
                _load_weight(wglu_hbm, layer, wglu_ref, wglu_stage, wglu_sem)
                _load_weight(wmix_hbm, layer, wmix_ref, wmix_stage, wmix_sem)
            y_attn = _bdot(ao_ref[...], wap_ref[...])
            gl = jnp.dot(ys_ref[...], wglu_ref[...], preferred_element_type=F32)
            y_ssm = gl[:, :D_MODEL] * _sigmoid(gl[:, D_MODEL:])
            mixed = (_sigmoid(ga_ref[...].astype(F32)) * y_attn
                     + _sigmoid(gs_ref[...].astype(F32)) * y_ssm)
            x = x + mod_ref[5:6, :] * _bdot(mixed, wmix_ref[...])
        h = _norm_mod(x, g_ref[...], mod_ref[row0:row0 + 1, :], mod_ref[row0 + 1:row0 + 2, :])
        hb = h.astype(BF16)
        for j in range(n_chunks):
            lo, hi = j * FFN_TF, (j + 1) * FFN_TF
            if load_weights:
                if j + 1 < n_chunks:
                    for copy in chunk_copies(j + 1, (j + 1) % 2):
                        copy.start()
                for copy in chunk_copies(j, j % 2):
                    copy.wait()
                win_ref[:, lo:hi] = a_stage[j % 2].astype(BF16)
                win_ref[:, D_FF + lo:D_FF + hi] = b_stage[j % 2].astype(BF16)
                wout_ref[lo:hi, :] = o_stage[j % 2].astype(BF16)
            a = jnp.dot(hb, win_ref[:, lo:hi], preferred_element_type=F32)
            b = jnp.dot(hb, win_ref[:, D_FF + lo:D_FF + hi], preferred_element_type=F32)
            act_ref[:, lo:hi] = (a * _sigmoid(a) * b).astype(BF16)
        y = jnp.dot(act_ref[...], wout_ref[...], preferred_element_type=F32)
        out = x + (0.5 * mod_ref[row0 + 2:row0 + 3, :]) * y
        if final:
            ms = jnp.mean(out * out, axis=-1, keepdims=True)
            out = out * lax.rsqrt(ms + EPS) * fn_ref[...]
        o_ref[...] = out

    first = _first_grid_step()
    pl.when(first)(functools.partial(tile, True))
    pl.when(jnp.logical_not(first))(functools.partial(tile, False))


def _ffn(x, mod, norm_g, w_in, w_out, layer, row0, final_norm=None, mixer=None):
    final = final_norm is not None
    tm = FFN_MIX_TM if mixer else FFN_TM
    rows = lambda b, i: (b, i, 0)
    in_specs = [
        pl.BlockSpec((None, tm, D_MODEL), rows),
        pl.BlockSpec((None, None, N_MOD, D_MODEL), lambda b, i: (layer, b, 0, 0)),
        pl.BlockSpec((None, 1, D_MODEL), lambda b, i: (layer, 0, 0)),
        pl.BlockSpec(memory_space=pl.ANY),
        pl.BlockSpec(memory_space=pl.ANY),
    ]
    args = [x, mod, norm_g.reshape(DEPTH, 1, D_MODEL), w_in, w_out]
    scratch = [
        pltpu.VMEM((tm, D_FF), BF16),
        pltpu.VMEM((D_MODEL, 2 * D_FF), BF16),
        pltpu.VMEM((D_FF, D_MODEL), BF16),
        pltpu.VMEM((2, D_MODEL, FFN_TF), F32),
        pltpu.VMEM((2, D_MODEL, FFN_TF), F32),
        pltpu.VMEM((2, FFN_TF, D_MODEL), F32),
        pltpu.SemaphoreType.DMA((3, 2)),
    ]
    if mixer:
        attn_o, y_ssm, gates, w_ap, w_glu, w_mix = mixer
        in_specs += [
            pl.BlockSpec((None, tm, GROUP_WIDTH), rows),
            pl.BlockSpec((None, tm, SSM_WIDTH), rows),
            pl.BlockSpec((None, tm, D_MODEL), rows),
            pl.BlockSpec((None, tm, D_MODEL), lambda b, i: (b, i, 1)),
            pl.BlockSpec(memory_space=pl.ANY),
            pl.BlockSpec(memory_space=pl.ANY),
            pl.BlockSpec(memory_space=pl.ANY),
        ]
        args += [attn_o, y_ssm, gates, gates, w_ap, w_glu, w_mix]
        scratch += (_weight_scratch(GROUP_WIDTH, D_MODEL, WEIGHT_CHUNKS)
                    + _weight_scratch(SSM_WIDTH, 2 * D_MODEL, WEIGHT_CHUNKS)
                    + _weight_scratch(D_MODEL, D_MODEL, WEIGHT_CHUNKS))
    if final:
        in_specs.append(pl.BlockSpec((1, D_MODEL), lambda b, i: (0, 0)))
        args.append(final_norm.reshape(1, D_MODEL))
    return pl.pallas_call(
        functools.partial(_ffn_kernel, layer=layer, row0=row0, final=final, mixer=bool(mixer)),
        grid=(BATCH, SEQ // tm),
        in_specs=in_specs,
        out_specs=pl.BlockSpec((None, tm, D_MODEL), rows),
        out_shape=jax.ShapeDtypeStruct((BATCH, SEQ, D_MODEL), F32),
        scratch_shapes=scratch,
        compiler_params=_cparams(("arbitrary", "arbitrary")),
        name=("mix_" if mixer else "") + ("ffn_final" if final else "ffn"),
    )(*args)


def _rows_by_residue(hf, rows_ref, half_ref):
    tiles = D_MODEL // LANES
    out = {1: hf.astype(BF16)}
    if all(d == 1 for _, d in ATTN_GROUPS):
        return out
    for t in range(tiles):
        rows_ref[t] = hf[:, t * LANES:(t + 1) * LANES]

    def gather(ref, pieces):
        return jnp.concatenate(
            [jnp.concatenate([ref[t, pl.ds(start, count, stride=stride), :]
                              for start, count, stride in pieces], axis=0)
             for t in range(tiles)], axis=1)

    per_inner = PROJ_TM // FREE_STRIDE
    inner = gather(rows_ref, [(b, per_inner, FREE_STRIDE) for b in range(FREE_STRIDE)])
    for _, dilation in ATTN_GROUPS:
        if dilation == 1 or dilation in out:
            continue
        if dilation == FREE_STRIDE:
            out[dilation] = inner.astype(BF16)
            continue
        outer = dilation // FREE_STRIDE
        assert dilation % FREE_STRIDE == 0 and outer <= FREE_STRIDE
        for t in range(tiles):
            half_ref[t] = inner[:, t * LANES:(t + 1) * LANES]
        pieces = [(b * per_inner + a, PROJ_TM // dilation, outer)
                  for a, b in (divmod(r, FREE_STRIDE) for r in range(dilation))]
        out[dilation] = gather(half_ref, pieces).astype(BF16)
    return out


def _inproj_kernel(x_ref, mod_ref, g_ref, w_hbm, qkv0_ref, qkv1_ref, qkv2_ref, u_ref, gate_ref,
                   w_ref, w_stage, w_sem, rows_ref, half_ref, *, layer):
    qkv_refs = (qkv0_ref, qkv1_ref, qkv2_ref)
    n_segs = IN_COLS // GROUP_WIDTH
    order = sorted(range(n_segs),
                   key=lambda seg: ATTN_GROUPS[seg % N_GROUPS][1] if seg < 3 * N_GROUPS else 1)

    def seg_copy(k, slot):
        seg = order[k]
        return pltpu.make_async_copy(
            w_hbm.at[layer, :, seg * GROUP_WIDTH:(seg + 1) * GROUP_WIDTH], w_stage.at[slot],
            w_sem.at[slot])

    def tile(load_weights):
        hf = _norm_mod(x_ref[...], g_ref[...], mod_ref[3:4, :], mod_ref[4:5, :])
        h = _rows_by_residue(hf, rows_ref, half_ref)
        if load_weights:
            seg_copy(0, 0).start()
        for k, seg in enumerate(order):
            cols = slice(seg * GROUP_WIDTH, (seg + 1) * GROUP_WIDTH)
            if load_weights:
                if k + 1 < n_segs:
                    seg_copy(k + 1, (k + 1) % 2).start()
                seg_copy(k, k % 2).wait()
                w_ref[:, cols] = w_stage[k % 2].astype(BF16)
            if seg < 3 * N_GROUPS:
                which, g = divmod(seg, N_GROUPS)
                dilation = ATTN_GROUPS[g][1]
                ys = jnp.dot(h[dilation], w_ref[:, cols], preferred_element_type=F32)
                if which == 0:
                    ys = ys * QSCALE
                n = PROJ_TM // dilation
                for r in range(dilation):
                    qkv_refs[g][r, :, which * GROUP_WIDTH:(which + 1) * GROUP_WIDTH] = (
                        ys[r * n:(r + 1) * n, :].astype(BF16))
            else:
                ys = jnp.dot(h[1], w_ref[:, cols], preferred_element_type=F32)
                if seg == 3 * N_GROUPS:
                    u_ref[...] = ys
                else:
                    g0 = (seg - 3 * N_GROUPS - 1) * GROUP_WIDTH
                    gate_ref[:, g0:g0 + GROUP_WIDTH] = ys.astype(BF16)

    first = _first_grid_step()
    pl.when(first)(functools.partial(tile, True))
    pl.when(jnp.logical_not(first))(functools.partial(tile, False))


def _inproj(x, mod, norm_g, w_in, layer):
    rows = lambda b, i: (b, i, 0)
    qkv_specs, qkv_shapes = [], []
    for _, dilation in ATTN_GROUPS:
        qkv_specs.append(pl.BlockSpec((None, dilation, PROJ_TM // dilation, QKV_WIDTH),
                                      lambda b, i: (b, 0, i, 0)))
        qkv_shapes.append(jax.ShapeDtypeStruct((BATCH, dilation, SEQ // dilation, QKV_WIDTH), BF16))
    tiles = D_MODEL // LANES
    return pl.pallas_call(
        functools.partial(_inproj_kernel, layer=layer),
        grid=(BATCH, SEQ // PROJ_TM),
        in_specs=[
            pl.BlockSpec((None, PROJ_TM, D_MODEL), rows),
            pl.BlockSpec((None, None, N_MOD, D_MODEL), lambda b, i: (layer, b, 0, 0)),
            pl.BlockSpec((None, 1, D_MODEL), lambda b, i: (layer, 0, 0)),
            pl.BlockSpec(memory_space=pl.ANY),
        ],
        out_specs=(*qkv_specs,
                   pl.BlockSpec((None, PROJ_TM, SSM_WIDTH), rows),
                   pl.BlockSpec((None, PROJ_TM, 2 * D_MODEL), rows)),
        out_shape=(*qkv_shapes,
                   jax.ShapeDtypeStruct((BATCH, SEQ, SSM_WIDTH), F32),
                   jax.ShapeDtypeStruct((BATCH, SEQ, 2 * D_MODEL), BF16)),
        scratch_shapes=[
            pltpu.VMEM((D_MODEL, IN_COLS), BF16),
            pltpu.VMEM((2, D_MODEL, GROUP_WIDTH), F32),
            pltpu.SemaphoreType.DMA((2,)),
            pltpu.VMEM((tiles, PROJ_TM, LANES), F32),
            pltpu.VMEM((tiles, PROJ_TM, LANES), F32),
        ],
        compiler_params=_cparams(("arbitrary", "arbitrary")),
        name="inproj",
    )(x, mod, norm_g.reshape(DEPTH, 1, D_MODEL), w_in)


def _t5_bucket(dist):
    max_exact = REL_BUCKETS // 2
    d = np.maximum(dist, max_exact).astype(np.float32)
    large = max_exact + (np.log(d / max_exact) / np.log(REL_MAX_DIST / max_exact)
                         * (REL_BUCKETS - max_exact)).astype(np.int32)
    large = np.minimum(large, REL_BUCKETS - 1)
    return np.where(dist < max_exact, dist, large).astype(np.int32)


def _attn_bias_tables(rel_bias):
    qi = np.arange(BLOCK)[:, None]
    kj = np.arange(2 * BLOCK)[None, :]
    rel = BLOCK + qi - kj
    tabs = []
    for g, (window, dilation) in enumerate(ATTN_GROUPS):
        band = (rel >= 0) & (rel <= window // dilation)
        bucket = _t5_bucket(np.clip(rel, 0, None) * dilation)
        tbl = rel_bias[:, g * HEADS_PER_GROUP:(g + 1) * HEADS_PER_GROUP]
        onehot = jnp.asarray(bucket[None] == np.arange(REL_BUCKETS)[:, None, None], F32)
        bias = jnp.einsum('rqk,rh->hqk', onehot, tbl.astype(F32),
                          precision=lax.Precision.HIGHEST)
        general = jnp.where(band[None], bias * LOG2E, NEG)
        masked = jnp.full((HEADS_PER_GROUP, BLOCK, BLOCK), NEG, F32)
        first = jnp.concatenate([masked, general[:, :, BLOCK:]], axis=2)
        very_first = jnp.concatenate([general[:, :, BLOCK:], masked], axis=2)
        tabs.append(jnp.stack([very_first, first, general]))
    return jnp.stack(tabs).reshape(N_GROUPS, 3, HEAD_PAIRS, 2 * BLOCK, 2 * BLOCK)


def _attn_group(q_ref, k_ref, v_ref, bias_ref, o_ref, part_ref, *, dilation, next_dilation,
                src, dst):
    nb = SEQ // dilation // BLOCK
    lane = lax.broadcasted_iota(jnp.int32, (BLOCK, LANES), 1)
    head0 = lane < HEAD_DIM
    keep0 = jnp.where(head0, 1.0, 0.0).astype(BF16)
    keep1 = jnp.where(head0, 0.0, 1.0).astype(BF16)
    ones_cols = jnp.ones((2 * BLOCK, LANES), BF16)
    contract_last = (((1,), (1,)), ((), ()))

    step_out = dilation // next_dilation
    assert dilation % next_dilation == 0 and step_out <= FREE_STRIDE

    def next_rows(n):
        r = n // nb
        start = ((r % next_dilation) * (SEQ // next_dilation) + r // next_dilation
                 + (n % nb) * (BLOCK * step_out))
        return pl.ds(start, BLOCK, stride=step_out)

    def key_rows(n):
        return pl.ds(max(n - 1, 0) * BLOCK, 2 * BLOCK)

    def logits(n):
        qb = q_ref[pl.ds(n * BLOCK, BLOCK), :]
        qq = jnp.concatenate([qb * keep0, qb * keep1], axis=0)
        tab = 0 if n == 0 else (1 if n % nb == 0 else 2)
        l = lax.dot_general(qq, k_ref[key_rows(n), :], contract_last,
                            preferred_element_type=F32) + bias_ref[tab]
        return l, jnp.max(l, axis=1, keepdims=True)

    def weighted(n, l, m_rows):
        v_aug = jnp.concatenate([v_ref[key_rows(n), :], ones_cols], axis=1)
        p = jnp.exp2(l - m_rows).astype(BF16)
        r = jnp.dot(p, v_aug, preferred_element_type=F32)
        return (jnp.where(head0, r[:BLOCK, :LANES], r[BLOCK:, :LANES]),
                jnp.where(head0, r[:BLOCK, LANES:], r[BLOCK:, LANES:]))

    scores = [logits(n) for n in range(N_QBLOCKS)]
    for n, (l, m_rows) in enumerate(scores):
        acc, den = weighted(n, l, m_rows)
        m_blk = jnp.where(head0, m_rows[:BLOCK], m_rows[BLOCK:])
        if src is not None:
            own = pl.ds(n * BLOCK, BLOCK)
            out_prev, lse_prev = part_ref[src, 0, own, :], part_ref[src, 1, own, :]
            m_all = jnp.maximum(m_blk, lse_prev)
            w_own = jnp.exp2(m_blk - m_all)
            w_prev = jnp.exp2(lse_prev - m_all)
            acc = w_own * acc + w_prev * out_prev
            den = w_own * den + w_prev
            m_blk = m_all
        rows = next_rows(n)
        if dst is not None:
            part_ref[dst, 0, rows, :] = acc / den
            part_ref[dst, 1, rows, :] = m_blk + jnp.log2(den)
        else:
            o_ref[rows, :] = acc / den


def _attn_kernel(*refs):
    qkv_refs, (bias_ref, o_ref, part_ref) = refs[:3 * N_GROUPS], refs[3 * N_GROUPS:]
    dilations = [ATTN_GROUPS[gi][1] for gi in ATTN_ORDER] + [1]
    for i, gi in enumerate(ATTN_ORDER):
        q_ref, k_ref, v_ref = qkv_refs[3 * gi:3 * gi + 3]
        _attn_group(q_ref, k_ref, v_ref, bias_ref.at[gi], o_ref, part_ref,
                    dilation=dilations[i], next_dilation=dilations[i + 1],
                    src=i - 1 if i > 0 else None, dst=i if i < N_GROUPS - 1 else None)


def _attention(qkv_groups, bias_tabs):
    in_specs, args = [], []
    for qkv in qkv_groups:
        for which in range(3):
            in_specs.append(pl.BlockSpec(
                (None, SEQ, LANES), lambda b, hp, which=which: (b, 0, which * HEAD_PAIRS + hp)))
            args.append(qkv)
    in_specs.append(pl.BlockSpec((N_GROUPS, 3, None, 2 * BLOCK, 2 * BLOCK),
                                 lambda b, hp: (0, 0, hp, 0, 0)))
    return pl.pallas_call(
        _attn_kernel,
        grid=(BATCH, HEAD_PAIRS),
        in_specs=in_specs,
        out_specs=pl.BlockSpec((None, SEQ, LANES), lambda b, hp: (b, 0, hp)),
        out_shape=jax.ShapeDtypeStruct((BATCH, SEQ, GROUP_WIDTH), F32),
        scratch_shapes=[pltpu.VMEM((N_GROUPS - 1, 2, SEQ, LANES), F32)],
        compiler_params=_cparams(("arbitrary", "arbitrary")),
        name="dilated_attn",
    )(*args, bias_tabs)


def _ssm_param_kernel(lre_ref, lim_ref, ldt_ref, are_ref, aim_ref, fre_ref, fim_ref):
    lam_re = lre_ref[...]
    lam_im = lim_ref[...]
    dt = jnp.exp(ldt_ref[...])
    mag = jnp.exp(lam_re * dt)
    ang = lam_im * dt
    a_re = mag * jnp.cos(ang)
    a_im = mag * jnp.sin(ang)
    den = lam_re * lam_re + lam_im * lam_im
    are_ref[...] = a_re
    aim_ref[...] = a_im
    fre_ref[...] = ((a_re - 1) * lam_re + a_im * lam_im) / den
    fim_ref[...] = (a_im * lam_re - (a_re - 1) * lam_im) / den


def _ssm_params(lam_re, lam_im, log_dt):
    n = DEPTH * SSM_GROUPS
    shp = jax.ShapeDtypeStruct((n, SSM_STATE), F32)
    return pl.pallas_call(
        _ssm_param_kernel, out_shape=(shp, shp, shp, shp), name="ssm_discretise",
    )(lam_re.reshape(n, SSM_STATE), lam_im.reshape(n, SSM_STATE), log_dt.reshape(n, 1))


def _ssm_kernel(u_ref, bre_ref, bim_ref, cre_ref, cim_ref, are_ref, aim_ref, d_ref,
                y_ref, bre_s, bim_s, sre_ref, sim_ref, xre_ref, xim_ref):
    c = pl.program_id(1)

    @pl.when(c == 0)
    def _():
        xre_ref[...] = jnp.zeros_like(xre_ref)
        xim_ref[...] = jnp.zeros_like(xim_ref)

    def plane_rows(t, b):
        half, pair = divmod(t, HALF_TILES)
        return pair, pl.ds(half * BATCH + b, SSM_TC, stride=SUBLANES)

    def input_rows(t, b):
        half, pair = divmod(t, HALF_TILES)
        r0 = (half * BATCH + b) * SSM_PITCH
        return pair, slice(r0, r0 + SSM_TC)

    for b in range(BATCH):
        ub = u_ref[b].astype(BF16)
        bu_re = jnp.dot(ub, bre_ref[...], preferred_element_type=F32)
        bu_im = jnp.dot(ub, bim_ref[...], preferred_element_type=F32)
        for t in range(SLAB_TILES):
            pair, sl = input_rows(t, b)
            bre_s[pair, sl, :] = bu_re[:, t * LANES:(t + 1) * LANES]
            bim_s[pair, sl, :] = bu_im[:, t * LANES:(t + 1) * LANES]

    def coeff(ref, pair):
        return jnp.concatenate(
            [jnp.broadcast_to(ref[half * HALF_TILES + pair:half * HALF_TILES + pair + 1, :],
                              (BATCH, LANES)) for half in range(2)], axis=0)

    a_re = [coeff(are_ref, p) for p in range(HALF_TILES)]
    a_im = [coeff(aim_ref, p) for p in range(HALF_TILES)]

    def step(i, carry):
        xs = list(carry)
        rows = pl.ds(pl.multiple_of(i * SUBLANES, SUBLANES), SUBLANES)
        in_rows = pl.ds(i, SUBLANES, stride=SSM_PITCH)
        for p in range(HALF_TILES):
            xr, xi = xs[2 * p], xs[2 * p + 1]
            nr = a_re[p] * xr - a_im[p] * xi + bre_s[p, in_rows, :]
            ni = a_re[p] * xi + a_im[p] * xr + bim_s[p, in_rows, :]
            sre_ref[p, rows, :] = nr
            sim_ref[p, rows, :] = ni
            xs[2 * p], xs[2 * p + 1] = nr, ni
        return tuple(xs)

    init = []
    for p in range(HALF_TILES):
        init += [xre_ref[p], xim_ref[p]]
    fin = lax.fori_loop(0, SSM_TC, step, tuple(init), unroll=8)
    for p in range(HALF_TILES):
        xre_ref[p] = fin[2 * p]
        xim_ref[p] = fin[2 * p + 1]

    for b in range(BATCH):
        tiles = [plane_rows(t, b) for t in range(SLAB_TILES)]
        x_re = jnp.concatenate([sre_ref[pair, sl, :] for pair, sl in tiles], axis=1)
        x_im = jnp.concatenate([sim_ref[pair, sl, :] for pair, sl in tiles], axis=1)
        y = (_bdot(x_re, cre_ref[...]) + _bdot(x_im, cim_ref[...])
             + d_ref[...] * u_ref[b])
        y_ref[b] = jax.nn.gelu(y).astype(BF16)


def _ssm(u, bb_re, bb_im, cc_re, cc_im, a_re, a_im, d_skip, layer):
    slab = lambda s, c: (layer * SSM_SLABS + s, 0, 0)
    return pl.pallas_call(
        _ssm_kernel,
        grid=(SSM_SLABS, SEQ // SSM_TC),
        in_specs=[
            pl.BlockSpec((BATCH, SSM_TC, SLAB_CH), lambda s, c: (0, c, s)),
            pl.BlockSpec((None, SLAB_CH, SLAB_STATES), slab),
            pl.BlockSpec((None, SLAB_CH, SLAB_STATES), slab),
            pl.BlockSpec((None, SLAB_STATES, SLAB_CH), slab),
            pl.BlockSpec((None, SLAB_STATES, SLAB_CH), slab),
            pl.BlockSpec((None, SLAB_TILES, LANES), slab),
            pl.BlockSpec((None, SLAB_TILES, LANES), slab),
            pl.BlockSpec((None, 1, SLAB_CH), slab),
        ],
        out_specs=pl.BlockSpec((BATCH, SSM_TC, SLAB_CH), lambda s, c: (0, c, s)),
        out_shape=jax.ShapeDtypeStruct((BATCH, SEQ, SSM_WIDTH), BF16),
        scratch_shapes=[
            pltpu.VMEM((HALF_TILES, SUBLANES * SSM_PITCH, LANES), F32),
            pltpu.VMEM((HALF_TILES, SUBLANES * SSM_PITCH, LANES), F32),
            pltpu.VMEM((HALF_TILES, SUBLANES * SSM_TC, LANES), F32),
            pltpu.VMEM((HALF_TILES, SUBLANES * SSM_TC, LANES), F32),
            pltpu.VMEM((HALF_TILES, SUBLANES, LANES), F32),
            pltpu.VMEM((HALF_TILES, SUBLANES, LANES), F32),
        ],
        compiler_params=_cparams(("arbitrary", "arbitrary")),
        name="s5_scan",
    )(u, bb_re, bb_im, cc_re, cc_im, a_re, a_im, d_skip)


def _ssm_matrices(f_re, f_im, b_re, b_im, c_re, c_im):
    n = DEPTH * SSM_GROUPS
    f_re, f_im = f_re[..., None], f_im[..., None]
    b_re = b_re.reshape(n, SSM_STATE, SSM_GROUP)
    b_im = b_im.reshape(n, SSM_STATE, SSM_GROUP)
    bb_re = f_re * b_re - f_im * b_im
    bb_im = f_re * b_im + f_im * b_re
    gps = SSM_GROUPS // SSM_SLABS
    slabs = DEPTH * SSM_SLABS

    def block_diag(blocks):
        _, _, r, c = blocks.shape
        tiled = jnp.tile(blocks.reshape(slabs, gps * r, c), (1, 1, gps))
        on_diag = (np.arange(gps * r)[:, None] // r) == (np.arange(gps * c)[None, :] // c)
        return jnp.where(jnp.asarray(on_diag), tiled, 0.0)

    def in_map(bb):
        bb = bb.reshape(slabs, gps, SSM_STATE, SSM_GROUP)
        return block_diag(jnp.swapaxes(bb, 2, 3))

    def out_map(cc):
        cc = cc.reshape(slabs, gps, SSM_GROUP, SSM_STATE)
        return block_diag(jnp.swapaxes(cc, 2, 3))

    return (in_map(bb_re).astype(BF16), in_map(bb_im).astype(BF16),
            out_map(c_re).astype(BF16), out_map(-c_im).astype(BF16))


def kernel(x, c, w_ada, b_ada, norm_ffn1, w_ffn1_in, w_ffn1_out, norm_mix, w_in, rel_bias, lam_re, lam_im, log_dt, b_re, b_im, c_re, c_im, d_skip, w_glu, w_attn_proj, w_out, norm_ffn2, w_ffn2_in, w_ffn2_out, final_norm):
    c_pad = jnp.zeros((SUBLANES, D_MODEL), F32).at[:BATCH].set(c)
    mod = _ada(c_pad, w_ada, b_ada).reshape(DEPTH, SUBLANES, N_MOD, D_MODEL)
    bias_tabs = _attn_bias_tables(rel_bias)
    a_re, a_im, f_re, f_im = _ssm_params(lam_re, lam_im, log_dt)
    bb_re, bb_im, cc_re, cc_im = _ssm_matrices(f_re, f_im, b_re, b_im, c_re, c_im)
    a_re = a_re.reshape(DEPTH * SSM_SLABS, SLAB_TILES, LANES)
    a_im = a_im.reshape(DEPTH * SSM_SLABS, SLAB_TILES, LANES)
    d_skip = d_skip.reshape(DEPTH * SSM_SLABS, 1, SLAB_CH)
    for l in range(DEPTH):
        x = _ffn(x, mod, norm_ffn1, w_ffn1_in, w_ffn1_out, l, 0)

        *qkv_groups, u, gates = _inproj(x, mod, norm_mix, w_in, l)
        attn_o = _attention([qkv.reshape(BATCH, SEQ, QKV_WIDTH) for qkv in qkv_groups], bias_tabs)
        y_ssm = _ssm(u, bb_re, bb_im, cc_re, cc_im, a_re, a_im, d_skip, l)
        x = _ffn(x, mod, norm_ffn2, w_ffn2_in, w_ffn2_out, l, 6,
                 final_norm=final_norm if l == DEPTH - 1 else None,
                 mixer=(attn_o, y_ssm, gates, w_attn_proj, w_glu, w_out))
    return x
```
